```python
import numpy as np
import jax
import jax.numpy as jnp
from jax import lax

D_MODEL = 2048
BATCH = 4
SEQ = 4096
DEPTH = 2

M_HEADS = 4
M_DQK = 256
M_DV = 512
M_CHUNK = 64
N_KV = 4
N_HPG = 4
N_HEADS = N_KV * N_HPG
N_HD = 128
CMP_BLOCK = 32
CMP_STRIDE = 16
SEL_BLOCK = 64
SEL_TOPK = 16
WINDOW = 512
Q_BLOCK = 32
ROPE_THETA = 10000.0
N_EXPERTS = 32
N_GROUPS = 8
EXP_PER_GROUP = N_EXPERTS // N_GROUPS
TOP_K = 2
D_FF = 1024
ALPHA = (2 * DEPTH) ** 0.25
BETA = (8 * DEPTH) ** -0.25
EPS = 1e-5
NEG = -1e30

M_QK_W = M_HEADS * M_DQK
M_V_W = M_HEADS * M_DV
N_Q_W = N_HEADS * N_HD
N_KV_W = N_KV * N_HD
IN_SIZES = (M_QK_W, M_QK_W, M_V_W, M_V_W, M_HEADS, M_HEADS, N_Q_W, N_KV_W, N_KV_W, N_KV_W, N_KV_W, N_KV_W, N_KV_W, 3 * N_HEADS, 2 * D_MODEL)
IN_WIDTH = sum(IN_SIZES)
F_GATE_OFF = 2 * M_QK_W + 2 * M_V_W + M_HEADS

kernel_name = 'hybrid_mlstm_nsa_grouped_moe_deepnorm'


def _standardize(x):
    xf = x.astype(jnp.float32)
    mu = jnp.mean(xf, -1, keepdims=True)
    var = jnp.mean(jnp.square(xf - mu), -1, keepdims=True)
    return (xf - mu) * lax.rsqrt(var + EPS)


def _layer_norm(x, g, b):
    return (_standardize(x) * g + b).astype(x.dtype)


def _rope(x, pos):
    half = x.shape[-1] // 2
    inv = ROPE_THETA ** (-jnp.arange(half, dtype=jnp.float32) / half)
    ang = pos.astype(jnp.float32)[:, None] * inv[None, :]
    cos, sin = jnp.cos(ang), jnp.sin(ang)
    xf = x.astype(jnp.float32)
    x1, x2 = xf[..., :half], xf[..., half:]
    return jnp.concatenate([x1 * cos - x2 * sin, x2 * cos + x1 * sin], -1).astype(x.dtype)


def _masked_softmax(s, mask):
    s = jnp.where(mask, s, NEG)
    return jax.nn.softmax(s, axis=-1) * mask


def _mlstm(q, k, v, ig, fg):
    B, H, T, _ = q.shape
    L = M_CHUNK
    NC = T // L
    f32 = jnp.float32
    q = q.astype(f32) * (M_DQK ** -0.5)
    k = k.astype(f32)
    v = v.astype(f32)
    lf = jax.nn.log_sigmoid(fg.astype(f32))
    li = ig.astype(f32)

    def to_chunks(a):
        return jnp.moveaxis(a.reshape(a.shape[:2] + (NC, L) + a.shape[3:]), 2, 0)

    causal = jnp.tril(jnp.ones((L, L), dtype=bool))

    def step(carry, inp):
        C, n, m = carry
        qi, ki, vi, lfi, lii = inp
        b = jnp.cumsum(lfi, -1)
        dmat = jnp.where(causal, b[..., :, None] - b[..., None, :] + lii[..., None, :], -jnp.inf)
        inter = b + m[..., None]
        m_t = jnp.maximum(inter, jnp.max(dmat, -1))
        w_inter = jnp.exp(inter - m_t)
        s = jnp.einsum('bhtd,bhsd->bhts', qi, ki) * jnp.exp(dmat - m_t[..., None])
        num = jnp.einsum('bhts,bhsv->bhtv', s, vi) + w_inter[..., None] * jnp.einsum('bhvd,bhtd->bhtv', C, qi)
        den = jnp.sum(s, -1) + w_inter * jnp.einsum('bhd,bhtd->bht', n, qi)
        h = num / jnp.maximum(jnp.abs(den), jnp.exp(-m_t))[..., None]
        b_last = b[..., -1]
        g_s = b_last[..., None] - b + lii
        m_new = jnp.maximum(b_last + m, jnp.max(g_s, -1))
        decay = jnp.exp(b_last + m - m_new)
        ws = jnp.exp(g_s - m_new[..., None])
        C_new = decay[..., None, None] * C + jnp.einsum('bhsv,bhsd->bhvd', vi * ws[..., None], ki)
        n_new = decay[..., None] * n + jnp.einsum('bhs,bhsd->bhd', ws, ki)
        return (C_new, n_new, m_new), h

    init = (jnp.zeros((B, H, M_DV, M_DQK), f32), jnp.zeros((B, H, M_DQK), f32), jnp.zeros((B, H), f32))
    _, hs = lax.scan(step, init, (to_chunks(q), to_chunks(k), to_chunks(v), to_chunks(lf), to_chunks(li)))
    return jnp.moveaxis(hs, 0, 2).reshape(B, H, T, M_DV)


def _compress(x, pos_emb, w1, b1, w2, b2):
    T = x.shape[2]
    n_cmp = (T - CMP_BLOCK) // CMP_STRIDE + 1
    idx = np.arange(n_cmp)[:, None] * CMP_STRIDE + np.arange(CMP_BLOCK)[None, :]
    blocks = x[:, :, idx, :] + pos_emb
    h = jax.nn.gelu(jnp.einsum('bgnld,ldf->bgnf', blocks, w1) + b1)
    return jnp.einsum('bgnf,fd->bgnd', h, w2) + b2


def _nsa(q, k_cmp, v_cmp, k_sel, v_sel, k_win, v_win, gates):
    B, G, HPG, T, hd = q.shape
    n_cmp = k_cmp.shape[2]
    n_sel = T // SEL_BLOCK
    top = min(SEL_TOPK, n_sel)
    nqb = T // Q_BLOCK
    scale = hd ** -0.5
    cmp_start = np.arange(n_cmp) * CMP_STRIDE
    cmp_end = cmp_start + CMP_BLOCK - 1
    sel_start_np = np.arange(n_sel) * SEL_BLOCK
    overlap = jnp.asarray(((cmp_start[:, None] <= sel_start_np[None, :] + SEL_BLOCK - 1) & (cmp_end[:, None] >= sel_start_np[None, :])).astype(np.float32))
    cmp_end_j = jnp.asarray(cmp_end)
    sel_start = jnp.arange(n_sel) * SEL_BLOCK
    blk_ids = jnp.arange(n_sel)
    ks_blk = k_sel.reshape(B, G, n_sel, SEL_BLOCK, hd)
    vs_blk = v_sel.reshape(B, G, n_sel, SEL_BLOCK, hd)
    kw_pad = jnp.pad(k_win, ((0, 0), (0, 0), (WINDOW, 0), (0, 0)))
    vw_pad = jnp.pad(v_win, ((0, 0), (0, 0), (WINDOW, 0), (0, 0)))
    bi = jnp.arange(B)[:, None, None, None]
    gi = jnp.arange(G)[None, :, None, None]
    q_blocks = jnp.moveaxis(q.reshape(B, G, HPG, nqb, Q_BLOCK, hd), 3, 0)
    g_blocks = jnp.moveaxis(gates.reshape(B, G, HPG, nqb, Q_BLOCK, 3), 3, 0)
    t0s = jnp.arange(nqb) * Q_BLOCK

    def one_block(args):
        qb, gb, t0 = args
        t = t0 + jnp.arange(Q_BLOCK)
        s = jnp.einsum('bghqd,bgnd->bghqn', qb, k_cmp).astype(jnp.float32) * scale
        p_c = _masked_softmax(s, cmp_end_j[None, :] <= t[:, None])
        o_c = jnp.einsum('bghqn,bgnd->bghqd', p_c.astype(v_cmp.dtype), v_cmp)
        imp = jnp.einsum('bghqn,ns->bgqs', p_c, overlap)
        cur = t // SEL_BLOCK
        valid_s = sel_start[None, :] <= t[:, None]
        forced = (blk_ids[None, :] == 0) | (blk_ids[None, :] == cur[:, None]) | (blk_ids[None, :] == cur[:, None] - 1)
        score = jnp.where(forced, jnp.inf, jnp.where(valid_s, imp, -jnp.inf))
        _, idx = lax.top_k(score, top)
        kg = ks_blk[bi, gi, idx].reshape(B, G, Q_BLOCK, top * SEL_BLOCK, hd)
        vg = vs_blk[bi, gi, idx].reshape(B, G, Q_BLOCK, top * SEL_BLOCK, hd)
        kpos = (idx[..., None] * SEL_BLOCK + jnp.arange(SEL_BLOCK)).reshape(B, G, Q_BLOCK, top * SEL_BLOCK)
        valid_k = kpos <= t[:, None]
        s = jnp.einsum('bghqd,bgqkd->bghqk', qb, kg).astype(jnp.float32) * scale
        p_s = _masked_softmax(s, valid_k[:, :, None])
        o_s = jnp.einsum('bghqk,bgqkd->bghqd', p_s.astype(vg.dtype), vg)
        kw = lax.dynamic_slice_in_dim(kw_pad, t0, WINDOW + Q_BLOCK, axis=2)
        vw = lax.dynamic_slice_in_dim(vw_pad, t0, WINDOW + Q_BLOCK, axis=2)
        wpos = t0 - WINDOW + jnp.arange(WINDOW + Q_BLOCK)
        dlt = t[:, None] - wpos[None, :]
        valid_w = (dlt >= 0) & (dlt < WINDOW) & (wpos[None, :] >= 0)
        s = jnp.einsum('bghqd,bgkd->bghqk', qb, kw).astype(jnp.float32) * scale
        p_w = _masked_softmax(s, valid_w)
        o_w = jnp.einsum('bghqk,bgkd->bghqd', p_w.astype(vw.dtype), vw)
        o = gb[..., 0:1] * o_c + gb[..., 1:2] * o_s + gb[..., 2:3] * o_w
        return o.astype(qb.dtype)

    out = lax.map(one_block, (q_blocks, g_blocks, t0s))
    return jnp.moveaxis(out, 0, 3).reshape(B, G, HPG, T, hd)


def _mixer(x, w_in, b_in, m_norm_w, cmp_k, cmp_v, w_branch_m, w_branch_n, w_out):
    B, T, _ = x.shape
    pos = jnp.arange(T)
    z = jnp.einsum('btd,dc->btc', x, w_in) + b_in
    (mq, mk, mv, mo, mi, mf, nq, nkc, nvc, nks, nvs, nkw, nvw, ng, gmerge) = jnp.split(z, np.cumsum(IN_SIZES)[:-1].tolist(), axis=-1)

    def heads(a, h):
        return a.reshape(B, T, h, -1).transpose(0, 2, 1, 3)

    h = _mlstm(heads(mq, M_HEADS), heads(mk, M_HEADS), heads(mv, M_HEADS), mi.transpose(0, 2, 1), mf.transpose(0, 2, 1))
    h = _standardize(h) * m_norm_w.reshape(M_HEADS, 1, M_DV)
    h = h.transpose(0, 2, 1, 3).reshape(B, T, M_V_W).astype(x.dtype) * jax.nn.sigmoid(mo)
    y_m = h @ w_branch_m
    def kvh(a):
        return a.reshape(B, T, N_KV, N_HD).transpose(0, 2, 1, 3)
    q = _rope(nq.reshape(B, T, N_KV, N_HPG, N_HD).transpose(0, 2, 3, 1, 4), pos)
    n_cmp = (T - CMP_BLOCK) // CMP_STRIDE + 1
    cmp_pos = jnp.arange(n_cmp) * CMP_STRIDE + CMP_BLOCK - 1
    k_cmp = _rope(_compress(kvh(nkc), *cmp_k), cmp_pos)
    v_cmp = _compress(kvh(nvc), *cmp_v)
    gates = jax.nn.sigmoid(ng.reshape(B, T, N_KV, N_HPG, 3).transpose(0, 2, 3, 1, 4))
    o = _nsa(q, k_cmp, v_cmp, _rope(kvh(nks), pos), kvh(nvs), _rope(kvh(nkw), pos), kvh(nvw), gates)
    y_n = o.transpose(0, 3, 1, 2, 4).reshape(B, T, N_Q_W) @ w_branch_n
    g_m, g_n = jnp.split(jax.nn.sigmoid(gmerge), 2, axis=-1)
    return (g_m * y_m + g_n * y_n) @ w_out


def _moe(x, router_w, router_b, w1, w3, w2):
    N = x.shape[0]
    aff = jax.nn.sigmoid(x.astype(jnp.float32) @ router_w.astype(jnp.float32))
    sel = aff + router_b.astype(jnp.float32)
    grp = sel.reshape(N, N_GROUPS, EXP_PER_GROUP)
    grp_score = jnp.sum(lax.top_k(grp, TOP_K)[0], -1)
    g_idx = jnp.argmax(grp_score, -1)
    in_grp = grp[jnp.arange(N), g_idx]
    _, top_i = lax.top_k(in_grp, TOP_K)
    e_idx = g_idx[:, None] * EXP_PER_GROUP + top_i
    w = jnp.take_along_axis(aff, e_idx, -1)
    w = w / jnp.sum(w, -1, keepdims=True)
    gate = jnp.sum(jax.nn.one_hot(e_idx, N_EXPERTS, dtype=jnp.float32) * w[..., None], 1)
    y = jnp.zeros_like(x)
    for e in range(N_EXPERTS):
        h = jax.nn.silu(x @ w1[e]) * (x @ w3[e])
        y = y + gate[:, e:e + 1].astype(x.dtype) * (h @ w2[e])
    return y


def setup_inputs(seed: int = 0) -> dict:
    key = jax.random.key(seed)
    keys = iter(jax.random.split(key, 32))

    def nrm(shape, scale):
        return jax.random.normal(next(keys), shape, jnp.float32) * scale

    L, D, hd = DEPTH, D_MODEL, N_HD
    x = nrm((BATCH, SEQ, D), 1.0)
    w_in = nrm((L, D, IN_WIDTH), D ** -0.5)
    b_in = nrm((L, IN_WIDTH), 0.02)
    b_in = b_in.at[:, F_GATE_OFF:F_GATE_OFF + M_HEADS].add(jnp.linspace(3.0, 6.0, M_HEADS))
    m_norm_w = 1.0 + nrm((L, M_V_W), 0.02)
    cmp_pos_k = nrm((L, CMP_BLOCK, hd), 0.02)
    cmp_w1_k = nrm((L, CMP_BLOCK, hd, hd), (CMP_BLOCK * hd) ** -0.5)
    cmp_b1_k = nrm((L, hd), 0.02)
    cmp_w2_k = nrm((L, hd, hd), hd ** -0.5)
    cmp_b2_k = nrm((L, hd), 0.02)
    cmp_pos_v = nrm((L, CMP_BLOCK, hd), 0.02)
    cmp_w1_v = nrm((L, CMP_BLOCK, hd, hd), (CMP_BLOCK * hd) ** -0.5)
    cmp_b1_v = nrm((L, hd), 0.02)
    cmp_w2_v = nrm((L, hd, hd), hd ** -0.5)
    cmp_b2_v = nrm((L, hd), 0.02)
    w_branch_m = nrm((L, M_V_W, D), M_V_W ** -0.5)
    w_branch_n = nrm((L, N_Q_W, D), N_Q_W ** -0.5)
    w_out = nrm((L, D, D), BETA * D ** -0.5)
    ln1_g = 1.0 + nrm((L, D), 0.02)
    ln1_b = nrm((L, D), 0.02)
    router_w = nrm((D, N_EXPERTS), D ** -0.5)
    router_b = nrm((N_EXPERTS,), 0.01)
    exp_w1 = nrm((L, N_EXPERTS, D, D_FF), D ** -0.5)
    exp_w3 = nrm((L, N_EXPERTS, D, D_FF), D ** -0.5)
    exp_w2 = nrm((L, N_EXPERTS, D_FF, D), BETA * D_FF ** -0.5)
    ln2_g = 1.0 + nrm((L, D), 0.02)
    ln2_b = nrm((L, D), 0.02)
    return {'x': x, 'w_in': w_in, 'b_in': b_in, 'm_norm_w': m_norm_w,
            'cmp_pos_k': cmp_pos_k, 'cmp_w1_k': cmp_w1_k, 'cmp_b1_k': cmp_b1_k, 'cmp_w2_k': cmp_w2_k, 'cmp_b2_k': cmp_b2_k,
            'cmp_pos_v': cmp_pos_v, 'cmp_w1_v': cmp_w1_v, 'cmp_b1_v': cmp_b1_v, 'cmp_w2_v': cmp_w2_v, 'cmp_b2_v': cmp_b2_v,
            'w_branch_m': w_branch_m, 'w_branch_n': w_branch_n, 'w_out': w_out, 'ln1_g': ln1_g, 'ln1_b': ln1_b,
            'router_w': router_w, 'router_b': router_b, 'exp_w1': exp_w1, 'exp_w3': exp_w3, 'exp_w2': exp_w2,
            'ln2_g': ln2_g, 'ln2_b': ln2_b}


def reference(x, w_in, b_in, m_norm_w, cmp_pos_k, cmp_w1_k, cmp_b1_k, cmp_w2_k, cmp_b2_k,
              cmp_pos_v, cmp_w1_v, cmp_b1_v, cmp_w2_v, cmp_b2_v, w_branch_m, w_branch_n, w_out,
              ln1_g, ln1_b, router_w, router_b, exp_w1, exp_w3, exp_w2, ln2_g, ln2_b):
    B, T, D = x.shape
    for l in range(DEPTH):
        cmp_k = (cmp_pos_k[l], cmp_w1_k[l], cmp_b1_k[l], cmp_w2_k[l], cmp_b2_k[l])
        cmp_v = (cmp_pos_v[l], cmp_w1_v[l], cmp_b1_v[l], cmp_w2_v[l], cmp_b2_v[l])
        mix = _mixer(x, w_in[l], b_in[l], m_norm_w[l], cmp_k, cmp_v, w_branch_m[l], w_branch_n[l], w_out[l])
        x = _layer_norm(ALPHA * x + mix, ln1_g[l], ln1_b[l])
        ff = _moe(x.reshape(B * T, D), router_w, router_b, exp_w1[l], exp_w3[l], exp_w2[l]).reshape(B, T, D)
        x = _layer_norm(ALPHA * x + ff, ln2_g[l], ln2_b[l])
    return x
```

```python
import functools

import numpy as np
import jax
import jax.numpy as jnp
from jax import lax
from jax.experimental import pallas as pl
from jax.experimental.pallas import tpu as pltpu

F32 = jnp.float32
BF16 = jnp.bfloat16
HIGHEST = lax.Precision.HIGHEST

D_MODEL = 2048
DEPTH = 2
M_HEADS = 4
M_DQK = 256
M_DV = 512
N_KV = 4
N_HPG = 4
N_HEADS = N_KV * N_HPG
N_HD = 128
CMP_BLOCK = 32
CMP_STRIDE = 16
SEL_BLOCK = 64
SEL_TOPK = 16
WINDOW = 512
ROPE_THETA = 10000.0
N_EXPERTS = 32
N_GROUPS = 8
EXP_PER_GROUP = N_EXPERTS // N_GROUPS
D_FF = 1024
ALPHA = (2 * DEPTH) ** 0.25
EPS = 1e-5
NEG = -1e30

M_QK_W = M_HEADS * M_DQK
M_V_W = M_HEADS * M_DV
N_Q_W = N_HEADS * N_HD
N_KV_W = N_KV * N_HD
IN_SIZES = (M_QK_W, M_QK_W, M_V_W, M_V_W, M_HEADS, M_HEADS, N_Q_W, N_KV_W, N_KV_W, N_KV_W, N_KV_W,
            N_KV_W, N_KV_W, 3 * N_HEADS, 2 * D_MODEL)
IN_OFFS = tuple(int(v) for v in np.cumsum((0,) + IN_SIZES))

LANES = 128
MLSTM_CHUNK = 256
NSA_TQ = 256
MM_TM = 512
MM_TN = 1024
MOE_TM = 256
MOE_TD = 256
ROUTER_TM = 512

P1_MQ, P1_MK, P1_MV, P1_MO, P1_NVS, P1_NVW, P1_GM = 0, 1024, 2048, 4096, 6144, 6656, 7168
P1_W = 7168 + 2 * D_MODEL
P2_NQ, P2_NKS, P2_NKW = 0, 2048, 2560
P2_W = 3072
P3_GATES, P3_NG, P3_NKC, P3_NVC = 0, 128, 640, 1152
P3_W = 1664


def _params(sem, vmem_mb):
    return pltpu.CompilerParams(dimension_semantics=sem, vmem_limit_bytes=vmem_mb * 1024 * 1024)


def _nt(a, b, **kw):
    return lax.dot_general(a, b, (((1,), (1,)), ((), ())), preferred_element_type=F32, **kw)


def _tn(a, b):
    return lax.dot_general(a, b, (((0,), (0,)), ((), ())), preferred_element_type=F32)


def _standardize(x):
    mu = jnp.mean(x, -1, keepdims=True)
    xc = x - mu
    var = jnp.mean(xc * xc, -1, keepdims=True)
    return xc * lax.rsqrt(var + EPS)


def _mm_body(x_ref, w_ref, b_ref, *rest, rope, tn):
    acc = jnp.dot(x_ref[...], w_ref[...], preferred_element_type=F32) + b_ref[...]
    if rope:
        cos_ref, sin_ref, o_ref = rest
        cos = cos_ref[...]
        sin = sin_ref[...]
        for c in range(tn // N_HD):
            ch = acc[:, c * N_HD:(c + 1) * N_HD]
            o_ref[:, c * N_HD:(c + 1) * N_HD] = (ch * cos + pltpu.roll(ch, N_HD // 2, 1) * sin).astype(o_ref.dtype)
    else:
        (o_ref,) = rest
        o_ref[...] = acc.astype(o_ref.dtype)


def _matmul(x, w, b, out_dtype, tn, rope=None, seq=None):
    m, k = x.shape
    n = w.shape[1]
    tm = MM_TM
    grid = (n // tn, m // tm)
    in_specs = [pl.BlockSpec((tm, k), lambda j, i: (i, 0)),
                pl.BlockSpec((k, tn), lambda j, i: (0, j)),
                pl.BlockSpec((1, tn), lambda j, i: (0, j))]
    args = [x, w, b.reshape(1, n)]
    if rope is not None:
        nt = seq // tm
        in_specs += [pl.BlockSpec((tm, N_HD), lambda j, i: (i % nt, 0))] * 2
        args += list(rope)
    return pl.pallas_call(
        functools.partial(_mm_body, rope=rope is not None, tn=tn),
        grid=grid, in_specs=in_specs,
        out_specs=pl.BlockSpec((tm, tn), lambda j, i: (i, j)),
        out_shape=jax.ShapeDtypeStruct((m, n), out_dtype),
        compiler_params=_params(("parallel", "parallel"), 48),
        name="proj_matmul",
    )(*args)


def _log_sigmoid(x):
    return jnp.minimum(x, 0.0) - jnp.log1p(jnp.exp(-jnp.abs(x)))


def _mlstm_body(q_ref, k_ref, v_ref, og_ref, g_ref, gt_ref, nw_ref, out_ref, c_ref, n_ref, m_ref, *, L):
    @pl.when(pl.program_id(1) == 0)
    def _():
        c_ref[...] = jnp.zeros_like(c_ref)
        n_ref[...] = jnp.zeros_like(n_ref)
        m_ref[...] = jnp.zeros_like(m_ref)

    g = g_ref[...]
    gt = gt_ref[...]
    row = lax.broadcasted_iota(jnp.int32, (L, L), 0)
    col = lax.broadcasted_iota(jnp.int32, (L, L), 1)
    causal = row >= col
    b_all = jnp.dot(causal.astype(F32), _log_sigmoid(g), precision=HIGHEST, preferred_element_type=F32)
    bt_all = jnp.dot(_log_sigmoid(gt), (row <= col).astype(F32), precision=HIGHEST, preferred_element_type=F32)
    for h in range(M_HEADS):
        b_col = b_all[:, M_HEADS + h:M_HEADS + h + 1]
        li_col = g[:, h:h + 1]
        b_row = bt_all[M_HEADS + h:M_HEADS + h + 1, :]
        li_row = gt[h:h + 1, :]
        m_prev = m_ref[h][:, 0:1]
        dmat = jnp.where(causal, b_col - b_row + li_row, -jnp.inf)
        inter = b_col + m_prev
        m_t = jnp.maximum(inter, jnp.max(dmat, axis=1, keepdims=True))
        w_inter = jnp.exp(inter - m_t)
        q = q_ref[:, h * M_DQK:(h + 1) * M_DQK]
        k = k_ref[:, h * M_DQK:(h + 1) * M_DQK]
        v = v_ref[:, h * M_DV:(h + 1) * M_DV]
        s = _nt(q, k) * jnp.exp(dmat - m_t)
        ct = c_ref[h]
        n_row = n_ref[h]
        num = jnp.dot(s.astype(BF16), v, preferred_element_type=F32) + w_inter * jnp.dot(
            q, ct.astype(BF16), preferred_element_type=F32)
        qn = jnp.sum(q.astype(F32) * n_row, axis=1, keepdims=True)
        den = jnp.sum(s, axis=1, keepdims=True) + w_inter * qn
        hh = num / jnp.maximum(jnp.abs(den), jnp.exp(-m_t))
        hn = _standardize(hh) * nw_ref[:, h * M_DV:(h + 1) * M_DV]
        og = og_ref[:, h * M_DV:(h + 1) * M_DV].astype(F32)
        out_ref[:, h * M_DV:(h + 1) * M_DV] = (hn * jax.nn.sigmoid(og)).astype(out_ref.dtype)
        b_last = b_col[L - 1:L, :]
        g_col = b_last - b_col + li_col
        g_row = b_last - b_row + li_row
        m_new = jnp.maximum(b_last + m_prev, jnp.max(g_row, axis=1, keepdims=True))
        decay = jnp.exp(b_last + m_prev - m_new)
        ws_col = jnp.exp(g_col - m_new)
        kf = k.astype(F32)
        vw = (v.astype(F32) * ws_col).astype(BF16)
        c_ref[h] = decay * ct + _tn(k, vw)
        n_ref[h] = decay * n_row + jnp.sum(kf * ws_col, axis=0, keepdims=True)
        m_ref[h] = jnp.broadcast_to(m_new, (1, LANES))


def _mlstm(p1, p3, gt, norm_w, B, T):
    L = MLSTM_CHUNK
    p1 = p1.reshape(B, T, P1_W)
    p3 = p3.reshape(B, T, P3_W)
    return pl.pallas_call(
        functools.partial(_mlstm_body, L=L),
        grid=(B, T // L),
        in_specs=[pl.BlockSpec((None, L, M_QK_W), lambda b, c: (b, c, P1_MQ // M_QK_W)),
                  pl.BlockSpec((None, L, M_QK_W), lambda b, c: (b, c, P1_MK // M_QK_W)),
                  pl.BlockSpec((None, L, M_V_W), lambda b, c: (b, c, P1_MV // M_V_W)),
                  pl.BlockSpec((None, L, M_V_W), lambda b, c: (b, c, P1_MO // M_V_W)),
                  pl.BlockSpec((None, L, LANES), lambda b, c: (b, c, 0)),
                  pl.BlockSpec((None, 8, L), lambda b, c: (b, 0, c)),
                  pl.BlockSpec((1, M_V_W), lambda b, c: (0, 0))],
        out_specs=pl.BlockSpec((None, L, M_V_W), lambda b, c: (b, c, 0)),
        out_shape=jax.ShapeDtypeStruct((B, T, M_V_W), BF16),
        scratch_shapes=[pltpu.VMEM((M_HEADS, M_DQK, M_DV), F32),
                        pltpu.VMEM((M_HEADS, 1, M_DQK), F32),
                        pltpu.VMEM((M_HEADS, 1, LANES), F32)],
        compiler_params=_params(("parallel", "arbitrary"), 48),
        name="mlstm",
    )(p1, p1, p1, p1, p3, gt, norm_w.reshape(1, M_V_W))


def _compress_body(x_ref, pos_ref, w1_ref, b1_ref, w2_ref, b2_ref, cos_ref, sin_ref, o_ref, *, rope, nb):
    x = x_ref[...]
    half = CMP_STRIDE * N_HD
    lo = jnp.dot((x + pos_ref[:, :half]).astype(BF16), w1_ref[:half, :], preferred_element_type=F32)
    hi = jnp.dot((x + pos_ref[:, half:]).astype(BF16), w1_ref[half:, :], preferred_element_type=F32)
    pre = lo + pltpu.roll(hi, nb - 1, 0) + b1_ref[...]
    h = jax.nn.gelu(pre)
    y = jnp.dot(h.astype(BF16), w2_ref[...], preferred_element_type=F32) + b2_ref[...]
    if rope:
        y = y * cos_ref[...] + pltpu.roll(y, N_HD // 2, 1) * sin_ref[...]
    keep = lax.broadcasted_iota(jnp.int32, (nb, N_HD), 0) < nb - 1
    o_ref[...] = jnp.where(keep, y, 0.0).astype(o_ref.dtype)


def _compress(x2, pos, w1, b1, w2, b2, cos, sin, rope):
    B, G, nb, _ = x2.shape
    full = lambda shape: pl.BlockSpec(shape, lambda b, g: (0,) * len(shape))
    return pl.pallas_call(
        functools.partial(_compress_body, rope=rope, nb=nb),
        grid=(B, G),
        in_specs=[pl.BlockSpec((None, None, nb, CMP_STRIDE * N_HD), lambda b, g: (b, g, 0, 0)),
                  full((1, CMP_BLOCK * N_HD)), full((CMP_BLOCK * N_HD, N_HD)), full((1, N_HD)),
                  full((N_HD, N_HD)), full((1, N_HD)), full((nb, N_HD)), full((nb, N_HD))],
        out_specs=pl.BlockSpec((None, None, nb, N_HD), lambda b, g: (b, g, 0, 0)),
        out_shape=jax.ShapeDtypeStruct((B, G, nb, N_HD), BF16),
        compiler_params=_params(("parallel", "parallel"), 32),
        name="nsa_compress",
    )(x2, pos.reshape(1, -1), w1.reshape(CMP_BLOCK * N_HD, N_HD).astype(BF16), b1.reshape(1, N_HD),
      w2.astype(BF16), b2.reshape(1, N_HD), cos, sin)


def _nsa_body(q_ref, ks_ref, vs_ref, kw_ref, vw_ref, kc_ref, vc_ref, g_ref, ovt_ref, o_ref,
              acc_ref, m_ref, l_ref, *, tq, ncmp):
    qi = pl.program_id(2)
    t0 = qi * tq
    t_col = t0 + lax.broadcasted_iota(jnp.int32, (tq, 1), 0)
    nsel = ovt_ref.shape[0]

    kc = kc_ref[...]
    vc = vc_ref[...]
    n_idx = lax.broadcasted_iota(jnp.int32, (tq, ncmp), 1)
    cmask = (n_idx * CMP_STRIDE + (CMP_BLOCK - 1) <= t_col) & (n_idx < ncmp - 1)
    psum = jnp.zeros((tq, ncmp), F32)
    o_cmp = []
    for h in range(N_HPG):
        s = jnp.where(cmask, _nt(q_ref[:, h * N_HD:(h + 1) * N_HD], kc), NEG)
        e = jnp.where(cmask, jnp.exp(s - jnp.max(s, axis=1, keepdims=True)), 0.0)
        p = e / jnp.maximum(jnp.sum(e, axis=1, keepdims=True), 1e-30)
        psum = psum + p
        o_cmp.append(jnp.dot(p.astype(BF16), vc, preferred_element_type=F32))

    imp = _nt(ovt_ref[...], psum, precision=HIGHEST)
    j_idx = lax.broadcasted_iota(jnp.int32, (nsel, tq), 0)
    t_row = t0 + lax.broadcasted_iota(jnp.int32, (nsel, tq), 1)
    cur = lax.shift_right_logical(t_row, 6)
    forced = (j_idx == 0) | (j_idx == cur) | (j_idx == cur - 1)
    score = jnp.where(forced, jnp.inf, jnp.where(j_idx * SEL_BLOCK <= t_row, imp, -jnp.inf))

    def pick(_, carry):
        score, sel = carry
        mx = jnp.max(score, axis=0, keepdims=True)
        idx = jnp.min(jnp.where(score == mx, j_idx, nsel), axis=0, keepdims=True)
        hit = j_idx == idx
        return jnp.where(hit, -jnp.inf, score), jnp.where(hit, 1.0, sel)

    _, sel = lax.fori_loop(0, min(SEL_TOPK, nsel), pick, (score, jnp.zeros((nsel, tq), F32)), unroll=True)
    sel_tok = sel.T.astype(BF16)

    def flash_init():
        acc_ref[...] = jnp.zeros_like(acc_ref)
        l_ref[...] = jnp.zeros_like(l_ref)
        m_ref[...] = jnp.full(m_ref.shape, NEG, F32)

    def flash_tile(k, v, mask):
        for h in range(N_HPG):
            s = jnp.where(mask, _nt(q_ref[:, h * N_HD:(h + 1) * N_HD], k), NEG)
            m_old = m_ref[h]
            m_new = jnp.maximum(m_old, jnp.max(s, axis=1, keepdims=True))
            p = jnp.where(mask, jnp.exp(s - m_new), 0.0)
            alpha = jnp.exp(m_old - m_new)
            l_ref[h] = alpha * l_ref[h] + jnp.sum(p, axis=1, keepdims=True)
            acc_ref[h] = alpha * acc_ref[h] + jnp.dot(p.astype(BF16), v, preferred_element_type=F32)
            m_ref[h] = m_new

    def flash_out():
        return [acc_ref[h] / l_ref[h] for h in range(N_HPG)]

    flash_init()
    blk_per_tile = tq // SEL_BLOCK

    def sel_step(j, carry):
        start = pl.multiple_of(j * tq, tq)
        jj = lax.broadcasted_iota(jnp.int32, (nsel, tq), 0)
        kk = lax.broadcasted_iota(jnp.int32, (nsel, tq), 1)
        expand = (jj == j * blk_per_tile + lax.shift_right_logical(kk, 6)).astype(BF16)
        chosen = jnp.dot(sel_tok, expand, preferred_element_type=F32)
        kpos = j * tq + lax.broadcasted_iota(jnp.int32, (tq, tq), 1)
        flash_tile(ks_ref[pl.ds(start, tq), :], vs_ref[pl.ds(start, tq), :], (chosen > 0.5) & (kpos <= t_col))
        return carry

    lax.fori_loop(0, qi + 1, sel_step, 0)
    o_sel = flash_out()

    flash_init()
    for back in range(WINDOW // tq + 1):
        @pl.when(qi - back >= 0)
        def _():
            j = qi - back
            start = pl.multiple_of(j * tq, tq)
            dlt = t_col - (j * tq + lax.broadcasted_iota(jnp.int32, (tq, tq), 1))
            flash_tile(kw_ref[pl.ds(start, tq), :], vw_ref[pl.ds(start, tq), :], (dlt >= 0) & (dlt < WINDOW))
    o_win = flash_out()

    gates = jax.nn.sigmoid(g_ref[...])
    for h in range(N_HPG):
        o = (gates[:, 3 * h:3 * h + 1] * o_cmp[h] + gates[:, 3 * h + 1:3 * h + 2] * o_sel[h]
             + gates[:, 3 * h + 2:3 * h + 3] * o_win[h])
        o_ref[:, h * N_HD:(h + 1) * N_HD] = o.astype(o_ref.dtype)


def _nsa(p1, p2, p3, k_cmp, v_cmp, ovt, B, T):
    tq = NSA_TQ
    ncmp = k_cmp.shape[2]
    p1 = p1.reshape(B, T, P1_W)
    p2 = p2.reshape(B, T, P2_W)
    p3 = p3.reshape(B, T, P3_W)
    hw = N_HPG * N_HD
    kv = lambda off: pl.BlockSpec((None, T, N_HD), lambda b, g, i: (b, 0, off // N_HD + g))
    cmp_spec = pl.BlockSpec((None, None, ncmp, N_HD), lambda b, g, i: (b, g, 0, 0))
    return pl.pallas_call(
        functools.partial(_nsa_body, tq=tq, ncmp=ncmp),
        grid=(B, N_KV, T // tq),
        in_specs=[pl.BlockSpec((None, tq, hw), lambda b, g, i: (b, i, P2_NQ // hw + g)),
                  kv(P2_NKS), kv(P1_NVS), kv(P2_NKW), kv(P1_NVW), cmp_spec, cmp_spec,
                  pl.BlockSpec((None, tq, LANES), lambda b, g, i: (b, i, P3_NG // LANES + g)),
                  pl.BlockSpec(ovt.shape, lambda b, g, i: (0, 0))],
        out_specs=pl.BlockSpec((None, tq, hw), lambda b, g, i: (b, i, g)),
        out_shape=jax.ShapeDtypeStruct((B, T, N_Q_W), BF16),
        scratch_shapes=[pltpu.VMEM((N_HPG, tq, N_HD), F32), pltpu.VMEM((N_HPG, tq, 1), F32),
                        pltpu.VMEM((N_HPG, tq, 1), F32)],
        compiler_params=_params(("parallel", "parallel", "arbitrary"), 48),
        name="nsa_attention",
    )(p2, p2, p1, p2, p1, k_cmp, v_cmp, p3, ovt)


def _merge_body(hm_ref, hn_ref, wm_ref, wn_ref, gm_ref, gn_ref, o_ref):
    ym = jnp.dot(hm_ref[...], wm_ref[...], preferred_element_type=F32)
    yn = jnp.dot(hn_ref[...], wn_ref[...], preferred_element_type=F32)
    gm = jax.nn.sigmoid(gm_ref[...].astype(F32))
    gn = jax.nn.sigmoid(gn_ref[...].astype(F32))
    o_ref[...] = (gm * ym + gn * yn).astype(o_ref.dtype)


def _merge(hm, hn, wm, wn, p1):
    m = hm.shape[0]
    tm, tn = MM_TM, 512
    return pl.pallas_call(
        _merge_body,
        grid=(D_MODEL // tn, m // tm),
        in_specs=[pl.BlockSpec((tm, M_V_W), lambda j, i: (i, 0)),
                  pl.BlockSpec((tm, N_Q_W), lambda j, i: (i, 0)),
                  pl.BlockSpec((M_V_W, tn), lambda j, i: (0, j)),
                  pl.BlockSpec((N_Q_W, tn), lambda j, i: (0, j)),
                  pl.BlockSpec((tm, tn), lambda j, i: (i, P1_GM // tn + j)),
                  pl.BlockSpec((tm, tn), lambda j, i: (i, (P1_GM + D_MODEL) // tn + j))],
        out_specs=pl.BlockSpec((tm, tn), lambda j, i: (i, j)),
        out_shape=jax.ShapeDtypeStruct((m, D_MODEL), BF16),
        compiler_params=_params(("parallel", "parallel"), 48),
        name="branch_merge",
    )(hm, hn, wm, wn, p1, p1)


def _out_ln_body(y_ref, w_ref, x_ref, g_ref, b_ref, o_ref, ob_ref):
    mix = jnp.dot(y_ref[...], w_ref[...], preferred_element_type=F32)
    out = _standardize(ALPHA * x_ref[...] + mix) * g_ref[...] + b_ref[...]
    o_ref[...] = out
    ob_ref[...] = out.astype(BF16)


def _out_ln(y, w, x, g, b):
    m = y.shape[0]
    tm = MM_TM
    row = pl.BlockSpec((tm, D_MODEL), lambda i: (i, 0))
    vec = pl.BlockSpec((1, D_MODEL), lambda i: (0, 0))
    return pl.pallas_call(
        _out_ln_body,
        grid=(m // tm,),
        in_specs=[row, pl.BlockSpec((D_MODEL, D_MODEL), lambda i: (0, 0)), row, vec, vec],
        out_specs=[row, row],
        out_shape=[jax.ShapeDtypeStruct((m, D_MODEL), F32), jax.ShapeDtypeStruct((m, D_MODEL), BF16)],
        compiler_params=_params(("parallel",), 48),
        name="out_proj_layernorm",
    )(y, w, x, g.reshape(1, -1), b.reshape(1, -1))


def _first_of4(vals, target):
    return jnp.where(vals[0] == target, 0.0, jnp.where(vals[1] == target, 1.0, jnp.where(vals[2] == target, 2.0, 3.0)))


def _select4(idx, vals):
    return jnp.where(idx == 0.0, vals[0], jnp.where(idx == 1.0, vals[1], jnp.where(idx == 2.0, vals[2], vals[3])))


def _router_body(x_ref, rw_ref, rb_ref, o_ref, cnt_ref, carry_ref, *, tm):
    @pl.when(pl.program_id(0) == 0)
    def _():
        carry_ref[...] = jnp.zeros_like(carry_ref)

    logits = _nt(rw_ref[...], x_ref[...], precision=HIGHEST)
    aff = jax.nn.sigmoid(logits)
    biased = aff + rb_ref[:, 0:1]
    a = [biased[i * N_GROUPS:(i + 1) * N_GROUPS, :] for i in range(EXP_PER_GROUP)]
    af = [aff[i * N_GROUPS:(i + 1) * N_GROUPS, :] for i in range(EXP_PER_GROUP)]
    m1 = jnp.maximum(jnp.maximum(a[0], a[1]), jnp.maximum(a[2], a[3]))
    i1 = _first_of4(a, m1)
    rest = [jnp.where(i1 == float(i), -jnp.inf, a[i]) for i in range(EXP_PER_GROUP)]
    m2 = jnp.maximum(jnp.maximum(rest[0], rest[1]), jnp.maximum(rest[2], rest[3]))
    i2 = _first_of4(rest, m2)
    gscore = m1 + m2
    g_iota = lax.broadcasted_iota(jnp.int32, (N_GROUPS, tm), 0).astype(F32)
    g_idx = jnp.min(jnp.where(gscore == jnp.max(gscore, axis=0, keepdims=True), g_iota, float(N_GROUPS)),
                    axis=0, keepdims=True)
    in_g = g_iota == g_idx
    take = lambda v: jnp.sum(jnp.where(in_g, v, 0.0), axis=0, keepdims=True)
    s0 = take(i1)
    s1 = take(i2)
    w0 = take(_select4(i1, af))
    w1 = take(_select4(i2, af))
    wsum = w0 + w1
    r0 = s0 * N_GROUPS + g_idx
    r1 = s1 * N_GROUPS + g_idx
    r_iota = lax.broadcasted_iota(jnp.int32, (N_EXPERTS, tm), 0).astype(F32)
    member = (r_iota == r0) | (r_iota == r1)
    tt = lax.broadcasted_iota(jnp.int32, (tm, tm), 0)
    tc = lax.broadcasted_iota(jnp.int32, (tm, tm), 1)
    before = jnp.dot(member.astype(BF16), (tt < tc).astype(BF16), preferred_element_type=F32)
    base = before + carry_ref[:, 0:1]
    rank0 = jnp.sum(jnp.where(r_iota == r0, base, 0.0), axis=0, keepdims=True)
    rank1 = jnp.sum(jnp.where(r_iota == r1, base, 0.0), axis=0, keepdims=True)
    new_carry = carry_ref[:, 0:1] + jnp.sum(member.astype(F32), axis=1, keepdims=True)
    carry_ref[...] = jnp.broadcast_to(new_carry, carry_ref.shape)
    cnt_ref[...] = jnp.broadcast_to(new_carry, cnt_ref.shape)
    o_ref[0:1, :] = g_idx * EXP_PER_GROUP + s0
    o_ref[1:2, :] = g_idx * EXP_PER_GROUP + s1
    o_ref[2:3, :] = w0 / wsum
    o_ref[3:4, :] = w1 / wsum
    o_ref[4:5, :] = rank0
    o_ref[5:6, :] = rank1
    o_ref[6:8, :] = jnp.zeros((2, tm), F32)


def _router(x, rw_t, rb):
    m = x.shape[0]
    tm = ROUTER_TM
    return pl.pallas_call(
        functools.partial(_router_body, tm=tm),
        grid=(m // tm,),
        in_specs=[pl.BlockSpec((tm, D_MODEL), lambda i: (i, 0)),
                  pl.BlockSpec((N_EXPERTS, D_MODEL), lambda i: (0, 0)),
                  pl.BlockSpec((N_EXPERTS, LANES), lambda i: (0, 0))],
        out_specs=[pl.BlockSpec((8, tm), lambda i: (0, i)),
                   pl.BlockSpec((N_EXPERTS, LANES), lambda i: (0, 0))],
        out_shape=[jax.ShapeDtypeStruct((8, m), F32), jax.ShapeDtypeStruct((N_EXPERTS, LANES), F32)],
        scratch_shapes=[pltpu.VMEM((N_EXPERTS, LANES), F32)],
        compiler_params=_params(("arbitrary",), 48),
        name="moe_router",
    )(x, rw_t, rb)


def _row_copy(src_ref, src_row, dst_ref, dst_row, sem):
    return pltpu.make_async_copy(src_ref.at[pl.ds(src_row, 1), :], dst_ref.at[pl.ds(dst_row, 1), :], sem)


def _dispatch_body(dest_ref, x_ref, init_ref, xs_ref, sem, *, td):
    del init_ref

    def issue(r, c):
        _row_copy(x_ref, r, xs_ref, dest_ref[0, 0, r], sem).start()
        _row_copy(x_ref, r, xs_ref, dest_ref[0, 0, td + r], sem).start()
        return c

    lax.fori_loop(0, td, issue, 0)

    def drain(r, c):
        _row_copy(x_ref, 0, xs_ref, 0, sem).wait()
        _row_copy(x_ref, 0, xs_ref, 0, sem).wait()
        return c

    lax.fori_loop(0, td, drain, 0)


def _dispatch(x, dest, rows):
    m = x.shape[0]
    td = MOE_TD
    return pl.pallas_call(
        functools.partial(_dispatch_body, td=td),
        grid=(m // td,),
        in_specs=[pl.BlockSpec((1, 1, 2 * td), lambda i: (i, 0, 0), memory_space=pltpu.SMEM),
                  pl.BlockSpec((td, D_MODEL), lambda i: (i, 0)),
                  pl.BlockSpec(memory_space=pl.ANY)],
        out_specs=pl.BlockSpec(memory_space=pl.ANY),
        out_shape=jax.ShapeDtypeStruct((rows, D_MODEL), F32),
        scratch_shapes=[pltpu.SemaphoreType.DMA(())],
        input_output_aliases={2: 0},
        compiler_params=_params(("arbitrary",), 32),
        name="moe_dispatch",
    )(dest, x, jnp.zeros((rows, D_MODEL), F32))


def _experts_body(te_ref, nused_ref, x_ref, w1_ref, w3_ref, w2_ref, o_ref):
    del te_ref

    @pl.when(pl.program_id(0) < nused_ref[0])
    def _():
        xb = x_ref[...].astype(BF16)
        a = jnp.dot(xb, w1_ref[...], preferred_element_type=F32)
        b = jnp.dot(xb, w3_ref[...], preferred_element_type=F32)
        h = (a * jax.nn.sigmoid(a) * b).astype(BF16)
        o_ref[...] = jnp.dot(h, w2_ref[...], preferred_element_type=F32)

    @pl.when(pl.program_id(0) >= nused_ref[0])
    def _():
        o_ref[...] = jnp.zeros_like(o_ref)


def _experts(xs, w1, w3, w2, tile_expert, n_used):
    rows = xs.shape[0]
    tm = MOE_TM
    grid_spec = pltpu.PrefetchScalarGridSpec(
        num_scalar_prefetch=2,
        grid=(rows // tm,),
        in_specs=[pl.BlockSpec((tm, D_MODEL), lambda i, te, nu: (i, 0)),
                  pl.BlockSpec((None, D_MODEL, D_FF), lambda i, te, nu: (te[i], 0, 0)),
                  pl.BlockSpec((None, D_MODEL, D_FF), lambda i, te, nu: (te[i], 0, 0)),
                  pl.BlockSpec((None, D_FF, D_MODEL), lambda i, te, nu: (te[i], 0, 0))],
        out_specs=pl.BlockSpec((tm, D_MODEL), lambda i, te, nu: (i, 0)))
    return pl.pallas_call(
        _experts_body,
        grid_spec=grid_spec,
        out_shape=jax.ShapeDtypeStruct((rows, D_MODEL), F32),
        compiler_params=_params(("arbitrary",), 56),
        name="moe_experts",
    )(tile_expert, n_used, xs, w1, w3, w2)


def _combine_body(dest_ref, ys_ref, x_ref, w_ref, g_ref, b_ref, o_ref, ob_ref, buf_ref, sem, *, td):
    def issue(r, c):
        _row_copy(ys_ref, dest_ref[0, 0, r], buf_ref.at[0], r, sem).start()
        _row_copy(ys_ref, dest_ref[0, 0, td + r], buf_ref.at[1], r, sem).start()
        return c

    lax.fori_loop(0, td, issue, 0)

    def drain(r, c):
        _row_copy(ys_ref, 0, buf_ref.at[0], 0, sem).wait()
        _row_copy(ys_ref, 0, buf_ref.at[1], 0, sem).wait()
        return c

    lax.fori_loop(0, td, drain, 0)
    w = w_ref[...]
    ff = w[:, 0:1] * buf_ref[0] + w[:, 1:2] * buf_ref[1]
    out = _standardize(ALPHA * x_ref[...] + ff) * g_ref[...] + b_ref[...]
    o_ref[...] = out
    ob_ref[...] = out.astype(BF16)


def _combine(ys, dest, x, w, g, b):
    m = x.shape[0]
    td = MOE_TD
    row = pl.BlockSpec((td, D_MODEL), lambda i: (i, 0))
    vec = pl.BlockSpec((1, D_MODEL), lambda i: (0, 0))
    return pl.pallas_call(
        functools.partial(_combine_body, td=td),
        grid=(m // td,),
        in_specs=[pl.BlockSpec((1, 1, 2 * td), lambda i: (i, 0, 0), memory_space=pltpu.SMEM),
                  pl.BlockSpec(memory_space=pl.ANY), row,
                  pl.BlockSpec((td, 8), lambda i: (i, 0)), vec, vec],
        out_specs=[row, row],
        out_shape=[jax.ShapeDtypeStruct((m, D_MODEL), F32), jax.ShapeDtypeStruct((m, D_MODEL), BF16)],
        scratch_shapes=[pltpu.VMEM((2, td, D_MODEL), F32), pltpu.SemaphoreType.DMA(())],
        compiler_params=_params(("arbitrary",), 32),
        name="moe_combine",
    )(dest, ys, x, w, g.reshape(1, -1), b.reshape(1, -1))


def _moe(x, xb, rw_t, rb, w1, w3, w2, ln_g, ln_b):
    del xb
    m = x.shape[0]
    tm, td = MOE_TM, MOE_TD
    rows = 2 * m + N_EXPERTS * tm
    ro, cnt = _router(x, rw_t, rb)
    e0 = ro[0].astype(jnp.int32)
    e1 = ro[1].astype(jnp.int32)
    counts = cnt[:, 0].astype(jnp.int32).reshape(EXP_PER_GROUP, N_GROUPS).T.reshape(N_EXPERTS)
    padded = (counts + tm - 1) // tm * tm
    ends = jnp.cumsum(padded)
    offs = ends - padded
    d0 = offs[e0] + ro[4].astype(jnp.int32)
    d1 = offs[e1] + ro[5].astype(jnp.int32)
    dest = jnp.concatenate([d0.reshape(m // td, 1, td), d1.reshape(m // td, 1, td)], axis=-1)
    tile_start = jnp.arange(rows // tm, dtype=jnp.int32) * tm
    tile_expert = jnp.minimum(jnp.sum(tile_start[:, None] >= ends[None, :], axis=1), N_EXPERTS - 1).astype(jnp.int32)
    n_used = (ends[-1:] // tm).astype(jnp.int32)
    xs = _dispatch(x, dest, rows)
    ys = _experts(xs, w1, w3, w2, tile_expert, n_used)
    wcol = jnp.pad(ro[2:4].T, ((0, 0), (0, 6)))
    return _combine(ys, dest, x, wcol, ln_g, ln_b)


def _rope_tables(pos):
    half = N_HD // 2
    inv = ROPE_THETA ** (-jnp.arange(half, dtype=F32) / half)
    ang = pos.astype(F32)[:, None] * inv[None, :]
    cos, sin = jnp.cos(ang), jnp.sin(ang)
    return jnp.concatenate([cos, cos], -1), jnp.concatenate([-sin, sin], -1)


def _overlap_t(T, ncmp_pad):
    n_cmp = (T - CMP_BLOCK) // CMP_STRIDE + 1
    n_sel = T // SEL_BLOCK
    cs = np.arange(n_cmp) * CMP_STRIDE
    ss = np.arange(n_sel) * SEL_BLOCK
    ov = ((cs[:, None] <= ss[None, :] + SEL_BLOCK - 1) & (cs[:, None] + CMP_BLOCK - 1 >= ss[None, :])).astype(np.float32)
    out = np.zeros((n_sel, ncmp_pad), np.float32)
    out[:, :n_cmp] = ov.T
    return jnp.asarray(out)


def _split_in(w):
    return [w[..., IN_OFFS[i]:IN_OFFS[i + 1]] for i in range(len(IN_SIZES))]


def _pack_in(w_in, b_in):
    wb = jnp.concatenate([w_in, b_in[None, :]], axis=0)
    mq, mk, mv, mo, mi, mf, nq, nkc, nvc, nks, nvs, nkw, nvw, ng, gm = _split_in(wb)
    rows = wb.shape[0]
    p1 = jnp.concatenate([mq * (M_DQK ** -0.5), mk, mv, mo, nvs, nvw, gm], axis=1)
    p2 = jnp.concatenate([nq * (N_HD ** -0.5), nks, nkw], axis=1)
    zpad = lambda n: jnp.zeros((rows, n), F32)
    ng_g = ng.reshape(rows, N_KV, 3 * N_HPG)
    ng_pad = jnp.concatenate([ng_g, jnp.zeros((rows, N_KV, LANES - 3 * N_HPG), F32)], axis=-1).reshape(rows, N_KV * LANES)
    p3 = jnp.concatenate([mi, mf, zpad(LANES - 2 * M_HEADS), ng_pad, nkc, nvc], axis=1)
    return [(p[:-1].astype(BF16), p[-1]) for p in (p1, p2, p3)]


def kernel(x, w_in, b_in, m_norm_w, cmp_pos_k, cmp_w1_k, cmp_b1_k, cmp_w2_k, cmp_b2_k, cmp_pos_v, cmp_w1_v,
           cmp_b1_v, cmp_w2_v, cmp_b2_v, w_branch_m, w_branch_n, w_out, ln1_g, ln1_b, router_w, router_b,
           exp_w1, exp_w3, exp_w2, ln2_g, ln2_b):
    B, T, D = x.shape
    m = B * T
    nb = T // CMP_STRIDE
    cos_t, sin_t = _rope_tables(jnp.arange(T))
    cos_c, sin_c = _rope_tables(jnp.arange(nb) * CMP_STRIDE + CMP_BLOCK - 1)
    ovt = _overlap_t(T, nb)
    perm = (np.arange(N_GROUPS)[None, :] * EXP_PER_GROUP + np.arange(EXP_PER_GROUP)[:, None]).reshape(-1)
    rw_t = router_w.T[perm]
    rb = jnp.broadcast_to(router_b[perm][:, None], (N_EXPERTS, LANES))

    xf = x.reshape(m, D)
    xb = xf.astype(BF16)
    for l in range(DEPTH):
        (w1p, b1p), (w2p, b2p), (w3p, b3p) = _pack_in(w_in[l], b_in[l])
        p1 = _matmul(xb, w1p, b1p, BF16, MM_TN)
        p2 = _matmul(xb, w2p, b2p, BF16, MM_TN, rope=(cos_t, sin_t), seq=T)
        p3 = _matmul(xb, w3p, b3p, F32, P3_W)
        gt = p3.reshape(B, T, P3_W)[:, :, :8].transpose(0, 2, 1)
        h_m = _mlstm(p1, p3, gt, m_norm_w[l], B, T).reshape(m, M_V_W)

        def blocks16(off):
            a = p3.reshape(B, nb, CMP_STRIDE, P3_W)[..., off:off + N_KV_W]
            return a.reshape(B, nb, CMP_STRIDE, N_KV, N_HD).transpose(0, 3, 1, 2, 4).reshape(B, N_KV, nb, CMP_STRIDE * N_HD)

        k_cmp = _compress(blocks16(P3_NKC), cmp_pos_k[l], cmp_w1_k[l], cmp_b1_k[l], cmp_w2_k[l], cmp_b2_k[l],
                          cos_c, sin_c, True)
        v_cmp = _compress(blocks16(P3_NVC), cmp_pos_v[l], cmp_w1_v[l], cmp_b1_v[l], cmp_w2_v[l], cmp_b2_v[l],
                          cos_c, sin_c, False)
        h_n = _nsa(p1, p2, p3, k_cmp, v_cmp, ovt, B, T).reshape(m, N_Q_W)
        merged = _merge(h_m, h_n, w_branch_m[l].astype(BF16), w_branch_n[l].astype(BF16), p1)
        xf, xb = _out_ln(merged, w_out[l].astype(BF16), xf, ln1_g[l], ln1_b[l])
        xf, xb = _moe(xf, xb, rw_t, rb, exp_w1[l].astype(BF16), exp_w3[l].astype(BF16), exp_w2[l].astype(BF16),
                      ln2_g[l], ln2_b[l])
    return xf.reshape(B, T, D)
```

```python
import functools

import numpy as np
import jax
import jax.numpy as jnp
from jax import lax
from jax.experimental import pallas as pl
from jax.experimental.pallas import tpu as pltpu

F32 = jnp.float32
BF16 = jnp.bfloat16
HIGHEST = lax.Precision.HIGHEST

D_MODEL = 2048
DEPTH = 2
M_HEADS = 4
M_DQK = 256
M_DV = 512
N_KV = 4
N_HPG = 4
N_HEADS = N_KV * N_HPG
N_HD = 128
CMP_BLOCK = 32
CMP_STRIDE = 16
SEL_BLOCK = 64
SEL_TOPK = 16
WINDOW = 512
ROPE_THETA = 10000.0
N_EXPERTS = 32
N_GROUPS = 8
EXP_PER_GROUP = N_EXPERTS // N_GROUPS
D_FF = 1024
ALPHA = (2 * DEPTH) ** 0.25
EPS = 1e-5
NEG = -1e30

M_QK_W = M_HEADS * M_DQK
M_V_W = M_HEADS * M_DV
N_Q_W = N_HEADS * N_HD
N_KV_W = N_KV * N_HD
IN_SIZES = (M_QK_W, M_QK_W, M_V_W, M_V_W, M_HEADS, M_HEADS, N_Q_W, N_KV_W, N_KV_W, N_KV_W, N_KV_W,
            N_KV_W, N_KV_W, 3 * N_HEADS, 2 * D_MODEL)
IN_OFFS = tuple(int(v) for v in np.cumsum((0,) + IN_SIZES))

LANES = 128
MLSTM_CHUNK = 256
NSA_TQ = 256
MM_TM = 512
MM_TN = 1024
MOE_TM = 256
MOE_TD = 256
ROUTER_TM = 512

P1_MQ, P1_MK, P1_MV, P1_MO, P1_NVS, P1_NVW, P1_GM = 0, 1024, 2048, 4096, 6144, 6656, 7168
P1_W = 7168 + 2 * D_MODEL
P2_NQ, P2_NKS, P2_NKW = 0, 2048, 2560
P2_W = 3072
P3_GATES, P3_NG, P3_NKC, P3_NVC = 0, 128, 640, 1152
P3_W = 1664


def _params(sem, vmem_mb):
    return pltpu.CompilerParams(dimension_semantics=sem, vmem_limit_bytes=vmem_mb * 1024 * 1024)


def _nt(a, b, **kw):
    return lax.dot_general(a, b, (((1,), (1,)), ((), ())), preferred_element_type=F32, **kw)


def _tn(a, b):
    return lax.dot_general(a, b, (((0,), (0,)), ((), ())), preferred_element_type=F32)


def _standardize(x):
    mu = jnp.mean(x, -1, keepdims=True)
    xc = x - mu
    var = jnp.mean(xc * xc, -1, keepdims=True)
    return xc * lax.rsqrt(var + EPS)


def _mm_body(x_ref, w_ref, b_ref, *rest, rope, tn):
    acc = jnp.dot(x_ref[...], w_ref[...], preferred_element_type=F32) + b_ref[...]
    if rope:
        cos_ref, sin_ref, o_ref = rest
        cos = cos_ref[...]
        sin = sin_ref[...]
        for c in range(tn // N_HD):
            ch = acc[:, c * N_HD:(c + 1) * N_HD]
            o_ref[:, c * N_HD:(c + 1) * N_HD] = (ch * cos + pltpu.roll(ch, N_HD // 2, 1) * sin).astype(o_ref.dtype)
    else:
        (o_ref,) = rest
        o_ref[...] = acc.astype(o_ref.dtype)


def _matmul(x, w, b, out_dtype, tn, rope=None, seq=None):
    m, k = x.shape
    n = w.shape[1]
    tm = MM_TM
    grid = (n // tn, m // tm)
    in_specs = [pl.BlockSpec((tm, k), lambda j, i: (i, 0)),
                pl.BlockSpec((k, tn), lambda j, i: (0, j)),
                pl.BlockSpec((1, tn), lambda j, i: (0, j))]
    args = [x, w, b.reshape(1, n)]
    if rope is not None:
        nt = seq // tm
        in_specs += [pl.BlockSpec((tm, N_HD), lambda j, i: (i % nt, 0))] * 2
        args += list(rope)
    return pl.pallas_call(
        functools.partial(_mm_body, rope=rope is not None, tn=tn),
        grid=grid, in_specs=in_specs,
        out_specs=pl.BlockSpec((tm, tn), lambda j, i: (i, j)),
        out_shape=jax.ShapeDtypeStruct((m, n), out_dtype),
        compiler_params=_params(("parallel", "parallel"), 48),
        name="proj_matmul",
    )(*args)


def _log_sigmoid(x):
    return jnp.minimum(x, 0.0) - jnp.log1p(jnp.exp(-jnp.abs(x)))


def _mlstm_body(q_ref, k_ref, v_ref, og_ref, g_ref, gt_ref, nw_ref, out_ref, c_ref, n_ref, m_ref, *, L):
    @pl.when(pl.program_id(1) == 0)
    def _():
        c_ref[...] = jnp.zeros_like(c_ref)
        n_ref[...] = jnp.zeros_like(n_ref)
        m_ref[...] = jnp.zeros_like(m_ref)

    g = g_ref[...]
    gt = gt_ref[...]
    row = lax.broadcasted_iota(jnp.int32, (L, L), 0)
    col = lax.broadcasted_iota(jnp.int32, (L, L), 1)
    causal = row >= col
    b_all = jnp.dot(causal.astype(F32), _log_sigmoid(g), precision=HIGHEST, preferred_element_type=F32)
    bt_all = jnp.dot(_log_sigmoid(gt), (row <= col).astype(F32), precision=HIGHEST, preferred_element_type=F32)
    for h in range(M_HEADS):
        b_col = b_all[:, M_HEADS + h:M_HEADS + h + 1]
        li_col = g[:, h:h + 1]
        b_row = bt_all[M_HEADS + h:M_HEADS + h + 1, :]
        li_row = gt[h:h + 1, :]
        m_prev = m_ref[h][:, 0:1]
        dmat = jnp.where(causal, b_col - b_row + li_row, -jnp.inf)
        inter = b_col + m_prev
        m_t = jnp.maximum(inter, jnp.max(dmat, axis=1, keepdims=True))
        w_inter = jnp.exp(inter - m_t)
        q = q_ref[:, h * M_DQK:(h + 1) * M_DQK]
        k = k_ref[:, h * M_DQK:(h + 1) * M_DQK]
        v = v_ref[:, h * M_DV:(h + 1) * M_DV]
        s = _nt(q, k) * jnp.exp(dmat - m_t)
        ct = c_ref[h]
        n_row = n_ref[h]
        num = jnp.dot(s.astype(BF16), v, preferred_element_type=F32) + w_inter * jnp.dot(
            q, ct.astype(BF16), preferred_element_type=F32)
        qn = jnp.sum(q.astype(F32) * n_row, axis=1, keepdims=True)
        den = jnp.sum(s, axis=1, keepdims=True) + w_inter * qn
        hh = num / jnp.maximum(jnp.abs(den), jnp.exp(-m_t))
        hn = _standardize(hh) * nw_ref[:, h * M_DV:(h + 1) * M_DV]
        og = og_ref[:, h * M_DV:(h + 1) * M_DV].astype(F32)
        out_ref[:, h * M_DV:(h + 1) * M_DV] = (hn * jax.nn.sigmoid(og)).astype(out_ref.dtype)
        b_last = b_col[L - 1:L, :]
        g_col = b_last - b_col + li_col
        g_row = b_last - b_row + li_row
        m_new = jnp.maximum(b_last + m_prev, jnp.max(g_row, axis=1, keepdims=True))
        decay = jnp.exp(b_last + m_prev - m_new)
        ws_col = jnp.exp(g_col - m_new)
        kf = k.astype(F32)
        vw = (v.astype(F32) * ws_col).astype(BF16)
        c_ref[h] = decay * ct + _tn(k, vw)
        n_ref[h] = decay * n_row + jnp.sum(kf * ws_col, axis=0, keepdims=True)
        m_ref[h] = jnp.broadcast_to(m_new, (1, LANES))


def _mlstm(p1, p3, gt, norm_w, B, T):
    L = MLSTM_CHUNK
    p1 = p1.reshape(B, T, P1_W)
    p3 = p3.reshape(B, T, P3_W)
    return pl.pallas_call(
        functools.partial(_mlstm_body, L=L),
        grid=(B, T // L),
        in_specs=[pl.BlockSpec((None, L, M_QK_W), lambda b, c: (b, c, P1_MQ // M_QK_W)),
                  pl.BlockSpec((None, L, M_QK_W), lambda b, c: (b, c, P1_MK // M_QK_W)),
                  pl.BlockSpec((None, L, M_V_W), lambda b, c: (b, c, P1_MV // M_V_W)),
                  pl.BlockSpec((None, L, M_V_W), lambda b, c: (b, c, P1_MO // M_V_W)),
                  pl.BlockSpec((None, L, LANES), lambda b, c: (b, c, 0)),
                  pl.BlockSpec((None, 8, L), lambda b, c: (b, 0, c)),
                  pl.BlockSpec((1, M_V_W), lambda b, c: (0, 0))],
        out_specs=pl.BlockSpec((None, L, M_V_W), lambda b, c: (b, c, 0)),
        out_shape=jax.ShapeDtypeStruct((B, T, M_V_W), BF16),
        scratch_shapes=[pltpu.VMEM((M_HEADS, M_DQK, M_DV), F32),
                        pltpu.VMEM((M_HEADS, 1, M_DQK), F32),
                        pltpu.VMEM((M_HEADS, 1, LANES), F32)],
        compiler_params=_params(("parallel", "arbitrary"), 48),
        name="mlstm",
    )(p1, p1, p1, p1, p3, gt, norm_w.reshape(1, M_V_W))


def _compress_body(x_ref, pos_ref, w1_ref, b1_ref, w2_ref, b2_ref, cos_ref, sin_ref, o_ref, *, rope, nb):
    x = x_ref[...]
    half = CMP_STRIDE * N_HD
    lo = jnp.dot((x + pos_ref[:, :half]).astype(BF16), w1_ref[:half, :], preferred_element_type=F32)
    hi = jnp.dot((x + pos_ref[:, half:]).astype(BF16), w1_ref[half:, :], preferred_element_type=F32)
    pre = lo + pltpu.roll(hi, nb - 1, 0) + b1_ref[...]
    h = jax.nn.gelu(pre)
    y = jnp.dot(h.astype(BF16), w2_ref[...], preferred_element_type=F32) + b2_ref[...]
    if rope:
        y = y * cos_ref[...] + pltpu.roll(y, N_HD // 2, 1) * sin_ref[...]
    keep = lax.broadcasted_iota(jnp.int32, (nb, N_HD), 0) < nb - 1
    o_ref[...] = jnp.where(keep, y, 0.0).astype(o_ref.dtype)


def _compress(x2, pos, w1, b1, w2, b2, cos, sin, rope):
    B, G, nb, _ = x2.shape
    full = lambda shape: pl.BlockSpec(shape, lambda b, g: (0,) * len(shape))
    return pl.pallas_call(
        functools.partial(_compress_body, rope=rope, nb=nb),
        grid=(B, G),
        in_specs=[pl.BlockSpec((None, None, nb, CMP_STRIDE * N_HD), lambda b, g: (b, g, 0, 0)),
                  full((1, CMP_BLOCK * N_HD)), full((CMP_BLOCK * N_HD, N_HD)), full((1, N_HD)),
                  full((N_HD, N_HD)), full((1, N_HD)), full((nb, N_HD)), full((nb, N_HD))],
        out_specs=pl.BlockSpec((None, None, nb, N_HD), lambda b, g: (b, g, 0, 0)),
        out_shape=jax.ShapeDtypeStruct((B, G, nb, N_HD), BF16),
        compiler_params=_params(("parallel", "parallel"), 32),
        name="nsa_compress",
    )(x2, pos.reshape(1, -1), w1.reshape(CMP_BLOCK * N_HD, N_HD).astype(BF16), b1.reshape(1, N_HD),
      w2.astype(BF16), b2.reshape(1, N_HD), cos, sin)


def _nsa_body(q_ref, ks_ref, vst_ref, kw_ref, vwt_ref, kc_ref, vct_ref, gt_ref, ovt_ref, o_ref,
              acc_ref, m_ref, l_ref, sel_ref, out_ref, s_ref, p_ref, *, tq, ncmp):
    qi = pl.program_id(2)
    t0 = qi * tq
    nsel = ovt_ref.shape[0]
    gates = jax.nn.sigmoid(gt_ref[...])
    q_heads = [q_ref[:, h * N_HD:(h + 1) * N_HD] for h in range(N_HPG)]

    kc = kc_ref[...]
    vct = vct_ref[...]
    n_idx = lax.broadcasted_iota(jnp.int32, (ncmp, tq), 0)
    t_cmp = t0 + lax.broadcasted_iota(jnp.int32, (ncmp, tq), 1)
    cmask = (n_idx * CMP_STRIDE + (CMP_BLOCK - 1) <= t_cmp) & (n_idx < ncmp - 1)
    psum = jnp.zeros((ncmp, tq), F32)
    for h in range(N_HPG):
        s = jnp.where(cmask, _nt(kc, q_heads[h]), NEG)
        e = jnp.where(cmask, jnp.exp(s - jnp.max(s, axis=0, keepdims=True)), 0.0)
        p = e / jnp.maximum(jnp.sum(e, axis=0, keepdims=True), 1e-30)
        psum = psum + p
        out_ref[h] = gates[3 * h:3 * h + 1, :] * jnp.dot(vct, p.astype(BF16), preferred_element_type=F32)

    imp = jnp.dot(ovt_ref[...], psum, precision=HIGHEST, preferred_element_type=F32)
    j_idx = lax.broadcasted_iota(jnp.int32, (nsel, tq), 0)
    t_row = t0 + lax.broadcasted_iota(jnp.int32, (nsel, tq), 1)
    cur = lax.shift_right_logical(t_row, 6)
    forced = (j_idx == 0) | (j_idx == cur) | (j_idx == cur - 1)
    score = jnp.where(forced, jnp.inf, jnp.where(j_idx * SEL_BLOCK <= t_row, imp, -jnp.inf))

    def pick(_, carry):
        score, sel = carry
        mx = jnp.max(score, axis=0, keepdims=True)
        idx = jnp.min(jnp.where(score == mx, j_idx, nsel), axis=0, keepdims=True)
        hit = j_idx == idx
        return jnp.where(hit, -jnp.inf, score), jnp.where(hit, 1.0, sel)

    _, sel = lax.fori_loop(0, min(SEL_TOPK, nsel), pick, (score, jnp.zeros((nsel, tq), F32)), unroll=True)
    sel_ref[...] = sel

    def flash_init():
        acc_ref[...] = jnp.zeros_like(acc_ref)
        l_ref[...] = jnp.zeros_like(l_ref)
        m_ref[...] = jnp.full(m_ref.shape, NEG, F32)

    def flash_tile(k_ref, vt_ref, j, mask):
        start = pl.multiple_of(j * tq, tq)
        k = k_ref[pl.ds(start, tq), :]
        vt = vt_ref[:, pl.ds(start, tq)]
        for h in range(N_HPG):
            s_ref[h] = _nt(k, q_heads[h])
        alphas = []
        for h in range(N_HPG):
            s = s_ref[h]
            if mask is not None:
                s = jnp.where(mask, s, NEG)
            m_old = m_ref[h]
            m_new = jnp.maximum(m_old, jnp.max(s, axis=0, keepdims=True))
            p = jnp.exp(s - m_new)
            alpha = jnp.exp(m_old - m_new)
            l_ref[h] = alpha * l_ref[h] + jnp.sum(p, axis=0, keepdims=True)
            p_ref[h] = p.astype(BF16)
            m_ref[h] = m_new
            alphas.append(alpha)
        for h in range(N_HPG):
            acc_ref[h] = alphas[h] * acc_ref[h] + jnp.dot(vt, p_ref[h], preferred_element_type=F32)

    def flash_add(branch):
        for h in range(N_HPG):
            out_ref[h] = out_ref[h] + (gates[3 * h + branch:3 * h + branch + 1, :] / l_ref[h]) * acc_ref[h]

    kpos = lax.broadcasted_iota(jnp.int32, (tq, tq), 0)
    qpos = lax.broadcasted_iota(jnp.int32, (tq, tq), 1)
    blk_per_tile = tq // SEL_BLOCK

    def sel_mask(j):
        rows = [jnp.broadcast_to(sel_ref[pl.ds(j * blk_per_tile + b, 1), :], (SEL_BLOCK, tq))
                for b in range(blk_per_tile)]
        return jnp.concatenate(rows, axis=0) > 0.5

    flash_init()

    def sel_step(j, carry):
        flash_tile(ks_ref, vst_ref, j, sel_mask(j))
        return carry

    lax.fori_loop(0, qi, sel_step, 0)
    flash_tile(ks_ref, vst_ref, qi, sel_mask(qi) & (kpos <= qpos))
    flash_add(1)

    flash_init()
    n_back = WINDOW // tq
    for back in range(n_back + 1):
        if back == 0:
            mask = kpos <= qpos
        elif back < n_back:
            mask = None
        else:
            mask = kpos > qpos

        @pl.when(qi - back >= 0)
        def _():
            flash_tile(kw_ref, vwt_ref, qi - back, mask)
    flash_add(2)

    for h in range(N_HPG):
        o_ref[:, h * N_HD:(h + 1) * N_HD] = out_ref[h].T.astype(o_ref.dtype)


def _nsa(p1, p2, p3, k_cmp, v_cmp, ovt, B, T):
    tq = NSA_TQ
    ncmp = k_cmp.shape[2]
    p1 = p1.reshape(B, T, P1_W)
    p2 = p2.reshape(B, T, P2_W)
    p3 = p3.reshape(B, T, P3_W)
    hw = N_HPG * N_HD
    assert WINDOW % tq == 0 and tq % SEL_BLOCK == 0
    vst = p1[:, :, P1_NVS:P1_NVS + N_KV_W].transpose(0, 2, 1)
    vwt = p1[:, :, P1_NVW:P1_NVW + N_KV_W].transpose(0, 2, 1)
    vct = v_cmp.transpose(0, 1, 3, 2)
    gt = p3[:, :, P3_NG:P3_NG + N_KV * LANES].reshape(B, T, N_KV, LANES)[..., :16].transpose(0, 2, 3, 1)
    k_spec = lambda off: pl.BlockSpec((None, T, N_HD), lambda b, g, i: (b, 0, off // N_HD + g))
    vt_spec = pl.BlockSpec((None, N_HD, T), lambda b, g, i: (b, g, 0))
    return pl.pallas_call(
        functools.partial(_nsa_body, tq=tq, ncmp=ncmp),
        grid=(B, N_KV, T // tq),
        in_specs=[pl.BlockSpec((None, tq, hw), lambda b, g, i: (b, i, P2_NQ // hw + g)),
                  k_spec(P2_NKS), vt_spec, k_spec(P2_NKW), vt_spec,
                  pl.BlockSpec((None, None, ncmp, N_HD), lambda b, g, i: (b, g, 0, 0)),
                  pl.BlockSpec((None, None, N_HD, ncmp), lambda b, g, i: (b, g, 0, 0)),
                  pl.BlockSpec((None, None, 16, tq), lambda b, g, i: (b, g, 0, i)),
                  pl.BlockSpec(ovt.shape, lambda b, g, i: (0, 0))],
        out_specs=pl.BlockSpec((None, tq, hw), lambda b, g, i: (b, i, g)),
        out_shape=jax.ShapeDtypeStruct((B, T, N_Q_W), BF16),
        scratch_shapes=[pltpu.VMEM((N_HPG, N_HD, tq), F32), pltpu.VMEM((N_HPG, 1, tq), F32),
                        pltpu.VMEM((N_HPG, 1, tq), F32), pltpu.VMEM((ovt.shape[0], tq), F32),
                        pltpu.VMEM((N_HPG, N_HD, tq), F32), pltpu.VMEM((N_HPG, tq, tq), F32),
                        pltpu.VMEM((N_HPG, tq, tq), BF16)],
        compiler_params=_params(("parallel", "parallel", "arbitrary"), 48),
        name="nsa_attention",
    )(p2, p2, vst, p2, vwt, k_cmp, vct, gt, ovt)


def _merge_body(hm_ref, hn_ref, wm_ref, wn_ref, gm_ref, gn_ref, o_ref):
    ym = jnp.dot(hm_ref[...], wm_ref[...], preferred_element_type=F32)
    yn = jnp.dot(hn_ref[...], wn_ref[...], preferred_element_type=F32)
    gm = jax.nn.sigmoid(gm_ref[...].astype(F32))
    gn = jax.nn.sigmoid(gn_ref[...].astype(F32))
    o_ref[...] = (gm * ym + gn * yn).astype(o_ref.dtype)


def _merge(hm, hn, wm, wn, p1):
    m = hm.shape[0]
    tm, tn = MM_TM, 512
    return pl.pallas_call(
        _merge_body,
        grid=(D_MODEL // tn, m // tm),
        in_specs=[pl.BlockSpec((tm, M_V_W), lambda j, i: (i, 0)),
                  pl.BlockSpec((tm, N_Q_W), lambda j, i: (i, 0)),
                  pl.BlockSpec((M_V_W, tn), lambda j, i: (0, j)),
                  pl.BlockSpec((N_Q_W, tn), lambda j, i: (0, j)),
                  pl.BlockSpec((tm, tn), lambda j, i: (i, P1_GM // tn + j)),
                  pl.BlockSpec((tm, tn), lambda j, i: (i, (P1_GM + D_MODEL) // tn + j))],
        out_specs=pl.BlockSpec((tm, tn), lambda j, i: (i, j)),
        out_shape=jax.ShapeDtypeStruct((m, D_MODEL), BF16),
        compiler_params=_params(("parallel", "parallel"), 48),
        name="branch_merge",
    )(hm, hn, wm, wn, p1, p1)


def _out_ln_body(y_ref, w_ref, x_ref, g_ref, b_ref, o_ref, ob_ref):
    mix = jnp.dot(y_ref[...], w_ref[...], preferred_element_type=F32)
    out = _standardize(ALPHA * x_ref[...] + mix) * g_ref[...] + b_ref[...]
    o_ref[...] = out
    ob_ref[...] = out.astype(BF16)


def _out_ln(y, w, x, g, b):
    m = y.shape[0]
    tm = MM_TM
    row = pl.BlockSpec((tm, D_MODEL), lambda i: (i, 0))
    vec = pl.BlockSpec((1, D_MODEL), lambda i: (0, 0))
    return pl.pallas_call(
        _out_ln_body,
        grid=(m // tm,),
        in_specs=[row, pl.BlockSpec((D_MODEL, D_MODEL), lambda i: (0, 0)), row, vec, vec],
        out_specs=[row, row],
        out_shape=[jax.ShapeDtypeStruct((m, D_MODEL), F32), jax.ShapeDtypeStruct((m, D_MODEL), BF16)],
        compiler_params=_params(("parallel",), 48),
        name="out_proj_layernorm",
    )(y, w, x, g.reshape(1, -1), b.reshape(1, -1))


def _first_of4(vals, target):
    return jnp.where(vals[0] == target, 0.0, jnp.where(vals[1] == target, 1.0, jnp.where(vals[2] == target, 2.0, 3.0)))


def _select4(idx, vals):
    return jnp.where(idx == 0.0, vals[0], jnp.where(idx == 1.0, vals[1], jnp.where(idx == 2.0, vals[2], vals[3])))


def _router_body(x_ref, rw_ref, rb_ref, o_ref, cnt_ref, carry_ref, *, tm):
    @pl.when(pl.program_id(0) == 0)
    def _():
        carry_ref[...] = jnp.zeros_like(carry_ref)

    logits = _nt(rw_ref[...], x_ref[...], precision=HIGHEST)
    aff = jax.nn.sigmoid(logits)
    biased = aff + rb_ref[:, 0:1]
    a = [biased[i * N_GROUPS:(i + 1) * N_GROUPS, :] for i in range(EXP_PER_GROUP)]
    af = [aff[i * N_GROUPS:(i + 1) * N_GROUPS, :] for i in range(EXP_PER_GROUP)]
    m1 = jnp.maximum(jnp.maximum(a[0], a[1]), jnp.maximum(a[2], a[3]))
    i1 = _first_of4(a, m1)
    rest = [jnp.where(i1 == float(i), -jnp.inf, a[i]) for i in range(EXP_PER_GROUP)]
    m2 = jnp.maximum(jnp.maximum(rest[0], rest[1]), jnp.maximum(rest[2], rest[3]))
    i2 = _first_of4(rest, m2)
    gscore = m1 + m2
    g_iota = lax.broadcasted_iota(jnp.int32, (N_GROUPS, tm), 0).astype(F32)
    g_idx = jnp.min(jnp.where(gscore == jnp.max(gscore, axis=0, keepdims=True), g_iota, float(N_GROUPS)),
                    axis=0, keepdims=True)
    in_g = g_iota == g_idx
    take = lambda v: jnp.sum(jnp.where(in_g, v, 0.0), axis=0, keepdims=True)
    s0 = take(i1)
    s1 = take(i2)
    w0 = take(_select4(i1, af))
    w1 = take(_select4(i2, af))
    wsum = w0 + w1
    r0 = s0 * N_GROUPS + g_idx
    r1 = s1 * N_GROUPS + g_idx
    r_iota = lax.broadcasted_iota(jnp.int32, (N_EXPERTS, tm), 0).astype(F32)
    member = (r_iota == r0) | (r_iota == r1)
    tt = lax.broadcasted_iota(jnp.int32, (tm, tm), 0)
    tc = lax.broadcasted_iota(jnp.int32, (tm, tm), 1)
    before = jnp.dot(member.astype(BF16), (tt < tc).astype(BF16), preferred_element_type=F32)
    base = before + carry_ref[:, 0:1]
    rank0 = jnp.sum(jnp.where(r_iota == r0, base, 0.0), axis=0, keepdims=True)
    rank1 = jnp.sum(jnp.where(r_iota == r1, base, 0.0), axis=0, keepdims=True)
    new_carry = carry_ref[:, 0:1] + jnp.sum(member.astype(F32), axis=1, keepdims=True)
    carry_ref[...] = jnp.broadcast_to(new_carry, carry_ref.shape)
    cnt_ref[...] = jnp.broadcast_to(new_carry, cnt_ref.shape)
    o_ref[0:1, :] = g_idx * EXP_PER_GROUP + s0
    o_ref[1:2, :] = g_idx * EXP_PER_GROUP + s1
    o_ref[2:3, :] = w0 / wsum
    o_ref[3:4, :] = w1 / wsum
    o_ref[4:5, :] = rank0
    o_ref[5:6, :] = rank1
    o_ref[6:8, :] = jnp.zeros((2, tm), F32)


def _router(x, rw_t, rb):
    m = x.shape[0]
    tm = ROUTER_TM
    return pl.pallas_call(
        functools.partial(_router_body, tm=tm),
        grid=(m // tm,),
        in_specs=[pl.BlockSpec((tm, D_MODEL), lambda i: (i, 0)),
                  pl.BlockSpec((N_EXPERTS, D_MODEL), lambda i: (0, 0)),
                  pl.BlockSpec((N_EXPERTS, LANES), lambda i: (0, 0))],
        out_specs=[pl.BlockSpec((8, tm), lambda i: (0, i)),
                   pl.BlockSpec((N_EXPERTS, LANES), lambda i: (0, 0))],
        out_shape=[jax.ShapeDtypeStruct((8, m), F32), jax.ShapeDtypeStruct((N_EXPERTS, LANES), F32)],
        scratch_shapes=[pltpu.VMEM((N_EXPERTS, LANES), F32)],
        compiler_params=_params(("arbitrary",), 48),
        name="moe_router",
    )(x, rw_t, rb)


def _row_copy(src_ref, src_row, dst_ref, dst_row, sem):
    return pltpu.make_async_copy(src_ref.at[pl.ds(src_row, 1), :], dst_ref.at[pl.ds(dst_row, 1), :], sem)


def _dispatch_body(dest_ref, x_ref, init_ref, xs_ref, sem, *, td):
    del init_ref

    def issue(r, c):
        _row_copy(x_ref, r, xs_ref, dest_ref[0, 0, r], sem).start(priority=0)
        _row_copy(x_ref, r, xs_ref, dest_ref[0, 0, td + r], sem).start(priority=1)
        return c

    lax.fori_loop(0, td, issue, 0)

    def drain(r, c):
        _row_copy(x_ref, 0, xs_ref, 0, sem).wait()
        _row_copy(x_ref, 0, xs_ref, 0, sem).wait()
        return c

    lax.fori_loop(0, td, drain, 0)


def _dispatch(x, dest, rows):
    m = x.shape[0]
    td = MOE_TD
    return pl.pallas_call(
        functools.partial(_dispatch_body, td=td),
        grid=(m // td,),
        in_specs=[pl.BlockSpec((1, 1, 2 * td), lambda i: (i, 0, 0), memory_space=pltpu.SMEM),
                  pl.BlockSpec((td, D_MODEL), lambda i: (i, 0)),
                  pl.BlockSpec(memory_space=pl.ANY)],
        out_specs=pl.BlockSpec(memory_space=pl.ANY),
        out_shape=jax.ShapeDtypeStruct((rows, D_MODEL), F32),
        scratch_shapes=[pltpu.SemaphoreType.DMA(())],
        input_output_aliases={2: 0},
        compiler_params=_params(("arbitrary",), 32),
        name="moe_dispatch",
    )(dest, x, jnp.zeros((rows, D_MODEL), F32))


def _expert_changed(te_ref, i):
    return (i == 0) | (te_ref[i] != te_ref[jnp.maximum(i - 1, 0)])


def _experts_up_body(te_ref, nused_ref, x_ref, w1_ref, w3_ref, h_ref, w1b_ref, w3b_ref):
    i = pl.program_id(0)
    used = i < nused_ref[0]

    @pl.when(used & _expert_changed(te_ref, i))
    def _():
        w1b_ref[...] = w1_ref[...].astype(BF16)
        w3b_ref[...] = w3_ref[...].astype(BF16)

    @pl.when(used)
    def _():
        xb = x_ref[...].astype(BF16)
        a = jnp.dot(xb, w1b_ref[...], preferred_element_type=F32)
        b = jnp.dot(xb, w3b_ref[...], preferred_element_type=F32)
        h_ref[...] = (a * jax.nn.sigmoid(a) * b).astype(BF16)

    @pl.when(jnp.logical_not(used))
    def _():
        h_ref[...] = jnp.zeros_like(h_ref)


def _experts_down_body(te_ref, nused_ref, h_ref, w2_ref, o_ref, w2b_ref):
    i = pl.program_id(0)
    used = i < nused_ref[0]

    @pl.when(used & _expert_changed(te_ref, i))
    def _():
        w2b_ref[...] = w2_ref[...].astype(BF16)

    @pl.when(used)
    def _():
        o_ref[...] = jnp.dot(h_ref[...], w2b_ref[...], preferred_element_type=F32)

    @pl.when(jnp.logical_not(used))
    def _():
        o_ref[...] = jnp.zeros_like(o_ref)


def _experts(xs, w1, w3, w2, layer, tile_expert, n_used):
    rows = xs.shape[0]
    tm = MOE_TM
    by_expert = lambda r, c: pl.BlockSpec((None, None, r, c), lambda i, te, nu: (layer, te[i], 0, 0))
    by_tile = lambda c: pl.BlockSpec((tm, c), lambda i, te, nu: (i, 0))
    h = pl.pallas_call(
        _experts_up_body,
        grid_spec=pltpu.PrefetchScalarGridSpec(
            num_scalar_prefetch=2, grid=(rows // tm,),
            in_specs=[by_tile(D_MODEL), by_expert(D_MODEL, D_FF), by_expert(D_MODEL, D_FF)],
            out_specs=by_tile(D_FF),
            scratch_shapes=[pltpu.VMEM((D_MODEL, D_FF), BF16), pltpu.VMEM((D_MODEL, D_FF), BF16)]),
        out_shape=jax.ShapeDtypeStruct((rows, D_FF), BF16),
        compiler_params=_params(("arbitrary",), 56),
        name="moe_experts_up",
    )(tile_expert, n_used, xs, w1, w3)
    return pl.pallas_call(
        _experts_down_body,
        grid_spec=pltpu.PrefetchScalarGridSpec(
            num_scalar_prefetch=2, grid=(rows // tm,),
            in_specs=[by_tile(D_FF), by_expert(D_FF, D_MODEL)],
            out_specs=by_tile(D_MODEL),
            scratch_shapes=[pltpu.VMEM((D_FF, D_MODEL), BF16)]),
        out_shape=jax.ShapeDtypeStruct((rows, D_MODEL), F32),
        compiler_params=_params(("arbitrary",), 40),
        name="moe_experts_down",
    )(tile_expert, n_used, h, w2)


def _combine_body(dest_ref, ys_ref, x_ref, w_ref, g_ref, b_ref, o_ref, ob_ref, buf_ref, sem, *, td):
    def issue(r, c):
        _row_copy(ys_ref, dest_ref[0, 0, r], buf_ref.at[0], r, sem).start(priority=0)
        _row_copy(ys_ref, dest_ref[0, 0, td + r], buf_ref.at[1], r, sem).start(priority=1)
        return c

    lax.fori_loop(0, td, issue, 0)

    def drain(r, c):
        _row_copy(ys_ref, 0, buf_ref.at[0], 0, sem).wait()
        _row_copy(ys_ref, 0, buf_ref.at[1], 0, sem).wait()
        return c

    lax.fori_loop(0, td, drain, 0)
    w = w_ref[...]
    ff = w[:, 0:1] * buf_ref[0] + w[:, 1:2] * buf_ref[1]
    out = _standardize(ALPHA * x_ref[...] + ff) * g_ref[...] + b_ref[...]
    o_ref[...] = out
    ob_ref[...] = out.astype(BF16)


def _combine(ys, dest, x, w, g, b):
    m = x.shape[0]
    td = MOE_TD
    row = pl.BlockSpec((td, D_MODEL), lambda i: (i, 0))
    vec = pl.BlockSpec((1, D_MODEL), lambda i: (0, 0))
    return pl.pallas_call(
        functools.partial(_combine_body, td=td),
        grid=(m // td,),
        in_specs=[pl.BlockSpec((1, 1, 2 * td), lambda i: (i, 0, 0), memory_space=pltpu.SMEM),
                  pl.BlockSpec(memory_space=pl.ANY), row,
                  pl.BlockSpec((td, 8), lambda i: (i, 0)), vec, vec],
        out_specs=[row, row],
        out_shape=[jax.ShapeDtypeStruct((m, D_MODEL), F32), jax.ShapeDtypeStruct((m, D_MODEL), BF16)],
        scratch_shapes=[pltpu.VMEM((2, td, D_MODEL), F32), pltpu.SemaphoreType.DMA(())],
        compiler_params=_params(("arbitrary",), 32),
        name="moe_combine",
    )(dest, ys, x, w, g.reshape(1, -1), b.reshape(1, -1))


def _moe(x, xb, rw_t, rb, w1, w3, w2, layer, ln_g, ln_b):
    del xb
    m = x.shape[0]
    tm, td = MOE_TM, MOE_TD
    rows = 2 * m + N_EXPERTS * tm
    ro, cnt = _router(x, rw_t, rb)
    e0 = ro[0].astype(jnp.int32)
    e1 = ro[1].astype(jnp.int32)
    counts = cnt[:, 0].astype(jnp.int32).reshape(EXP_PER_GROUP, N_GROUPS).T.reshape(N_EXPERTS)
    padded = (counts + tm - 1) // tm * tm
    ends = jnp.cumsum(padded)
    offs = ends - padded
    d0 = offs[e0] + ro[4].astype(jnp.int32)
    d1 = offs[e1] + ro[5].astype(jnp.int32)
    dest = jnp.concatenate([d0.reshape(m // td, 1, td), d1.reshape(m // td, 1, td)], axis=-1)
    tile_start = jnp.arange(rows // tm, dtype=jnp.int32) * tm
    tile_expert = jnp.minimum(jnp.sum(tile_start[:, None] >= ends[None, :], axis=1), N_EXPERTS - 1).astype(jnp.int32)
    n_used = (ends[-1:] // tm).astype(jnp.int32)
    xs = _dispatch(x, dest, rows)
    ys = _experts(xs, w1, w3, w2, layer, tile_expert, n_used)
    wcol = jnp.pad(ro[2:4].T, ((0, 0), (0, 6)))
    return _combine(ys, dest, x, wcol, ln_g, ln_b)


def _rope_tables(pos):
    half = N_HD // 2
    inv = ROPE_THETA ** (-jnp.arange(half, dtype=F32) / half)
    ang = pos.astype(F32)[:, None] * inv[None, :]
    cos, sin = jnp.cos(ang), jnp.sin(ang)
    return jnp.concatenate([cos, cos], -1), jnp.concatenate([-sin, sin], -1)


def _overlap_t(T, ncmp_pad):
    n_cmp = (T - CMP_BLOCK) // CMP_STRIDE + 1
    n_sel = T // SEL_BLOCK
    cs = np.arange(n_cmp) * CMP_STRIDE
    ss = np.arange(n_sel) * SEL_BLOCK
    ov = ((cs[:, None] <= ss[None, :] + SEL_BLOCK - 1) & (cs[:, None] + CMP_BLOCK - 1 >= ss[None, :])).astype(np.float32)
    out = np.zeros((n_sel, ncmp_pad), np.float32)
    out[:, :n_cmp] = ov.T
    return jnp.asarray(out)


def _pack_body(w_ref, o1_ref, o2_ref, o3_ref):
    (mq, mk, mv, mo, mi, mf, nq, nkc, nvc, nks, nvs, nkw, nvw, ng, gm, end) = IN_OFFS

    def put(o_ref, off, a, b, scale=1.0):
        o_ref[:, off:off + b - a] = (w_ref[:, a:b] * scale).astype(o_ref.dtype)

    put(o1_ref, P1_MQ, mq, mk, M_DQK ** -0.5)
    put(o1_ref, P1_MK, mk, mo)
    put(o1_ref, P1_MO, mo, mi)
    put(o1_ref, P1_NVS, nvs, nkw)
    put(o1_ref, P1_NVW, nvw, ng)
    put(o1_ref, P1_GM, gm, end)
    put(o2_ref, P2_NQ, nq, nkc, N_HD ** -0.5)
    put(o2_ref, P2_NKS, nks, nvs)
    put(o2_ref, P2_NKW, nkw, nvw)
    o3_ref[:, 0:P3_NKC] = jnp.zeros((o3_ref.shape[0], P3_NKC), o3_ref.dtype)
    put(o3_ref, P3_GATES, mi, nq)
    for g in range(N_KV):
        put(o3_ref, P3_NG + g * LANES, ng + g * 3 * N_HPG, ng + (g + 1) * 3 * N_HPG)
    put(o3_ref, P3_NKC, nkc, nks)


def _pack(w, layer, out_dtype, tr):
    rows = w.shape[1]
    out = lambda n: pl.BlockSpec((tr, n), lambda i: (i, 0))
    return pl.pallas_call(
        _pack_body,
        grid=(rows // tr,),
        in_specs=[pl.BlockSpec((None, tr, IN_OFFS[-1]), lambda i: (layer, i, 0))],
        out_specs=[out(P1_W), out(P2_W), out(P3_W)],
        out_shape=[jax.ShapeDtypeStruct((rows, n), out_dtype) for n in (P1_W, P2_W, P3_W)],
        compiler_params=_params(("parallel",), 56),
        name="pack_in_proj",
    )(w)


def kernel(x, w_in, b_in, m_norm_w, cmp_pos_k, cmp_w1_k, cmp_b1_k, cmp_w2_k, cmp_b2_k, cmp_pos_v, cmp_w1_v,
           cmp_b1_v, cmp_w2_v, cmp_b2_v, w_branch_m, w_branch_n, w_out, ln1_g, ln1_b, router_w, router_b,
           exp_w1, exp_w3, exp_w2, ln2_g, ln2_b):
    B, T, D = x.shape
    m = B * T
    nb = T // CMP_STRIDE
    cos_t, sin_t = _rope_tables(jnp.arange(T))
    cos_c, sin_c = _rope_tables(jnp.arange(nb) * CMP_STRIDE + CMP_BLOCK - 1)
    ovt = _overlap_t(T, nb)
    perm = (np.arange(N_GROUPS)[None, :] * EXP_PER_GROUP + np.arange(EXP_PER_GROUP)[:, None]).reshape(-1)
    rw_t = router_w.T[perm]
    rb = jnp.broadcast_to(router_b[perm][:, None], (N_EXPERTS, LANES))

    b_rows = jnp.pad(b_in[:, None, :], ((0, 0), (0, 7), (0, 0)))
    xf = x.reshape(m, D)
    xb = xf.astype(BF16)
    for l in range(DEPTH):
        w1p, w2p, w3p = _pack(w_in, l, BF16, 256)
        b1p, b2p, b3p = (b[0] for b in _pack(b_rows, l, F32, 8))
        p1 = _matmul(xb, w1p, b1p, BF16, MM_TN)
        p2 = _matmul(xb, w2p, b2p, BF16, MM_TN, rope=(cos_t, sin_t), seq=T)
        p3 = _matmul(xb, w3p, b3p, F32, P3_W)
        gt = p3.reshape(B, T, P3_W)[:, :, :8].transpose(0, 2, 1)
        h_m = _mlstm(p1, p3, gt, m_norm_w[l], B, T).reshape(m, M_V_W)

        def blocks16(off):
            a = p3.reshape(B, nb, CMP_STRIDE, P3_W)[..., off:off + N_KV_W]
            return a.reshape(B, nb, CMP_STRIDE, N_KV, N_HD).transpose(0, 3, 1, 2, 4).reshape(B, N_KV, nb, CMP_STRIDE * N_HD)

        k_cmp = _compress(blocks16(P3_NKC), cmp_pos_k[l], cmp_w1_k[l], cmp_b1_k[l], cmp_w2_k[l], cmp_b2_k[l],
                          cos_c, sin_c, True)
        v_cmp = _compress(blocks16(P3_NVC), cmp_pos_v[l], cmp_w1_v[l], cmp_b1_v[l], cmp_w2_v[l], cmp_b2_v[l],
                          cos_c, sin_c, False)
        h_n = _nsa(p1, p2, p3, k_cmp, v_cmp, ovt, B, T).reshape(m, N_Q_W)
        merged = _merge(h_m, h_n, w_branch_m[l].astype(BF16), w_branch_n[l].astype(BF16), p1)
        xf, xb = _out_ln(merged, w_out[l].astype(BF16), xf, ln1_g[l], ln1_b[l])
        xf, xb = _moe(xf, xb, rw_t, rb, exp_w1, exp_w3, exp_w2, l, ln2_g[l], ln2_b[l])
    return xf.reshape(B, T, D)
```

```python
import functools

import numpy as np
import jax
import jax.numpy as jnp
from jax import lax
from jax.experimental import pallas as pl
from jax.experimental.pallas import tpu as pltpu

F32 = jnp.float32
BF16 = jnp.bfloat16
HIGHEST = lax.Precision.HIGHEST

D_MODEL = 2048
DEPTH = 2
M_HEADS = 4
M_DQK = 256
M_DV = 512
N_KV = 4
N_HPG = 4
N_HEADS = N_KV * N_HPG
N_HD = 128
CMP_BLOCK = 32
CMP_STRIDE = 16
SEL_BLOCK = 64
SEL_TOPK = 16
WINDOW = 512
ROPE_THETA = 10000.0
N_EXPERTS = 32
N_GROUPS = 8
EXP_PER_GROUP = N_EXPERTS // N_GROUPS
D_FF = 1024
ALPHA = (2 * DEPTH) ** 0.25
EPS = 1e-5
NEG = -1e30

M_QK_W = M_HEADS * M_DQK
M_V_W = M_HEADS * M_DV
N_Q_W = N_HEADS * N_HD
N_KV_W = N_KV * N_HD
IN_SIZES = (M_QK_W, M_QK_W, M_V_W, M_V_W, M_HEADS, M_HEADS, N_Q_W, N_KV_W, N_KV_W, N_KV_W, N_KV_W,
            N_KV_W, N_KV_W, 3 * N_HEADS, 2 * D_MODEL)
IN_OFFS = tuple(int(v) for v in np.cumsum((0,) + IN_SIZES))

LANES = 128
MLSTM_CHUNK = 256
NSA_TQ = 256
NSA_GROUP = 4
MM_TM = 512
MM_TN = 1024
MOE_TM = 256
MOE_TD = 256
ROUTER_TM = 512

P1_MQ, P1_MK, P1_MV, P1_MO, P1_NVS, P1_NVW, P1_GM = 0, 1024, 2048, 4096, 6144, 6656, 7168
P1_W = 7168 + 2 * D_MODEL
P2_NQ, P2_NKS, P2_NKW = 0, 2048, 2560
P2_W = 3072
P3_GATES, P3_NG, P3_NKC, P3_NVC = 0, 128, 640, 1152
P3_W = 1664


def _params(sem, vmem_mb):
    return pltpu.CompilerParams(dimension_semantics=sem, vmem_limit_bytes=vmem_mb * 1024 * 1024)


def _nt(a, b, **kw):
    return lax.dot_general(a, b, (((1,), (1,)), ((), ())), preferred_element_type=F32, **kw)


def _tn(a, b):
    return lax.dot_general(a, b, (((0,), (0,)), ((), ())), preferred_element_type=F32)


def _standardize(x):
    mu = jnp.mean(x, -1, keepdims=True)
    xc = x - mu
    var = jnp.mean(xc * xc, -1, keepdims=True)
    return xc * lax.rsqrt(var + EPS)


def _mm_body(x_ref, w_ref, b_ref, *rest, rope, tn):
    acc = jnp.dot(x_ref[...], w_ref[...], preferred_element_type=F32) + b_ref[...]
    if rope:
        cos_ref, sin_ref, o_ref = rest
        cos = cos_ref[...]
        sin = sin_ref[...]
        for c in range(tn // N_HD):
            ch = acc[:, c * N_HD:(c + 1) * N_HD]
            o_ref[:, c * N_HD:(c + 1) * N_HD] = (ch * cos + pltpu.roll(ch, N_HD // 2, 1) * sin).astype(o_ref.dtype)
    else:
        (o_ref,) = rest
        o_ref[...] = acc.astype(o_ref.dtype)


def _matmul(x, w, b, out_dtype, tn, rope=None, seq=None):
    m, k = x.shape
    n = w.shape[1]
    tm = MM_TM
    grid = (n // tn, m // tm)
    in_specs = [pl.BlockSpec((tm, k), lambda j, i: (i, 0)),
                pl.BlockSpec((k, tn), lambda j, i: (0, j)),
                pl.BlockSpec((1, tn), lambda j, i: (0, j))]
    args = [x, w, b.reshape(1, n)]
    if rope is not None:
        nt = seq // tm
        in_specs += [pl.BlockSpec((tm, N_HD), lambda j, i: (i % nt, 0))] * 2
        args += list(rope)
    return pl.pallas_call(
        functools.partial(_mm_body, rope=rope is not None, tn=tn),
        grid=grid, in_specs=in_specs,
        out_specs=pl.BlockSpec((tm, tn), lambda j, i: (i, j)),
        out_shape=jax.ShapeDtypeStruct((m, n), out_dtype),
        compiler_params=_params(("parallel", "parallel"), 48),
        name="proj_matmul",
    )(*args)


def _log_sigmoid(x):
    return jnp.minimum(x, 0.0) - jnp.log1p(jnp.exp(-jnp.abs(x)))


def _mlstm_body(q_ref, k_ref, v_ref, og_ref, g_ref, gt_ref, nw_ref, out_ref, c_ref, n_ref, m_ref, *, L):
    @pl.when(pl.program_id(1) == 0)
    def _():
        c_ref[...] = jnp.zeros_like(c_ref)
        n_ref[...] = jnp.zeros_like(n_ref)
        m_ref[...] = jnp.zeros_like(m_ref)

    g = g_ref[...]
    gt = gt_ref[...]
    row = lax.broadcasted_iota(jnp.int32, (L, L), 0)
    col = lax.broadcasted_iota(jnp.int32, (L, L), 1)
    causal = row >= col
    b_all = jnp.dot(causal.astype(F32), _log_sigmoid(g), precision=HIGHEST, preferred_element_type=F32)
    bt_all = jnp.dot(_log_sigmoid(gt), (row <= col).astype(F32), precision=HIGHEST, preferred_element_type=F32)
    for h in range(M_HEADS):
        b_col = b_all[:, M_HEADS + h:M_HEADS + h + 1]
        li_col = g[:, h:h + 1]
        b_row = bt_all[M_HEADS + h:M_HEADS + h + 1, :]
        li_row = gt[h:h + 1, :]
        m_prev = m_ref[h][:, 0:1]
        dmat = jnp.where(causal, b_col - b_row + li_row, -jnp.inf)
        inter = b_col + m_prev
        m_t = jnp.maximum(inter, jnp.max(dmat, axis=1, keepdims=True))
        w_inter = jnp.exp(inter - m_t)
        q = q_ref[:, h * M_DQK:(h + 1) * M_DQK]
        k = k_ref[:, h * M_DQK:(h + 1) * M_DQK]
        v = v_ref[:, h * M_DV:(h + 1) * M_DV]
        s = _nt(q, k) * jnp.exp(dmat - m_t)
        ct = c_ref[h]
        n_row = n_ref[h]
        num = jnp.dot(s.astype(BF16), v, preferred_element_type=F32) + w_inter * jnp.dot(
            q, ct.astype(BF16), preferred_element_type=F32)
        qn = jnp.sum(q.astype(F32) * n_row, axis=1, keepdims=True)
        den = jnp.sum(s, axis=1, keepdims=True) + w_inter * qn
        hh = num / jnp.maximum(jnp.abs(den), jnp.exp(-m_t))
        hn = _standardize(hh) * nw_ref[:, h * M_DV:(h + 1) * M_DV]
        og = og_ref[:, h * M_DV:(h + 1) * M_DV].astype(F32)
        out_ref[:, h * M_DV:(h + 1) * M_DV] = (hn * jax.nn.sigmoid(og)).astype(out_ref.dtype)
        b_last = b_col[L - 1:L, :]
        g_col = b_last - b_col + li_col
        g_row = b_last - b_row + li_row
        m_new = jnp.maximum(b_last + m_prev, jnp.max(g_row, axis=1, keepdims=True))
        decay = jnp.exp(b_last + m_prev - m_new)
        ws_col = jnp.exp(g_col - m_new)
        kf = k.astype(F32)
        vw = (v.astype(F32) * ws_col).astype(BF16)
        c_ref[h] = decay * ct + _tn(k, vw)
        n_ref[h] = decay * n_row + jnp.sum(kf * ws_col, axis=0, keepdims=True)
        m_ref[h] = jnp.broadcast_to(m_new, (1, LANES))


def _mlstm(p1, p3, gt, norm_w, B, T):
    L = MLSTM_CHUNK
    p1 = p1.reshape(B, T, P1_W)
    p3 = p3.reshape(B, T, P3_W)
    return pl.pallas_call(
        functools.partial(_mlstm_body, L=L),
        grid=(B, T // L),
        in_specs=[pl.BlockSpec((None, L, M_QK_W), lambda b, c: (b, c, P1_MQ // M_QK_W)),
                  pl.BlockSpec((None, L, M_QK_W), lambda b, c: (b, c, P1_MK // M_QK_W)),
                  pl.BlockSpec((None, L, M_V_W), lambda b, c: (b, c, P1_MV // M_V_W)),
                  pl.BlockSpec((None, L, M_V_W), lambda b, c: (b, c, P1_MO // M_V_W)),
                  pl.BlockSpec((None, L, LANES), lambda b, c: (b, c, 0)),
                  pl.BlockSpec((None, 8, L), lambda b, c: (b, 0, c)),
                  pl.BlockSpec((1, M_V_W), lambda b, c: (0, 0))],
        out_specs=pl.BlockSpec((None, L, M_V_W), lambda b, c: (b, c, 0)),
        out_shape=jax.ShapeDtypeStruct((B, T, M_V_W), BF16),
        scratch_shapes=[pltpu.VMEM((M_HEADS, M_DQK, M_DV), F32),
                        pltpu.VMEM((M_HEADS, 1, M_DQK), F32),
                        pltpu.VMEM((M_HEADS, 1, LANES), F32)],
        compiler_params=_params(("parallel", "arbitrary"), 48),
        name="mlstm",
    )(p1, p1, p1, p1, p3, gt, norm_w.reshape(1, M_V_W))


def _compress_body(x_ref, pos_ref, w1_ref, b1_ref, w2_ref, b2_ref, cos_ref, sin_ref, o_ref, *, rope, nb):
    x = x_ref[...]
    half = CMP_STRIDE * N_HD
    lo = jnp.dot((x + pos_ref[:, :half]).astype(BF16), w1_ref[:half, :], preferred_element_type=F32)
    hi = jnp.dot((x + pos_ref[:, half:]).astype(BF16), w1_ref[half:, :], preferred_element_type=F32)
    pre = lo + pltpu.roll(hi, nb - 1, 0) + b1_ref[...]
    h = jax.nn.gelu(pre)
    y = jnp.dot(h.astype(BF16), w2_ref[...], preferred_element_type=F32) + b2_ref[...]
    if rope:
        y = y * cos_ref[...] + pltpu.roll(y, N_HD // 2, 1) * sin_ref[...]
    keep = lax.broadcasted_iota(jnp.int32, (nb, N_HD), 0) < nb - 1
    o_ref[...] = jnp.where(keep, y, 0.0).astype(o_ref.dtype)


def _compress(x2, pos, w1, b1, w2, b2, cos, sin, rope):
    B, G, nb, _ = x2.shape
    full = lambda shape: pl.BlockSpec(shape, lambda b, g: (0,) * len(shape))
    return pl.pallas_call(
        functools.partial(_compress_body, rope=rope, nb=nb),
        grid=(B, G),
        in_specs=[pl.BlockSpec((None, None, nb, CMP_STRIDE * N_HD), lambda b, g: (b, g, 0, 0)),
                  full((1, CMP_BLOCK * N_HD)), full((CMP_BLOCK * N_HD, N_HD)), full((1, N_HD)),
                  full((N_HD, N_HD)), full((1, N_HD)), full((nb, N_HD)), full((nb, N_HD))],
        out_specs=pl.BlockSpec((None, None, nb, N_HD), lambda b, g: (b, g, 0, 0)),
        out_shape=jax.ShapeDtypeStruct((B, G, nb, N_HD), BF16),
        compiler_params=_params(("parallel", "parallel"), 32),
        name="nsa_compress",
    )(x2, pos.reshape(1, -1), w1.reshape(CMP_BLOCK * N_HD, N_HD).astype(BF16), b1.reshape(1, N_HD),
      w2.astype(BF16), b2.reshape(1, N_HD), cos, sin)


def _nsa_body(q_ref, ks_ref, vst_ref, kw_ref, vwt_ref, kc_ref, vct_ref, gt_ref, ovt_ref, o_ref,
              acc_ref, m_ref, l_ref, sel_ref, out_ref, s_ref, p_ref, *, tq, ncmp):
    qi = pl.program_id(2)
    t0 = qi * tq
    nsel = ovt_ref.shape[0]
    gates = jax.nn.sigmoid(gt_ref[...])
    q_heads = [q_ref[:, h * N_HD:(h + 1) * N_HD] for h in range(N_HPG)]

    kc = kc_ref[...]
    vct = vct_ref[...]
    n_idx = lax.broadcasted_iota(jnp.int32, (ncmp, tq), 0)
    t_cmp = t0 + lax.broadcasted_iota(jnp.int32, (ncmp, tq), 1)
    cmask = (n_idx * CMP_STRIDE + (CMP_BLOCK - 1) <= t_cmp) & (n_idx < ncmp - 1)
    psum = jnp.zeros((ncmp, tq), F32)
    for h in range(N_HPG):
        s_ref[0, h, :ncmp, :] = _nt(kc, q_heads[h])
    for h in range(N_HPG):
        s = jnp.where(cmask, s_ref[0, h, :ncmp, :], NEG)
        e = jnp.where(cmask, jnp.exp(s - jnp.max(s, axis=0, keepdims=True)), 0.0)
        p = e * (1.0 / jnp.maximum(jnp.sum(e, axis=0, keepdims=True), 1e-30))
        psum = psum + p
        p_ref[0, h, :ncmp, :] = p.astype(BF16)
    for h in range(N_HPG):
        out_ref[h] = gates[3 * h:3 * h + 1, :] * jnp.dot(vct, p_ref[0, h, :ncmp, :], preferred_element_type=F32)

    imp = jnp.dot(ovt_ref[...], psum, precision=HIGHEST, preferred_element_type=F32)
    j_idx = lax.broadcasted_iota(jnp.int32, (nsel, tq), 0)
    t_row = t0 + lax.broadcasted_iota(jnp.int32, (nsel, tq), 1)
    cur = lax.shift_right_logical(t_row, 6)
    forced = (j_idx == 0) | (j_idx == cur) | (j_idx == cur - 1)
    score = jnp.where(forced, jnp.inf, jnp.where(j_idx * SEL_BLOCK <= t_row, imp, -jnp.inf))

    def pick(_, carry):
        score, sel = carry
        mx = jnp.max(score, axis=0, keepdims=True)
        idx = jnp.min(jnp.where(score == mx, j_idx, nsel), axis=0, keepdims=True)
        hit = j_idx == idx
        return jnp.where(hit, -jnp.inf, score), jnp.where(hit, 1.0, sel)

    _, sel = lax.fori_loop(0, min(SEL_TOPK, nsel), pick, (score, jnp.zeros((nsel, tq), F32)), unroll=True)
    sel_ref[...] = sel

    def flash_init():
        acc_ref[...] = jnp.zeros_like(acc_ref)
        l_ref[...] = jnp.zeros_like(l_ref)
        m_ref[...] = jnp.full(m_ref.shape, NEG, F32)

    def flash_tiles(k_ref, vt_ref, tiles):
        vts = []
        for n, (j, _) in enumerate(tiles):
            start = pl.multiple_of(j * tq, tq)
            k = k_ref[pl.ds(start, tq), :]
            vts.append(vt_ref[:, pl.ds(start, tq)])
            for h in range(N_HPG):
                s_ref[n, h] = _nt(k, q_heads[h])
        for n, (_, mask) in enumerate(tiles):
            alphas = []
            for h in range(N_HPG):
                s = s_ref[n, h]
                if mask is not None:
                    s = jnp.where(mask, s, NEG)
                m_old = m_ref[h]
                m_new = jnp.maximum(m_old, jnp.max(s, axis=0, keepdims=True))
                p = jnp.exp(s - m_new)
                alpha = jnp.exp(m_old - m_new)
                l_ref[h] = alpha * l_ref[h] + jnp.sum(p, axis=0, keepdims=True)
                p_ref[n, h] = p.astype(BF16)
                m_ref[h] = m_new
                alphas.append(alpha)
            for h in range(N_HPG):
                acc_ref[h] = alphas[h] * acc_ref[h] + jnp.dot(vts[n], p_ref[n, h], preferred_element_type=F32)

    def flash_add(branch):
        for h in range(N_HPG):
            out_ref[h] = out_ref[h] + (gates[3 * h + branch:3 * h + branch + 1, :] / l_ref[h]) * acc_ref[h]

    kpos = lax.broadcasted_iota(jnp.int32, (tq, tq), 0)
    qpos = lax.broadcasted_iota(jnp.int32, (tq, tq), 1)
    blk_per_tile = tq // SEL_BLOCK

    def sel_mask(j):
        rows = [jnp.broadcast_to(sel_ref[pl.ds(j * blk_per_tile + b, 1), :], (SEL_BLOCK, tq))
                for b in range(blk_per_tile)]
        return jnp.concatenate(rows, axis=0) > 0.5

    flash_init()

    group = s_ref.shape[0]

    def sel_group(p, carry):
        flash_tiles(ks_ref, vst_ref, [(group * p + n, sel_mask(group * p + n)) for n in range(group)])
        return carry

    lax.fori_loop(0, qi // group, sel_group, 0)
    causal = kpos <= qpos
    for rem in range(group):
        @pl.when(qi % group == rem)
        def _():
            tiles = [(qi - rem + n, sel_mask(qi - rem + n)) for n in range(rem)]
            flash_tiles(ks_ref, vst_ref, tiles + [(qi, sel_mask(qi) & causal)])

    flash_add(1)

    flash_init()
    n_back = WINDOW // tq
    win_tiles = [(qi, causal)] + [(qi - back, None) for back in range(1, n_back)] + [(qi - n_back, kpos > qpos)]
    for n_tiles in range(1, n_back + 2):
        last = n_tiles == n_back + 1

        @pl.when((qi >= n_tiles - 1) if last else (qi == n_tiles - 1))
        def _():
            flash_tiles(kw_ref, vwt_ref, win_tiles[:n_tiles])
    flash_add(2)

    for h in range(N_HPG):
        o_ref[:, h * N_HD:(h + 1) * N_HD] = out_ref[h].T.astype(o_ref.dtype)


def _nsa(p1, p2, p3, k_cmp, v_cmp, ovt, B, T):
    tq = NSA_TQ
    ncmp = k_cmp.shape[2]
    p1 = p1.reshape(B, T, P1_W)
    p2 = p2.reshape(B, T, P2_W)
    p3 = p3.reshape(B, T, P3_W)
    hw = N_HPG * N_HD
    assert WINDOW % tq == 0 and tq % SEL_BLOCK == 0 and ncmp <= tq and NSA_GROUP >= WINDOW // tq + 1
    vst = p1[:, :, P1_NVS:P1_NVS + N_KV_W].transpose(0, 2, 1)
    vwt = p1[:, :, P1_NVW:P1_NVW + N_KV_W].transpose(0, 2, 1)
    vct = v_cmp.transpose(0, 1, 3, 2)
    gt = p3[:, :, P3_NG:P3_NG + N_KV * LANES].reshape(B, T, N_KV, LANES)[..., :16].transpose(0, 2, 3, 1)
    k_spec = lambda off: pl.BlockSpec((None, T, N_HD), lambda b, g, i: (b, 0, off // N_HD + g))
    vt_spec = pl.BlockSpec((None, N_HD, T), lambda b, g, i: (b, g, 0))
    return pl.pallas_call(
        functools.partial(_nsa_body, tq=tq, ncmp=ncmp),
        grid=(B, N_KV, T // tq),
        in_specs=[pl.BlockSpec((None, tq, hw), lambda b, g, i: (b, i, P2_NQ // hw + g)),
                  k_spec(P2_NKS), vt_spec, k_spec(P2_NKW), vt_spec,
                  pl.BlockSpec((None, None, ncmp, N_HD), lambda b, g, i: (b, g, 0, 0)),
                  pl.BlockSpec((None, None, N_HD, ncmp), lambda b, g, i: (b, g, 0, 0)),
                  pl.BlockSpec((None, None, 16, tq), lambda b, g, i: (b, g, 0, i)),
                  pl.BlockSpec(ovt.shape, lambda b, g, i: (0, 0))],
        out_specs=pl.BlockSpec((None, tq, hw), lambda b, g, i: (b, i, g)),
        out_shape=jax.ShapeDtypeStruct((B, T, N_Q_W), BF16),
        scratch_shapes=[pltpu.VMEM((N_HPG, N_HD, tq), F32), pltpu.VMEM((N_HPG, 1, tq), F32),
                        pltpu.VMEM((N_HPG, 1, tq), F32), pltpu.VMEM((ovt.shape[0], tq), F32),
                        pltpu.VMEM((N_HPG, N_HD, tq), F32),
                        pltpu.VMEM((NSA_GROUP, N_HPG, tq, tq), F32),
                        pltpu.VMEM((NSA_GROUP, N_HPG, tq, tq), BF16)],
        compiler_params=_params(("parallel", "parallel", "arbitrary"), 48),
        name="nsa_attention",
    )(p2, p2, vst, p2, vwt, k_cmp, vct, gt, ovt)


def _merge_body(hm_ref, hn_ref, wm_ref, wn_ref, gm_ref, gn_ref, o_ref):
    ym = jnp.dot(hm_ref[...], wm_ref[...], preferred_element_type=F32)
    yn = jnp.dot(hn_ref[...], wn_ref[...], preferred_element_type=F32)
    gm = jax.nn.sigmoid(gm_ref[...].astype(F32))
    gn = jax.nn.sigmoid(gn_ref[...].astype(F32))
    o_ref[...] = (gm * ym + gn * yn).astype(o_ref.dtype)


def _merge(hm, hn, wm, wn, p1):
    m = hm.shape[0]
    tm, tn = MM_TM, 512
    return pl.pallas_call(
        _merge_body,
        grid=(D_MODEL // tn, m // tm),
        in_specs=[pl.BlockSpec((tm, M_V_W), lambda j, i: (i, 0)),
                  pl.BlockSpec((tm, N_Q_W), lambda j, i: (i, 0)),
                  pl.BlockSpec((M_V_W, tn), lambda j, i: (0, j)),
                  pl.BlockSpec((N_Q_W, tn), lambda j, i: (0, j)),
                  pl.BlockSpec((tm, tn), lambda j, i: (i, P1_GM // tn + j)),
                  pl.BlockSpec((tm, tn), lambda j, i: (i, (P1_GM + D_MODEL) // tn + j))],
        out_specs=pl.BlockSpec((tm, tn), lambda j, i: (i, j)),
        out_shape=jax.ShapeDtypeStruct((m, D_MODEL), BF16),
        compiler_params=_params(("parallel", "parallel"), 48),
        name="branch_merge",
    )(hm, hn, wm, wn, p1, p1)


def _out_ln_body(y_ref, w_ref, x_ref, g_ref, b_ref, o_ref, ob_ref):
    mix = jnp.dot(y_ref[...], w_ref[...], preferred_element_type=F32)
    out = _standardize(ALPHA * x_ref[...] + mix) * g_ref[...] + b_ref[...]
    o_ref[...] = out
    ob_ref[...] = out.astype(BF16)


def _out_ln(y, w, x, g, b):
    m = y.shape[0]
    tm = MM_TM
    row = pl.BlockSpec((tm, D_MODEL), lambda i: (i, 0))
    vec = pl.BlockSpec((1, D_MODEL), lambda i: (0, 0))
    return pl.pallas_call(
        _out_ln_body,
        grid=(m // tm,),
        in_specs=[row, pl.BlockSpec((D_MODEL, D_MODEL), lambda i: (0, 0)), row, vec, vec],
        out_specs=[row, row],
        out_shape=[jax.ShapeDtypeStruct((m, D_MODEL), F32), jax.ShapeDtypeStruct((m, D_MODEL), BF16)],
        compiler_params=_params(("parallel",), 48),
        name="out_proj_layernorm",
    )(y, w, x, g.reshape(1, -1), b.reshape(1, -1))


def _first_of4(vals, target):
    return jnp.where(vals[0] == target, 0.0, jnp.where(vals[1] == target, 1.0, jnp.where(vals[2] == target, 2.0, 3.0)))


def _select4(idx, vals):
    return jnp.where(idx == 0.0, vals[0], jnp.where(idx == 1.0, vals[1], jnp.where(idx == 2.0, vals[2], vals[3])))


def _router_body(x_ref, rw_ref, rb_ref, o_ref, cnt_ref, carry_ref, *, tm):
    @pl.when(pl.program_id(0) == 0)
    def _():
        carry_ref[...] = jnp.zeros_like(carry_ref)

    logits = _nt(rw_ref[...], x_ref[...], precision=HIGHEST)
    aff = jax.nn.sigmoid(logits)
    biased = aff + rb_ref[:, 0:1]
    a = [biased[i * N_GROUPS:(i + 1) * N_GROUPS, :] for i in range(EXP_PER_GROUP)]
    af = [aff[i * N_GROUPS:(i + 1) * N_GROUPS, :] for i in range(EXP_PER_GROUP)]
    m1 = jnp.maximum(jnp.maximum(a[0], a[1]), jnp.maximum(a[2], a[3]))
    i1 = _first_of4(a, m1)
    rest = [jnp.where(i1 == float(i), -jnp.inf, a[i]) for i in range(EXP_PER_GROUP)]
    m2 = jnp.maximum(jnp.maximum(rest[0], rest[1]), jnp.maximum(rest[2], rest[3]))
    i2 = _first_of4(rest, m2)
    gscore = m1 + m2
    g_iota = lax.broadcasted_iota(jnp.int32, (N_GROUPS, tm), 0).astype(F32)
    g_idx = jnp.min(jnp.where(gscore == jnp.max(gscore, axis=0, keepdims=True), g_iota, float(N_GROUPS)),
                    axis=0, keepdims=True)
    in_g = g_iota == g_idx
    take = lambda v: jnp.sum(jnp.where(in_g, v, 0.0), axis=0, keepdims=True)
    s0 = take(i1)
    s1 = take(i2)
    w0 = take(_select4(i1, af))
    w1 = take(_select4(i2, af))
    wsum = w0 + w1
    r0 = s0 * N_GROUPS + g_idx
    r1 = s1 * N_GROUPS + g_idx
    r_iota = lax.broadcasted_iota(jnp.int32, (N_EXPERTS, tm), 0).astype(F32)
    member = (r_iota == r0) | (r_iota == r1)
    tt = lax.broadcasted_iota(jnp.int32, (tm, tm), 0)
    tc = lax.broadcasted_iota(jnp.int32, (tm, tm), 1)
    before = jnp.dot(member.astype(BF16), (tt < tc).astype(BF16), preferred_element_type=F32)
    base = before + carry_ref[:, 0:1]
    rank0 = jnp.sum(jnp.where(r_iota == r0, base, 0.0), axis=0, keepdims=True)
    rank1 = jnp.sum(jnp.where(r_iota == r1, base, 0.0), axis=0, keepdims=True)
    new_carry = carry_ref[:, 0:1] + jnp.sum(member.astype(F32), axis=1, keepdims=True)
    carry_ref[...] = jnp.broadcast_to(new_carry, carry_ref.shape)
    cnt_ref[...] = jnp.broadcast_to(new_carry, cnt_ref.shape)
    o_ref[0:1, :] = g_idx * EXP_PER_GROUP + s0
    o_ref[1:2, :] = g_idx * EXP_PER_GROUP + s1
    o_ref[2:3, :] = w0 / wsum
    o_ref[3:4, :] = w1 / wsum
    o_ref[4:5, :] = rank0
    o_ref[5:6, :] = rank1
    o_ref[6:8, :] = jnp.zeros((2, tm), F32)


def _router(x, rw_t, rb):
    m = x.shape[0]
    tm = ROUTER_TM
    return pl.pallas_call(
        functools.partial(_router_body, tm=tm),
        grid=(m // tm,),
        in_specs=[pl.BlockSpec((tm, D_MODEL), lambda i: (i, 0)),
                  pl.BlockSpec((N_EXPERTS, D_MODEL), lambda i: (0, 0)),
                  pl.BlockSpec((N_EXPERTS, LANES), lambda i: (0, 0))],
        out_specs=[pl.BlockSpec((8, tm), lambda i: (0, i)),
                   pl.BlockSpec((N_EXPERTS, LANES), lambda i: (0, 0))],
        out_shape=[jax.ShapeDtypeStruct((8, m), F32), jax.ShapeDtypeStruct((N_EXPERTS, LANES), F32)],
        scratch_shapes=[pltpu.VMEM((N_EXPERTS, LANES), F32)],
        compiler_params=_params(("arbitrary",), 48),
        name="moe_router",
    )(x, rw_t, rb)


def _row_copy(src_ref, src_row, dst_ref, dst_row, sem):
    return pltpu.make_async_copy(src_ref.at[pl.ds(src_row, 1), :], dst_ref.at[pl.ds(dst_row, 1), :], sem)


def _dispatch_body(dest_ref, x_ref, init_ref, xs_ref, sem, *, td):
    del init_ref

    def issue(r, c):
        _row_copy(x_ref, r, xs_ref, dest_ref[0, 0, r], sem).start(priority=0)
        _row_copy(x_ref, r, xs_ref, dest_ref[0, 0, td + r], sem).start(priority=1)
        return c

    lax.fori_loop(0, td, issue, 0)

    def drain(r, c):
        _row_copy(x_ref, 0, xs_ref, 0, sem).wait()
        _row_copy(x_ref, 0, xs_ref, 0, sem).wait()
        return c

    lax.fori_loop(0, td, drain, 0)


def _dispatch(x, dest, rows):
    m = x.shape[0]
    td = MOE_TD
    return pl.pallas_call(
        functools.partial(_dispatch_body, td=td),
        grid=(m // td,),
        in_specs=[pl.BlockSpec((1, 1, 2 * td), lambda i: (i, 0, 0), memory_space=pltpu.SMEM),
                  pl.BlockSpec((td, D_MODEL), lambda i: (i, 0)),
                  pl.BlockSpec(memory_space=pl.ANY)],
        out_specs=pl.BlockSpec(memory_space=pl.ANY),
        out_shape=jax.ShapeDtypeStruct((rows, D_MODEL), F32),
        scratch_shapes=[pltpu.SemaphoreType.DMA(())],
        input_output_aliases={2: 0},
        compiler_params=_params(("arbitrary",), 32),
        name="moe_dispatch",
    )(dest, x, jnp.zeros((rows, D_MODEL), F32))


def _expert_changed(te_ref, i):
    return (i == 0) | (te_ref[i] != te_ref[jnp.maximum(i - 1, 0)])


def _experts_up_body(te_ref, nused_ref, x_ref, w1_ref, w3_ref, h_ref, w1b_ref, w3b_ref):
    i = pl.program_id(0)
    used = i < nused_ref[0]

    @pl.when(used & _expert_changed(te_ref, i))
    def _():
        w1b_ref[...] = w1_ref[...].astype(BF16)
        w3b_ref[...] = w3_ref[...].astype(BF16)

    @pl.when(used)
    def _():
        xb = x_ref[...].astype(BF16)
        a = jnp.dot(xb, w1b_ref[...], preferred_element_type=F32)
        b = jnp.dot(xb, w3b_ref[...], preferred_element_type=F32)
        h_ref[...] = (a * jax.nn.sigmoid(a) * b).astype(BF16)

    @pl.when(jnp.logical_not(used))
    def _():
        h_ref[...] = jnp.zeros_like(h_ref)


def _experts_down_body(te_ref, nused_ref, h_ref, w2_ref, o_ref, w2b_ref):
    i = pl.program_id(0)
    used = i < nused_ref[0]

    @pl.when(used & _expert_changed(te_ref, i))
    def _():
        w2b_ref[...] = w2_ref[...].astype(BF16)

    @pl.when(used)
    def _():
        o_ref[...] = jnp.dot(h_ref[...], w2b_ref[...], preferred_element_type=F32)

    @pl.when(jnp.logical_not(used))
    def _():
        o_ref[...] = jnp.zeros_like(o_ref)


def _experts(xs, w1, w3, w2, layer, tile_expert, n_used):
    rows = xs.shape[0]
    tm = MOE_TM
    by_expert = lambda r, c: pl.BlockSpec((None, None, r, c), lambda i, te, nu: (layer, te[i], 0, 0))
    by_tile = lambda c: pl.BlockSpec((tm, c), lambda i, te, nu: (i, 0))
    h = pl.pallas_call(
        _experts_up_body,
        grid_spec=pltpu.PrefetchScalarGridSpec(
            num_scalar_prefetch=2, grid=(rows // tm,),
            in_specs=[by_tile(D_MODEL), by_expert(D_MODEL, D_FF), by_expert(D_MODEL, D_FF)],
            out_specs=by_tile(D_FF),
            scratch_shapes=[pltpu.VMEM((D_MODEL, D_FF), BF16), pltpu.VMEM((D_MODEL, D_FF), BF16)]),
        out_shape=jax.ShapeDtypeStruct((rows, D_FF), BF16),
        compiler_params=_params(("arbitrary",), 56),
        name="moe_experts_up",
    )(tile_expert, n_used, xs, w1, w3)
    return pl.pallas_call(
        _experts_down_body,
        grid_spec=pltpu.PrefetchScalarGridSpec(
            num_scalar_prefetch=2, grid=(rows // tm,),
            in_specs=[by_tile(D_FF), by_expert(D_FF, D_MODEL)],
            out_specs=by_tile(D_MODEL),
            scratch_shapes=[pltpu.VMEM((D_FF, D_MODEL), BF16)]),
        out_shape=jax.ShapeDtypeStruct((rows, D_MODEL), F32),
        compiler_params=_params(("arbitrary",), 40),
        name="moe_experts_down",
    )(tile_expert, n_used, h, w2)


def _combine_body(dest_ref, ys_ref, x_ref, w_ref, g_ref, b_ref, o_ref, ob_ref, buf_ref, sem, *, td):
    def issue(r, c):
        _row_copy(ys_ref, dest_ref[0, 0, r], buf_ref.at[0], r, sem).start(priority=0)
        _row_copy(ys_ref, dest_ref[0, 0, td + r], buf_ref.at[1], r, sem).start(priority=1)
        return c

    lax.fori_loop(0, td, issue, 0)

    def drain(r, c):
        _row_copy(ys_ref, 0, buf_ref.at[0], 0, sem).wait()
        _row_copy(ys_ref, 0, buf_ref.at[1], 0, sem).wait()
        return c

    lax.fori_loop(0, td, drain, 0)
    w = w_ref[...]
    ff = w[:, 0:1] * buf_ref[0] + w[:, 1:2] * buf_ref[1]
    out = _standardize(ALPHA * x_ref[...] + ff) * g_ref[...] + b_ref[...]
    o_ref[...] = out
    ob_ref[...] = out.astype(BF16)


def _combine(ys, dest, x, w, g, b):
    m = x.shape[0]
    td = MOE_TD
    row = pl.BlockSpec((td, D_MODEL), lambda i: (i, 0))
    vec = pl.BlockSpec((1, D_MODEL), lambda i: (0, 0))
    return pl.pallas_call(
        functools.partial(_combine_body, td=td),
        grid=(m // td,),
        in_specs=[pl.BlockSpec((1, 1, 2 * td), lambda i: (i, 0, 0), memory_space=pltpu.SMEM),
                  pl.BlockSpec(memory_space=pl.ANY), row,
                  pl.BlockSpec((td, 8), lambda i: (i, 0)), vec, vec],
        out_specs=[row, row],
        out_shape=[jax.ShapeDtypeStruct((m, D_MODEL), F32), jax.ShapeDtypeStruct((m, D_MODEL), BF16)],
        scratch_shapes=[pltpu.VMEM((2, td, D_MODEL), F32), pltpu.SemaphoreType.DMA(())],
        compiler_params=_params(("arbitrary",), 32),
        name="moe_combine",
    )(dest, ys, x, w, g.reshape(1, -1), b.reshape(1, -1))


def _moe(x, xb, rw_t, rb, w1, w3, w2, layer, ln_g, ln_b):
    del xb
    m = x.shape[0]
    tm, td = MOE_TM, MOE_TD
    rows = 2 * m + N_EXPERTS * tm
    ro, cnt = _router(x, rw_t, rb)
    e0 = ro[0].astype(jnp.int32)
    e1 = ro[1].astype(jnp.int32)
    counts = cnt[:, 0].astype(jnp.int32).reshape(EXP_PER_GROUP, N_GROUPS).T.reshape(N_EXPERTS)
    padded = (counts + tm - 1) // tm * tm
    ends = jnp.cumsum(padded)
    offs = ends - padded
    d0 = offs[e0] + ro[4].astype(jnp.int32)
    d1 = offs[e1] + ro[5].astype(jnp.int32)
    dest = jnp.concatenate([d0.reshape(m // td, 1, td), d1.reshape(m // td, 1, td)], axis=-1)
    tile_start = jnp.arange(rows // tm, dtype=jnp.int32) * tm
    tile_expert = jnp.minimum(jnp.sum(tile_start[:, None] >= ends[None, :], axis=1), N_EXPERTS - 1).astype(jnp.int32)
    n_used = (ends[-1:] // tm).astype(jnp.int32)
    xs = _dispatch(x, dest, rows)
    ys = _experts(xs, w1, w3, w2, layer, tile_expert, n_used)
    wcol = jnp.pad(ro[2:4].T, ((0, 0), (0, 6)))
    return _combine(ys, dest, x, wcol, ln_g, ln_b)


def _rope_tables(pos):
    half = N_HD // 2
    inv = ROPE_THETA ** (-jnp.arange(half, dtype=F32) / half)
    ang = pos.astype(F32)[:, None] * inv[None, :]
    cos, sin = jnp.cos(ang), jnp.sin(ang)
    return jnp.concatenate([cos, cos], -1), jnp.concatenate([-sin, sin], -1)


def _overlap_t(T, ncmp_pad):
    n_cmp = (T - CMP_BLOCK) // CMP_STRIDE + 1
    n_sel = T // SEL_BLOCK
    cs = np.arange(n_cmp) * CMP_STRIDE
    ss = np.arange(n_sel) * SEL_BLOCK
    ov = ((cs[:, None] <= ss[None, :] + SEL_BLOCK - 1) & (cs[:, None] + CMP_BLOCK - 1 >= ss[None, :])).astype(np.float32)
    out = np.zeros((n_sel, ncmp_pad), np.float32)
    out[:, :n_cmp] = ov.T
    return jnp.asarray(out)


def _pack_body(w_ref, o1_ref, o2_ref, o3_ref):
    (mq, mk, mv, mo, mi, mf, nq, nkc, nvc, nks, nvs, nkw, nvw, ng, gm, end) = IN_OFFS

    def put(o_ref, off, a, b, scale=1.0):
        o_ref[:, off:off + b - a] = (w_ref[:, a:b] * scale).astype(o_ref.dtype)

    put(o1_ref, P1_MQ, mq, mk, M_DQK ** -0.5)
    put(o1_ref, P1_MK, mk, mo)
    put(o1_ref, P1_MO, mo, mi)
    put(o1_ref, P1_NVS, nvs, nkw)
    put(o1_ref, P1_NVW, nvw, ng)
    put(o1_ref, P1_GM, gm, end)
    put(o2_ref, P2_NQ, nq, nkc, N_HD ** -0.5)
    put(o2_ref, P2_NKS, nks, nvs)
    put(o2_ref, P2_NKW, nkw, nvw)
    o3_ref[:, 0:P3_NKC] = jnp.zeros((o3_ref.shape[0], P3_NKC), o3_ref.dtype)
    put(o3_ref, P3_GATES, mi, nq)
    for g in range(N_KV):
        put(o3_ref, P3_NG + g * LANES, ng + g * 3 * N_HPG, ng + (g + 1) * 3 * N_HPG)
    put(o3_ref, P3_NKC, nkc, nks)


def _pack(w, layer, out_dtype, tr):
    rows = w.shape[1]
    out = lambda n: pl.BlockSpec((tr, n), lambda i: (i, 0))
    return pl.pallas_call(
        _pack_body,
        grid=(rows // tr,),
        in_specs=[pl.BlockSpec((None, tr, IN_OFFS[-1]), lambda i: (layer, i, 0))],
        out_specs=[out(P1_W), out(P2_W), out(P3_W)],
        out_shape=[jax.ShapeDtypeStruct((rows, n), out_dtype) for n in (P1_W, P2_W, P3_W)],
        compiler_params=_params(("parallel",), 56),
        name="pack_in_proj",
    )(w)


def kernel(x, w_in, b_in, m_norm_w, cmp_pos_k, cmp_w1_k, cmp_b1_k, cmp_w2_k, cmp_b2_k, cmp_pos_v, cmp_w1_v,
           cmp_b1_v, cmp_w2_v, cmp_b2_v, w_branch_m, w_branch_n, w_out, ln1_g, ln1_b, router_w, router_b,
           exp_w1, exp_w3, exp_w2, ln2_g, ln2_b):
    B, T, D = x.shape
    m = B * T
    nb = T // CMP_STRIDE
    cos_t, sin_t = _rope_tables(jnp.arange(T))
    cos_c, sin_c = _rope_tables(jnp.arange(nb) * CMP_STRIDE + CMP_BLOCK - 1)
    ovt = _overlap_t(T, nb)
    perm = (np.arange(N_GROUPS)[None, :] * EXP_PER_GROUP + np.arange(EXP_PER_GROUP)[:, None]).reshape(-1)
    rw_t = router_w.T[perm]
    rb = jnp.broadcast_to(router_b[perm][:, None], (N_EXPERTS, LANES))

    b_rows = jnp.pad(b_in[:, None, :], ((0, 0), (0, 7), (0, 0)))
    xf = x.reshape(m, D)
    xb = xf.astype(BF16)
    for l in range(DEPTH):
        w1p, w2p, w3p = _pack(w_in, l, BF16, 256)
        b1p, b2p, b3p = (b[0] for b in _pack(b_rows, l, F32, 8))
        p1 = _matmul(xb, w1p, b1p, BF16, MM_TN)
        p2 = _matmul(xb, w2p, b2p, BF16, MM_TN, rope=(cos_t, sin_t), seq=T)
        p3 = _matmul(xb, w3p, b3p, F32, P3_W)
        gt = p3.reshape(B, T, P3_W)[:, :, :8].transpose(0, 2, 1)
        h_m = _mlstm(p1, p3, gt, m_norm_w[l], B, T).reshape(m, M_V_W)

        def blocks16(off):
            a = p3.reshape(B, nb, CMP_STRIDE, P3_W)[..., off:off + N_KV_W]
            return a.reshape(B, nb, CMP_STRIDE, N_KV, N_HD).transpose(0, 3, 1, 2, 4).reshape(B, N_KV, nb, CMP_STRIDE * N_HD)

        k_cmp = _compress(blocks16(P3_NKC), cmp_pos_k[l], cmp_w1_k[l], cmp_b1_k[l], cmp_w2_k[l], cmp_b2_k[l],
                          cos_c, sin_c, True)
        v_cmp = _compress(blocks16(P3_NVC), cmp_pos_v[l], cmp_w1_v[l], cmp_b1_v[l], cmp_w2_v[l], cmp_b2_v[l],
                          cos_c, sin_c, False)
        h_n = _nsa(p1, p2, p3, k_cmp, v_cmp, ovt, B, T).reshape(m, N_Q_W)
        merged = _merge(h_m, h_n, w_branch_m[l].astype(BF16), w_branch_n[l].astype(BF16), p1)
        xf, xb = _out_ln(merged, w_out[l].astype(BF16), xf, ln1_g[l], ln1_b[l])
        xf, xb = _moe(xf, xb, rw_t, rb, exp_w1, exp_w3, exp_w2, l, ln2_g[l], ln2_b[l])
    return xf.reshape(B, T, D)
```

```python
import functools

import numpy as np
import jax
import jax.numpy as jnp
from jax import lax
from jax.experimental import pallas as pl
from jax.experimental.pallas import tpu as pltpu

F32 = jnp.float32
BF16 = jnp.bfloat16
HIGHEST = lax.Precision.HIGHEST

D_MODEL = 2048
DEPTH = 2
M_HEADS = 4
M_DQK = 256
M_DV = 512
N_KV = 4
N_HPG = 4
N_HEADS = N_KV * N_HPG
N_HD = 128
CMP_BLOCK = 32
CMP_STRIDE = 16
SEL_BLOCK = 64
SEL_TOPK = 16
WINDOW = 512
ROPE_THETA = 10000.0
N_EXPERTS = 32
N_GROUPS = 8
EXP_PER_GROUP = N_EXPERTS // N_GROUPS
D_FF = 1024
ALPHA = (2 * DEPTH) ** 0.25
EPS = 1e-5
NEG = -1e30

M_QK_W = M_HEADS * M_DQK
M_V_W = M_HEADS * M_DV
N_Q_W = N_HEADS * N_HD
N_KV_W = N_KV * N_HD
IN_SIZES = (M_QK_W, M_QK_W, M_V_W, M_V_W, M_HEADS, M_HEADS, N_Q_W, N_KV_W, N_KV_W, N_KV_W, N_KV_W,
            N_KV_W, N_KV_W, 3 * N_HEADS, 2 * D_MODEL)
IN_OFFS = tuple(int(v) for v in np.cumsum((0,) + IN_SIZES))

LANES = 128
MLSTM_CHUNK = 256
NSA_TQ = 256
NSA_GROUP = 4
MM_TM = 512
MM_TN = 1024
MOE_TM = 256
MOE_TD = 256
ROUTER_TM = 512

P1_MQ, P1_MK, P1_MV, P1_MO, P1_NVS, P1_NVW, P1_GM = 0, 1024, 2048, 4096, 6144, 6656, 7168
P1_W = 7168 + 2 * D_MODEL
P2_NQ, P2_NKS, P2_NKW = 0, 2048, 2560
P2_W = 3072
P3_GATES, P3_NG, P3_NKC, P3_NVC = 0, 128, 640, 1152
P3_W = 1664


def _params(sem, vmem_mb):
    return pltpu.CompilerParams(dimension_semantics=sem, vmem_limit_bytes=vmem_mb * 1024 * 1024)


def _nt(a, b, **kw):
    return lax.dot_general(a, b, (((1,), (1,)), ((), ())), preferred_element_type=F32, **kw)


def _tn(a, b):
    return lax.dot_general(a, b, (((0,), (0,)), ((), ())), preferred_element_type=F32)


def _standardize(x):
    mu = jnp.mean(x, -1, keepdims=True)
    xc = x - mu
    var = jnp.mean(xc * xc, -1, keepdims=True)
    return xc * lax.rsqrt(var + EPS)


def _mm_body(x_ref, w_ref, b_ref, *rest, rope, tn):
    acc = jnp.dot(x_ref[...], w_ref[...], preferred_element_type=F32) + b_ref[...]
    if rope:
        cos_ref, sin_ref, o_ref = rest
        cos = cos_ref[...]
        sin = sin_ref[...]
        for c in range(tn // N_HD):
            ch = acc[:, c * N_HD:(c + 1) * N_HD]
            o_ref[:, c * N_HD:(c + 1) * N_HD] = (ch * cos + pltpu.roll(ch, N_HD // 2, 1) * sin).astype(o_ref.dtype)
    else:
        (o_ref,) = rest
        o_ref[...] = acc.astype(o_ref.dtype)


def _matmul(x, w, b, out_dtype, tn, rope=None, seq=None):
    m, k = x.shape
    n = w.shape[1]
    tm = MM_TM
    grid = (n // tn, m // tm)
    in_specs = [pl.BlockSpec((tm, k), lambda j, i: (i, 0)),
                pl.BlockSpec((k, tn), lambda j, i: (0, j)),
                pl.BlockSpec((1, tn), lambda j, i: (0, j))]
    args = [x, w, b.reshape(1, n)]
    if rope is not None:
        nt = seq // tm
        in_specs += [pl.BlockSpec((tm, N_HD), lambda j, i: (i % nt, 0))] * 2
        args += list(rope)
    return pl.pallas_call(
        functools.partial(_mm_body, rope=rope is not None, tn=tn),
        grid=grid, in_specs=in_specs,
        out_specs=pl.BlockSpec((tm, tn), lambda j, i: (i, j)),
        out_shape=jax.ShapeDtypeStruct((m, n), out_dtype),
        compiler_params=_params(("parallel", "parallel"), 48),
        name="proj_matmul",
    )(*args)


def _log_sigmoid(x):
    return jnp.minimum(x, 0.0) - jnp.log1p(jnp.exp(-jnp.abs(x)))


def _mlstm_body(q_ref, k_ref, v_ref, og_ref, g_ref, gt_ref, nw_ref, out_ref, c_ref, n_ref, m_ref, *, L):
    @pl.when(pl.program_id(1) == 0)
    def _():
        c_ref[...] = jnp.zeros_like(c_ref)
        n_ref[...] = jnp.zeros_like(n_ref)
        m_ref[...] = jnp.zeros_like(m_ref)

    g = g_ref[...]
    gt = gt_ref[...]
    row = lax.broadcasted_iota(jnp.int32, (L, L), 0)
    col = lax.broadcasted_iota(jnp.int32, (L, L), 1)
    causal = row >= col
    b_all = jnp.dot(causal.astype(F32), _log_sigmoid(g), precision=HIGHEST, preferred_element_type=F32)
    bt_all = jnp.dot(_log_sigmoid(gt), (row <= col).astype(F32), precision=HIGHEST, preferred_element_type=F32)
    for h in range(M_HEADS):
        b_col = b_all[:, M_HEADS + h:M_HEADS + h + 1]
        li_col = g[:, h:h + 1]
        b_row = bt_all[M_HEADS + h:M_HEADS + h + 1, :]
        li_row = gt[h:h + 1, :]
        m_prev = m_ref[h][:, 0:1]
        dmat = jnp.where(causal, b_col - b_row + li_row, -jnp.inf)
        inter = b_col + m_prev
        m_t = jnp.maximum(inter, jnp.max(dmat, axis=1, keepdims=True))
        w_inter = jnp.exp(inter - m_t)
        q = q_ref[:, h * M_DQK:(h + 1) * M_DQK]
        k = k_ref[:, h * M_DQK:(h + 1) * M_DQK]
        v = v_ref[:, h * M_DV:(h + 1) * M_DV]
        s = _nt(q, k) * jnp.exp(dmat - m_t)
        ct = c_ref[h]
        n_row = n_ref[h]
        num = jnp.dot(s.astype(BF16), v, preferred_element_type=F32) + w_inter * jnp.dot(
            q, ct.astype(BF16), preferred_element_type=F32)
        qn = jnp.sum(q.astype(F32) * n_row, axis=1, keepdims=True)
        den = jnp.sum(s, axis=1, keepdims=True) + w_inter * qn
        hh = num / jnp.maximum(jnp.abs(den), jnp.exp(-m_t))
        hn = _standardize(hh) * nw_ref[:, h * M_DV:(h + 1) * M_DV]
        og = og_ref[:, h * M_DV:(h + 1) * M_DV].astype(F32)
        out_ref[:, h * M_DV:(h + 1) * M_DV] = (hn * jax.nn.sigmoid(og)).astype(out_ref.dtype)
        b_last = b_col[L - 1:L, :]
        g_col = b_last - b_col + li_col
        g_row = b_last - b_row + li_row
        m_new = jnp.maximum(b_last + m_prev, jnp.max(g_row, axis=1, keepdims=True))
        decay = jnp.exp(b_last + m_prev - m_new)
        ws_col = jnp.exp(g_col - m_new)
        kf = k.astype(F32)
        vw = (v.astype(F32) * ws_col).astype(BF16)
        c_ref[h] = decay * ct + _tn(k, vw)
        n_ref[h] = decay * n_row + jnp.sum(kf * ws_col, axis=0, keepdims=True)
        m_ref[h] = jnp.broadcast_to(m_new, (1, LANES))


def _mlstm(p1, p3, gt, norm_w, B, T):
    L = MLSTM_CHUNK
    p1 = p1.reshape(B, T, P1_W)
    p3 = p3.reshape(B, T, P3_W)
    return pl.pallas_call(
        functools.partial(_mlstm_body, L=L),
        grid=(B, T // L),
        in_specs=[pl.BlockSpec((None, L, M_QK_W), lambda b, c: (b, c, P1_MQ // M_QK_W)),
                  pl.BlockSpec((None, L, M_QK_W), lambda b, c: (b, c, P1_MK // M_QK_W)),
                  pl.BlockSpec((None, L, M_V_W), lambda b, c: (b, c, P1_MV // M_V_W)),
                  pl.BlockSpec((None, L, M_V_W), lambda b, c: (b, c, P1_MO // M_V_W)),
                  pl.BlockSpec((None, L, LANES), lambda b, c: (b, c, 0)),
                  pl.BlockSpec((None, 8, L), lambda b, c: (b, 0, c)),
                  pl.BlockSpec((1, M_V_W), lambda b, c: (0, 0))],
        out_specs=pl.BlockSpec((None, L, M_V_W), lambda b, c: (b, c, 0)),
        out_shape=jax.ShapeDtypeStruct((B, T, M_V_W), BF16),
        scratch_shapes=[pltpu.VMEM((M_HEADS, M_DQK, M_DV), F32),
                        pltpu.VMEM((M_HEADS, 1, M_DQK), F32),
                        pltpu.VMEM((M_HEADS, 1, LANES), F32)],
        compiler_params=_params(("parallel", "arbitrary"), 48),
        name="mlstm",
    )(p1, p1, p1, p1, p3, gt, norm_w.reshape(1, M_V_W))


def _compress_body(x_ref, pos_ref, w1_ref, b1_ref, w2_ref, b2_ref, cos_ref, sin_ref, o_ref, *, rope, nb):
    x = x_ref[...]
    half = CMP_STRIDE * N_HD
    lo = jnp.dot((x + pos_ref[:, :half]).astype(BF16), w1_ref[:half, :], preferred_element_type=F32)
    hi = jnp.dot((x + pos_ref[:, half:]).astype(BF16), w1_ref[half:, :], preferred_element_type=F32)
    pre = lo + pltpu.roll(hi, nb - 1, 0) + b1_ref[...]
    h = jax.nn.gelu(pre)
    y = jnp.dot(h.astype(BF16), w2_ref[...], preferred_element_type=F32) + b2_ref[...]
    if rope:
        y = y * cos_ref[...] + pltpu.roll(y, N_HD // 2, 1) * sin_ref[...]
    keep = lax.broadcasted_iota(jnp.int32, (nb, N_HD), 0) < nb - 1
    o_ref[...] = jnp.where(keep, y, 0.0).astype(o_ref.dtype)


def _compress(x2, pos, w1, b1, w2, b2, cos, sin, rope):
    B, G, nb, _ = x2.shape
    full = lambda shape: pl.BlockSpec(shape, lambda b, g: (0,) * len(shape))
    return pl.pallas_call(
        functools.partial(_compress_body, rope=rope, nb=nb),
        grid=(B, G),
        in_specs=[pl.BlockSpec((None, None, nb, CMP_STRIDE * N_HD), lambda b, g: (b, g, 0, 0)),
                  full((1, CMP_BLOCK * N_HD)), full((CMP_BLOCK * N_HD, N_HD)), full((1, N_HD)),
                  full((N_HD, N_HD)), full((1, N_HD)), full((nb, N_HD)), full((nb, N_HD))],
        out_specs=pl.BlockSpec((None, None, nb, N_HD), lambda b, g: (b, g, 0, 0)),
        out_shape=jax.ShapeDtypeStruct((B, G, nb, N_HD), BF16),
        compiler_params=_params(("parallel", "parallel"), 32),
        name="nsa_compress",
    )(x2, pos.reshape(1, -1), w1.reshape(CMP_BLOCK * N_HD, N_HD).astype(BF16), b1.reshape(1, N_HD),
      w2.astype(BF16), b2.reshape(1, N_HD), cos, sin)


def _nsa_body(q_ref, ks_ref, vst_ref, kw_ref, vwt_ref, kc_ref, vct_ref, gt_ref, ovt_ref, o_ref,
              acc_ref, m_ref, l_ref, sel_ref, out_ref, s_ref, p_ref, *, tq, ncmp):
    qi = pl.program_id(2)
    t0 = qi * tq
    nsel = ovt_ref.shape[0]
    gates = jax.nn.sigmoid(gt_ref[...])
    q_heads = [q_ref[:, h * N_HD:(h + 1) * N_HD] for h in range(N_HPG)]

    kc = kc_ref[...]
    vct = vct_ref[...]
    n_idx = lax.broadcasted_iota(jnp.int32, (ncmp, tq), 0)
    t_cmp = t0 + lax.broadcasted_iota(jnp.int32, (ncmp, tq), 1)
    cmask = (n_idx * CMP_STRIDE + (CMP_BLOCK - 1) <= t_cmp) & (n_idx < ncmp - 1)
    psum = jnp.zeros((ncmp, tq), F32)
    for h in range(N_HPG):
        s_ref[0, h, :ncmp, :] = _nt(kc, q_heads[h])
    for h in range(N_HPG):
        s = jnp.where(cmask, s_ref[0, h, :ncmp, :], NEG)
        e = jnp.where(cmask, jnp.exp(s - jnp.max(s, axis=0, keepdims=True)), 0.0)
        p = e * (1.0 / jnp.maximum(jnp.sum(e, axis=0, keepdims=True), 1e-30))
        psum = psum + p
        p_ref[0, h, :ncmp, :] = p.astype(BF16)
    for h in range(N_HPG):
        out_ref[h] = gates[3 * h:3 * h + 1, :] * jnp.dot(vct, p_ref[0, h, :ncmp, :], preferred_element_type=F32)

    imp = jnp.dot(ovt_ref[...], psum, precision=HIGHEST, preferred_element_type=F32)
    j_idx = lax.broadcasted_iota(jnp.int32, (nsel, tq), 0)
    t_row = t0 + lax.broadcasted_iota(jnp.int32, (nsel, tq), 1)
    cur = lax.shift_right_logical(t_row, 6)
    forced = (j_idx == 0) | (j_idx == cur) | (j_idx == cur - 1)
    score = jnp.where(forced, jnp.inf, jnp.where(j_idx * SEL_BLOCK <= t_row, imp, -jnp.inf))

    def pick(_, carry):
        score, sel = carry
        mx = jnp.max(score, axis=0, keepdims=True)
        idx = jnp.min(jnp.where(score == mx, j_idx, nsel), axis=0, keepdims=True)
        hit = j_idx == idx
        return jnp.where(hit, -jnp.inf, score), jnp.where(hit, 1.0, sel)

    _, sel = lax.fori_loop(0, min(SEL_TOPK, nsel), pick, (score, jnp.zeros((nsel, tq), F32)), unroll=True)
    sel_ref[...] = sel

    def flash_init():
        acc_ref[...] = jnp.zeros_like(acc_ref)
        l_ref[...] = jnp.zeros_like(l_ref)
        m_ref[...] = jnp.full(m_ref.shape, NEG, F32)

    def flash_tiles(k_ref, vt_ref, tiles):
        vts = []
        for n, (j, _) in enumerate(tiles):
            start = pl.multiple_of(j * tq, tq)
            k = k_ref[pl.ds(start, tq), :]
            vts.append(vt_ref[:, pl.ds(start, tq)])
            for h in range(N_HPG):
                s_ref[n, h] = _nt(k, q_heads[h])
        for n, (_, mask) in enumerate(tiles):
            alphas = []
            for h in range(N_HPG):
                s = s_ref[n, h]
                if mask is not None:
                    s = jnp.where(mask, s, NEG)
                m_old = m_ref[h]
                m_new = jnp.maximum(m_old, jnp.max(s, axis=0, keepdims=True))
                p = jnp.exp(s - m_new)
                alpha = jnp.exp(m_old - m_new)
                l_ref[h] = alpha * l_ref[h] + jnp.sum(p, axis=0, keepdims=True)
                p_ref[n, h] = p.astype(BF16)
                m_ref[h] = m_new
                alphas.append(alpha)
            for h in range(N_HPG):
                acc_ref[h] = alphas[h] * acc_ref[h] + jnp.dot(vts[n], p_ref[n, h], preferred_element_type=F32)

    def flash_add(branch):
        for h in range(N_HPG):
            out_ref[h] = out_ref[h] + (gates[3 * h + branch:3 * h + branch + 1, :] / l_ref[h]) * acc_ref[h]

    kpos = lax.broadcasted_iota(jnp.int32, (tq, tq), 0)
    qpos = lax.broadcasted_iota(jnp.int32, (tq, tq), 1)
    blk_per_tile = tq // SEL_BLOCK

    def sel_mask(j):
        rows = [jnp.broadcast_to(sel_ref[pl.ds(j * blk_per_tile + b, 1), :], (SEL_BLOCK, tq))
                for b in range(blk_per_tile)]
        return jnp.concatenate(rows, axis=0) > 0.5

    flash_init()

    group = s_ref.shape[0]

    def sel_group(p, carry):
        flash_tiles(ks_ref, vst_ref, [(group * p + n, sel_mask(group * p + n)) for n in range(group)])
        return carry

    lax.fori_loop(0, qi // group, sel_group, 0)
    causal = kpos <= qpos
    for rem in range(group):
        @pl.when(qi % group == rem)
        def _():
            tiles = [(qi - rem + n, sel_mask(qi - rem + n)) for n in range(rem)]
            flash_tiles(ks_ref, vst_ref, tiles + [(qi, sel_mask(qi) & causal)])

    flash_add(1)

    flash_init()
    n_back = WINDOW // tq
    win_tiles = [(qi, causal)] + [(qi - back, None) for back in range(1, n_back)] + [(qi - n_back, kpos > qpos)]
    for n_tiles in range(1, n_back + 2):
        last = n_tiles == n_back + 1

        @pl.when((qi >= n_tiles - 1) if last else (qi == n_tiles - 1))
        def _():
            flash_tiles(kw_ref, vwt_ref, win_tiles[:n_tiles])
    flash_add(2)

    for h in range(N_HPG):
        o_ref[:, h * N_HD:(h + 1) * N_HD] = out_ref[h].T.astype(o_ref.dtype)


def _nsa(p1, p2, p3, k_cmp, v_cmp, ovt, B, T):
    tq = NSA_TQ
    ncmp = k_cmp.shape[2]
    p1 = p1.reshape(B, T, P1_W)
    p2 = p2.reshape(B, T, P2_W)
    p3 = p3.reshape(B, T, P3_W)
    hw = N_HPG * N_HD
    assert WINDOW % tq == 0 and tq % SEL_BLOCK == 0 and ncmp <= tq and NSA_GROUP >= WINDOW // tq + 1
    vst = p1[:, :, P1_NVS:P1_NVS + N_KV_W].transpose(0, 2, 1)
    vwt = p1[:, :, P1_NVW:P1_NVW + N_KV_W].transpose(0, 2, 1)
    vct = v_cmp.transpose(0, 1, 3, 2)
    gt = p3[:, :, P3_NG:P3_NG + N_KV * LANES].reshape(B, T, N_KV, LANES)[..., :16].transpose(0, 2, 3, 1)
    k_spec = lambda off: pl.BlockSpec((None, T, N_HD), lambda b, g, i: (b, 0, off // N_HD + g))
    vt_spec = pl.BlockSpec((None, N_HD, T), lambda b, g, i: (b, g, 0))
    return pl.pallas_call(
        functools.partial(_nsa_body, tq=tq, ncmp=ncmp),
        grid=(B, N_KV, T // tq),
        in_specs=[pl.BlockSpec((None, tq, hw), lambda b, g, i: (b, i, P2_NQ // hw + g)),
                  k_spec(P2_NKS), vt_spec, k_spec(P2_NKW), vt_spec,
                  pl.BlockSpec((None, None, ncmp, N_HD), lambda b, g, i: (b, g, 0, 0)),
                  pl.BlockSpec((None, None, N_HD, ncmp), lambda b, g, i: (b, g, 0, 0)),
                  pl.BlockSpec((None, None, 16, tq), lambda b, g, i: (b, g, 0, i)),
                  pl.BlockSpec(ovt.shape, lambda b, g, i: (0, 0))],
        out_specs=pl.BlockSpec((None, tq, hw), lambda b, g, i: (b, i, g)),
        out_shape=jax.ShapeDtypeStruct((B, T, N_Q_W), BF16),
        scratch_shapes=[pltpu.VMEM((N_HPG, N_HD, tq), F32), pltpu.VMEM((N_HPG, 1, tq), F32),
                        pltpu.VMEM((N_HPG, 1, tq), F32), pltpu.VMEM((ovt.shape[0], tq), F32),
                        pltpu.VMEM((N_HPG, N_HD, tq), F32),
                        pltpu.VMEM((NSA_GROUP, N_HPG, tq, tq), F32),
                        pltpu.VMEM((NSA_GROUP, N_HPG, tq, tq), BF16)],
        compiler_params=_params(("parallel", "parallel", "arbitrary"), 48),
        name="nsa_attention",
    )(p2, p2, vst, p2, vwt, k_cmp, vct, gt, ovt)


def _merge_body(hm_ref, hn_ref, wm_ref, wn_ref, gm_ref, gn_ref, o_ref):
    ym = jnp.dot(hm_ref[...], wm_ref[...], preferred_element_type=F32)
    yn = jnp.dot(hn_ref[...], wn_ref[...], preferred_element_type=F32)
    gm = jax.nn.sigmoid(gm_ref[...].astype(F32))
    gn = jax.nn.sigmoid(gn_ref[...].astype(F32))
    o_ref[...] = (gm * ym + gn * yn).astype(o_ref.dtype)


def _merge(hm, hn, wm, wn, p1):
    m = hm.shape[0]
    tm, tn = MM_TM, 512
    return pl.pallas_call(
        _merge_body,
        grid=(D_MODEL // tn, m // tm),
        in_specs=[pl.BlockSpec((tm, M_V_W), lambda j, i: (i, 0)),
                  pl.BlockSpec((tm, N_Q_W), lambda j, i: (i, 0)),
                  pl.BlockSpec((M_V_W, tn), lambda j, i: (0, j)),
                  pl.BlockSpec((N_Q_W, tn), lambda j, i: (0, j)),
                  pl.BlockSpec((tm, tn), lambda j, i: (i, P1_GM // tn + j)),
                  pl.BlockSpec((tm, tn), lambda j, i: (i, (P1_GM + D_MODEL) // tn + j))],
        out_specs=pl.BlockSpec((tm, tn), lambda j, i: (i, j)),
        out_shape=jax.ShapeDtypeStruct((m, D_MODEL), BF16),
        compiler_params=_params(("parallel", "parallel"), 48),
        name="branch_merge",
    )(hm, hn, wm, wn, p1, p1)


def _out_ln_body(y_ref, w_ref, x_ref, g_ref, b_ref, o_ref, ob_ref):
    mix = jnp.dot(y_ref[...], w_ref[...], preferred_element_type=F32)
    out = _standardize(ALPHA * x_ref[...] + mix) * g_ref[...] + b_ref[...]
    o_ref[...] = out
    ob_ref[...] = out.astype(BF16)


def _out_ln(y, w, x, g, b):
    m = y.shape[0]
    tm = MM_TM
    row = pl.BlockSpec((tm, D_MODEL), lambda i: (i, 0))
    vec = pl.BlockSpec((1, D_MODEL), lambda i: (0, 0))
    return pl.pallas_call(
        _out_ln_body,
        grid=(m // tm,),
        in_specs=[row, pl.BlockSpec((D_MODEL, D_MODEL), lambda i: (0, 0)), row, vec, vec],
        out_specs=[row, row],
        out_shape=[jax.ShapeDtypeStruct((m, D_MODEL), F32), jax.ShapeDtypeStruct((m, D_MODEL), BF16)],
        compiler_params=_params(("parallel",), 48),
        name="out_proj_layernorm",
    )(y, w, x, g.reshape(1, -1), b.reshape(1, -1))


def _first_of4(vals, target):
    return jnp.where(vals[0] == target, 0.0, jnp.where(vals[1] == target, 1.0, jnp.where(vals[2] == target, 2.0, 3.0)))


def _select4(idx, vals):
    return jnp.where(idx == 0.0, vals[0], jnp.where(idx == 1.0, vals[1], jnp.where(idx == 2.0, vals[2], vals[3])))


def _router_body(x_ref, rw_ref, rb_ref, o_ref, cnt_ref, carry_ref, *, tm):
    @pl.when(pl.program_id(0) == 0)
    def _():
        carry_ref[...] = jnp.zeros_like(carry_ref)

    logits = _nt(rw_ref[...], x_ref[...], precision=HIGHEST)
    aff = jax.nn.sigmoid(logits)
    biased = aff + rb_ref[:, 0:1]
    a = [biased[i * N_GROUPS:(i + 1) * N_GROUPS, :] for i in range(EXP_PER_GROUP)]
    af = [aff[i * N_GROUPS:(i + 1) * N_GROUPS, :] for i in range(EXP_PER_GROUP)]
    m1 = jnp.maximum(jnp.maximum(a[0], a[1]), jnp.maximum(a[2], a[3]))
    i1 = _first_of4(a, m1)
    rest = [jnp.where(i1 == float(i), -jnp.inf, a[i]) for i in range(EXP_PER_GROUP)]
    m2 = jnp.maximum(jnp.maximum(rest[0], rest[1]), jnp.maximum(rest[2], rest[3]))
    i2 = _first_of4(rest, m2)
    gscore = m1 + m2
    g_iota = lax.broadcasted_iota(jnp.int32, (N_GROUPS, tm), 0).astype(F32)
    g_idx = jnp.min(jnp.where(gscore == jnp.max(gscore, axis=0, keepdims=True), g_iota, float(N_GROUPS)),
                    axis=0, keepdims=True)
    in_g = g_iota == g_idx
    take = lambda v: jnp.sum(jnp.where(in_g, v, 0.0), axis=0, keepdims=True)
    s0 = take(i1)
    s1 = take(i2)
    w0 = take(_select4(i1, af))
    w1 = take(_select4(i2, af))
    wsum = w0 + w1
    r0 = s0 * N_GROUPS + g_idx
    r1 = s1 * N_GROUPS + g_idx
    r_iota = lax.broadcasted_iota(jnp.int32, (N_EXPERTS, tm), 0).astype(F32)
    member = (r_iota == r0) | (r_iota == r1)
    tt = lax.broadcasted_iota(jnp.int32, (tm, tm), 0)
    tc = lax.broadcasted_iota(jnp.int32, (tm, tm), 1)
    before = jnp.dot(member.astype(BF16), (tt < tc).astype(BF16), preferred_element_type=F32)
    base = before + carry_ref[:, 0:1]
    rank0 = jnp.sum(jnp.where(r_iota == r0, base, 0.0), axis=0, keepdims=True)
    rank1 = jnp.sum(jnp.where(r_iota == r1, base, 0.0), axis=0, keepdims=True)
    new_carry = carry_ref[:, 0:1] + jnp.sum(member.astype(F32), axis=1, keepdims=True)
    carry_ref[...] = jnp.broadcast_to(new_carry, carry_ref.shape)
    cnt_ref[...] = jnp.broadcast_to(new_carry, cnt_ref.shape)
    o_ref[0:1, :] = g_idx * EXP_PER_GROUP + s0
    o_ref[1:2, :] = g_idx * EXP_PER_GROUP + s1
    o_ref[2:3, :] = w0 / wsum
    o_ref[3:4, :] = w1 / wsum
    o_ref[4:5, :] = rank0
    o_ref[5:6, :] = rank1
    o_ref[6:8, :] = jnp.zeros((2, tm), F32)


def _router(x, rw_t, rb):
    m = x.shape[0]
    tm = ROUTER_TM
    return pl.pallas_call(
        functools.partial(_router_body, tm=tm),
        grid=(m // tm,),
        in_specs=[pl.BlockSpec((tm, D_MODEL), lambda i: (i, 0)),
                  pl.BlockSpec((N_EXPERTS, D_MODEL), lambda i: (0, 0)),
                  pl.BlockSpec((N_EXPERTS, LANES), lambda i: (0, 0))],
        out_specs=[pl.BlockSpec((8, tm), lambda i: (0, i)),
                   pl.BlockSpec((N_EXPERTS, LANES), lambda i: (0, 0))],
        out_shape=[jax.ShapeDtypeStruct((8, m), F32), jax.ShapeDtypeStruct((N_EXPERTS, LANES), F32)],
        scratch_shapes=[pltpu.VMEM((N_EXPERTS, LANES), F32)],
        compiler_params=_params(("arbitrary",), 48),
        name="moe_router",
    )(x, rw_t, rb)


def _row_copy(src_ref, src_row, dst_ref, dst_row, sem):
    return pltpu.make_async_copy(src_ref.at[pl.ds(src_row, 1), :], dst_ref.at[pl.ds(dst_row, 1), :], sem)


def _dispatch_body(dest_ref, x_ref, init_ref, xs_ref, sem, *, td):
    del init_ref
    i = pl.program_id(0)
    base = i * td

    def issue(r, c):
        _row_copy(x_ref, base + r, xs_ref, dest_ref[0, 0, r], sem).start(priority=0)
        _row_copy(x_ref, base + r, xs_ref, dest_ref[0, 0, td + r], sem).start(priority=1)
        return c

    lax.fori_loop(0, td, issue, 0)

    def wait_step():
        for _ in range(2):
            pltpu.make_async_copy(x_ref.at[pl.ds(0, td), :], xs_ref.at[pl.ds(0, td), :], sem).wait()

    @pl.when(i > 0)
    def _():
        wait_step()

    @pl.when(i == pl.num_programs(0) - 1)
    def _():
        wait_step()


def _dispatch(x, dest, rows):
    m = x.shape[0]
    td = MOE_TD
    return pl.pallas_call(
        functools.partial(_dispatch_body, td=td),
        grid=(m // td,),
        in_specs=[pl.BlockSpec((1, 1, 2 * td), lambda i: (i, 0, 0), memory_space=pltpu.SMEM),
                  pl.BlockSpec(memory_space=pl.ANY),
                  pl.BlockSpec(memory_space=pl.ANY)],
        out_specs=pl.BlockSpec(memory_space=pl.ANY),
        out_shape=jax.ShapeDtypeStruct((rows, D_MODEL), F32),
        scratch_shapes=[pltpu.SemaphoreType.DMA(())],
        input_output_aliases={2: 0},
        compiler_params=_params(("arbitrary",), 32),
        name="moe_dispatch",
    )(dest, x, jnp.zeros((rows, D_MODEL), F32))


def _expert_changed(te_ref, i):
    return (i == 0) | (te_ref[i] != te_ref[jnp.maximum(i - 1, 0)])


def _experts_up_body(te_ref, nused_ref, x_ref, w1_ref, w3_ref, h_ref, w1b_ref, w3b_ref):
    i = pl.program_id(0)
    used = i < nused_ref[0]

    @pl.when(used & _expert_changed(te_ref, i))
    def _():
        w1b_ref[...] = w1_ref[...].astype(BF16)
        w3b_ref[...] = w3_ref[...].astype(BF16)

    @pl.when(used)
    def _():
        xb = x_ref[...].astype(BF16)
        a = jnp.dot(xb, w1b_ref[...], preferred_element_type=F32)
        b = jnp.dot(xb, w3b_ref[...], preferred_element_type=F32)
        h_ref[...] = (a * jax.nn.sigmoid(a) * b).astype(BF16)

    @pl.when(jnp.logical_not(used))
    def _():
        h_ref[...] = jnp.zeros_like(h_ref)


def _experts_down_body(te_ref, nused_ref, h_ref, w2_ref, o_ref, w2b_ref):
    i = pl.program_id(0)
    used = i < nused_ref[0]

    @pl.when(used & _expert_changed(te_ref, i))
    def _():
        w2b_ref[...] = w2_ref[...].astype(BF16)

    @pl.when(used)
    def _():
        o_ref[...] = jnp.dot(h_ref[...], w2b_ref[...], preferred_element_type=F32)

    @pl.when(jnp.logical_not(used))
    def _():
        o_ref[...] = jnp.zeros_like(o_ref)


def _experts(xs, w1, w3, w2, layer, tile_expert, n_used):
    rows = xs.shape[0]
    tm = MOE_TM
    by_expert = lambda r, c: pl.BlockSpec((None, None, r, c), lambda i, te, nu: (layer, te[i], 0, 0))
    by_tile = lambda c: pl.BlockSpec((tm, c), lambda i, te, nu: (i, 0))
    h = pl.pallas_call(
        _experts_up_body,
        grid_spec=pltpu.PrefetchScalarGridSpec(
            num_scalar_prefetch=2, grid=(rows // tm,),
            in_specs=[by_tile(D_MODEL), by_expert(D_MODEL, D_FF), by_expert(D_MODEL, D_FF)],
            out_specs=by_tile(D_FF),
            scratch_shapes=[pltpu.VMEM((D_MODEL, D_FF), BF16), pltpu.VMEM((D_MODEL, D_FF), BF16)]),
        out_shape=jax.ShapeDtypeStruct((rows, D_FF), BF16),
        compiler_params=_params(("arbitrary",), 56),
        name="moe_experts_up",
    )(tile_expert, n_used, xs, w1, w3)
    return pl.pallas_call(
        _experts_down_body,
        grid_spec=pltpu.PrefetchScalarGridSpec(
            num_scalar_prefetch=2, grid=(rows // tm,),
            in_specs=[by_tile(D_FF), by_expert(D_FF, D_MODEL)],
            out_specs=by_tile(D_MODEL),
            scratch_shapes=[pltpu.VMEM((D_FF, D_MODEL), BF16)]),
        out_shape=jax.ShapeDtypeStruct((rows, D_MODEL), F32),
        compiler_params=_params(("arbitrary",), 40),
        name="moe_experts_down",
    )(tile_expert, n_used, h, w2)


def _combine_body(dest_ref, next_ref, ys_ref, x_ref, w_ref, g_ref, b_ref, o_ref, ob_ref, buf_ref, sem, *, td):
    i = pl.program_id(0)
    slot = i % 2

    def gather(idx_ref, s):
        def issue(r, c):
            _row_copy(ys_ref, idx_ref[0, 0, r], buf_ref.at[s, 0], r, sem.at[s]).start(priority=0)
            _row_copy(ys_ref, idx_ref[0, 0, td + r], buf_ref.at[s, 1], r, sem.at[s]).start(priority=1)
            return c

        lax.fori_loop(0, td, issue, 0)

    @pl.when(i == 0)
    def _():
        gather(dest_ref, 0)

    @pl.when(i + 1 < pl.num_programs(0))
    def _():
        gather(next_ref, 1 - slot)

    for half in range(2):
        pltpu.make_async_copy(ys_ref.at[pl.ds(0, td), :], buf_ref.at[slot, half], sem.at[slot]).wait()
    w = w_ref[...]
    ff = w[:, 0:1] * buf_ref[slot, 0] + w[:, 1:2] * buf_ref[slot, 1]
    out = _standardize(ALPHA * x_ref[...] + ff) * g_ref[...] + b_ref[...]
    o_ref[...] = out
    ob_ref[...] = out.astype(BF16)


def _combine(ys, dest, x, w, g, b):
    m = x.shape[0]
    td = MOE_TD
    row = pl.BlockSpec((td, D_MODEL), lambda i: (i, 0))
    vec = pl.BlockSpec((1, D_MODEL), lambda i: (0, 0))
    last = m // td - 1
    return pl.pallas_call(
        functools.partial(_combine_body, td=td),
        grid=(m // td,),
        in_specs=[pl.BlockSpec((1, 1, 2 * td), lambda i: (i, 0, 0), memory_space=pltpu.SMEM),
                  pl.BlockSpec((1, 1, 2 * td), lambda i: (jnp.minimum(i + 1, last), 0, 0), memory_space=pltpu.SMEM),
                  pl.BlockSpec(memory_space=pl.ANY), row,
                  pl.BlockSpec((td, 8), lambda i: (i, 0)), vec, vec],
        out_specs=[row, row],
        out_shape=[jax.ShapeDtypeStruct((m, D_MODEL), F32), jax.ShapeDtypeStruct((m, D_MODEL), BF16)],
        scratch_shapes=[pltpu.VMEM((2, 2, td, D_MODEL), F32), pltpu.SemaphoreType.DMA((2,))],
        compiler_params=_params(("arbitrary",), 40),
        name="moe_combine",
    )(dest, dest, ys, x, w, g.reshape(1, -1), b.reshape(1, -1))


def _moe(x, xb, rw_t, rb, w1, w3, w2, layer, ln_g, ln_b):
    del xb
    m = x.shape[0]
    tm, td = MOE_TM, MOE_TD
    rows = 2 * m + N_EXPERTS * tm
    ro, cnt = _router(x, rw_t, rb)
    e0 = ro[0].astype(jnp.int32)
    e1 = ro[1].astype(jnp.int32)
    counts = cnt[:, 0].astype(jnp.int32).reshape(EXP_PER_GROUP, N_GROUPS).T.reshape(N_EXPERTS)
    padded = (counts + tm - 1) // tm * tm
    ends = jnp.cumsum(padded)
    offs = ends - padded
    d0 = offs[e0] + ro[4].astype(jnp.int32)
    d1 = offs[e1] + ro[5].astype(jnp.int32)
    dest = jnp.concatenate([d0.reshape(m // td, 1, td), d1.reshape(m // td, 1, td)], axis=-1)
    tile_start = jnp.arange(rows // tm, dtype=jnp.int32) * tm
    tile_expert = jnp.minimum(jnp.sum(tile_start[:, None] >= ends[None, :], axis=1), N_EXPERTS - 1).astype(jnp.int32)
    n_used = (ends[-1:] // tm).astype(jnp.int32)
    xs = _dispatch(x, dest, rows)
    ys = _experts(xs, w1, w3, w2, layer, tile_expert, n_used)
    wcol = jnp.pad(ro[2:4].T, ((0, 0), (0, 6)))
    return _combine(ys, dest, x, wcol, ln_g, ln_b)


def _rope_tables(pos):
    half = N_HD // 2
    inv = ROPE_THETA ** (-jnp.arange(half, dtype=F32) / half)
    ang = pos.astype(F32)[:, None] * inv[None, :]
    cos, sin = jnp.cos(ang), jnp.sin(ang)
    return jnp.concatenate([cos, cos], -1), jnp.concatenate([-sin, sin], -1)


def _overlap_t(T, ncmp_pad):
    n_cmp = (T - CMP_BLOCK) // CMP_STRIDE + 1
    n_sel = T // SEL_BLOCK
    cs = np.arange(n_cmp) * CMP_STRIDE
    ss = np.arange(n_sel) * SEL_BLOCK
    ov = ((cs[:, None] <= ss[None, :] + SEL_BLOCK - 1) & (cs[:, None] + CMP_BLOCK - 1 >= ss[None, :])).astype(np.float32)
    out = np.zeros((n_sel, ncmp_pad), np.float32)
    out[:, :n_cmp] = ov.T
    return jnp.asarray(out)


def _pack_body(w_ref, o1_ref, o2_ref, o3_ref):
    (mq, mk, mv, mo, mi, mf, nq, nkc, nvc, nks, nvs, nkw, nvw, ng, gm, end) = IN_OFFS

    def put(o_ref, off, a, b, scale=1.0):
        o_ref[:, off:off + b - a] = (w_ref[:, a:b] * scale).astype(o_ref.dtype)

    put(o1_ref, P1_MQ, mq, mk, M_DQK ** -0.5)
    put(o1_ref, P1_MK, mk, mo)
    put(o1_ref, P1_MO, mo, mi)
    put(o1_ref, P1_NVS, nvs, nkw)
    put(o1_ref, P1_NVW, nvw, ng)
    put(o1_ref, P1_GM, gm, end)
    put(o2_ref, P2_NQ, nq, nkc, N_HD ** -0.5)
    put(o2_ref, P2_NKS, nks, nvs)
    put(o2_ref, P2_NKW, nkw, nvw)
    o3_ref[:, 0:P3_NKC] = jnp.zeros((o3_ref.shape[0], P3_NKC), o3_ref.dtype)
    put(o3_ref, P3_GATES, mi, nq)
    for g in range(N_KV):
        put(o3_ref, P3_NG + g * LANES, ng + g * 3 * N_HPG, ng + (g + 1) * 3 * N_HPG)
    put(o3_ref, P3_NKC, nkc, nks)


def _pack(w, layer, out_dtype, tr):
    rows = w.shape[1]
    out = lambda n: pl.BlockSpec((tr, n), lambda i: (i, 0))
    return pl.pallas_call(
        _pack_body,
        grid=(rows // tr,),
        in_specs=[pl.BlockSpec((None, tr, IN_OFFS[-1]), lambda i: (layer, i, 0))],
        out_specs=[out(P1_W), out(P2_W), out(P3_W)],
        out_shape=[jax.ShapeDtypeStruct((rows, n), out_dtype) for n in (P1_W, P2_W, P3_W)],
        compiler_params=_params(("parallel",), 56),
        name="pack_in_proj",
    )(w)


def kernel(x, w_in, b_in, m_norm_w, cmp_pos_k, cmp_w1_k, cmp_b1_k, cmp_w2_k, cmp_b2_k, cmp_pos_v, cmp_w1_v,
           cmp_b1_v, cmp_w2_v, cmp_b2_v, w_branch_m, w_branch_n, w_out, ln1_g, ln1_b, router_w, router_b,
           exp_w1, exp_w3, exp_w2, ln2_g, ln2_b):
    B, T, D = x.shape
    m = B * T
    nb = T // CMP_STRIDE
    cos_t, sin_t = _rope_tables(jnp.arange(T))
    cos_c, sin_c = _rope_tables(jnp.arange(nb) * CMP_STRIDE + CMP_BLOCK - 1)
    ovt = _overlap_t(T, nb)
    perm = (np.arange(N_GROUPS)[None, :] * EXP_PER_GROUP + np.arange(EXP_PER_GROUP)[:, None]).reshape(-1)
    rw_t = router_w.T[perm]
    rb = jnp.broadcast_to(router_b[perm][:, None], (N_EXPERTS, LANES))

    b_rows = jnp.pad(b_in[:, None, :], ((0, 0), (0, 7), (0, 0)))
    xf = x.reshape(m, D)
    xb = xf.astype(BF16)
    for l in range(DEPTH):
        w1p, w2p, w3p = _pack(w_in, l, BF16, 256)
        b1p, b2p, b3p = (b[0] for b in _pack(b_rows, l, F32, 8))
        p1 = _matmul(xb, w1p, b1p, BF16, MM_TN)
        p2 = _matmul(xb, w2p, b2p, BF16, MM_TN, rope=(cos_t, sin_t), seq=T)
        p3 = _matmul(xb, w3p, b3p, F32, P3_W)
        gt = p3.reshape(B, T, P3_W)[:, :, :8].transpose(0, 2, 1)
        h_m = _mlstm(p1, p3, gt, m_norm_w[l], B, T).reshape(m, M_V_W)

        def blocks16(off):
            a = p3.reshape(B, nb, CMP_STRIDE, P3_W)[..., off:off + N_KV_W]
            return a.reshape(B, nb, CMP_STRIDE, N_KV, N_HD).transpose(0, 3, 1, 2, 4).reshape(B, N_KV, nb, CMP_STRIDE * N_HD)

        k_cmp = _compress(blocks16(P3_NKC), cmp_pos_k[l], cmp_w1_k[l], cmp_b1_k[l], cmp_w2_k[l], cmp_b2_k[l],
                          cos_c, sin_c, True)
        v_cmp = _compress(blocks16(P3_NVC), cmp_pos_v[l], cmp_w1_v[l], cmp_b1_v[l], cmp_w2_v[l], cmp_b2_v[l],
                          cos_c, sin_c, False)
        h_n = _nsa(p1, p2, p3, k_cmp, v_cmp, ovt, B, T).reshape(m, N_Q_W)
        merged = _merge(h_m, h_n, w_branch_m[l].astype(BF16), w_branch_n[l].astype(BF16), p1)
        xf, xb = _out_ln(merged, w_out[l].astype(BF16), xf, ln1_g[l], ln1_b[l])
        xf, xb = _moe(xf, xb, rw_t, rb, exp_w1, exp_w3, exp_w2, l, ln2_g[l], ln2_b[l])
    return xf.reshape(B, T, D)
```

```python
import functools

import numpy as np
import jax
import jax.numpy as jnp
from jax import lax
from jax.experimental import pallas as pl
from jax.experimental.pallas import tpu as pltpu

F32 = jnp.float32
BF16 = jnp.bfloat16
HIGHEST = lax.Precision.HIGHEST

D_MODEL = 2048
DEPTH = 2
M_HEADS = 4
M_DQK = 256
M_DV = 512
N_KV = 4
N_HPG = 4
N_HEADS = N_KV * N_HPG
N_HD = 128
CMP_BLOCK = 32
CMP_STRIDE = 16
SEL_BLOCK = 64
SEL_TOPK = 16
WINDOW = 512
ROPE_THETA = 10000.0
N_EXPERTS = 32
N_GROUPS = 8
EXP_PER_GROUP = N_EXPERTS // N_GROUPS
D_FF = 1024
ALPHA = (2 * DEPTH) ** 0.25
EPS = 1e-5
NEG = -1e30

M_QK_W = M_HEADS * M_DQK
M_V_W = M_HEADS * M_DV
N_Q_W = N_HEADS * N_HD
N_KV_W = N_KV * N_HD
IN_SIZES = (M_QK_W, M_QK_W, M_V_W, M_V_W, M_HEADS, M_HEADS, N_Q_W, N_KV_W, N_KV_W, N_KV_W, N_KV_W,
            N_KV_W, N_KV_W, 3 * N_HEADS, 2 * D_MODEL)
IN_OFFS = tuple(int(v) for v in np.cumsum((0,) + IN_SIZES))

LANES = 128
MLSTM_CHUNK = 256
NSA_TQ = 256
NSA_GROUP = 4
NSA_ONES = 16
MM_TM = 512
MM_TN = 1024
MOE_TM = 256
MOE_TD = 256
ROUTER_TM = 512

P1_MQ, P1_MK, P1_MV, P1_MO, P1_NVS, P1_NVW, P1_GM = 0, 1024, 2048, 4096, 6144, 6656, 7168
P1_W = 7168 + 2 * D_MODEL
P2_NQ, P2_NKS, P2_NKW = 0, 2048, 2560
P2_W = 3072
P3_GATES, P3_NG, P3_NKC, P3_NVC = 0, 128, 640, 1152
P3_W = 1664


def _params(sem, vmem_mb):
    return pltpu.CompilerParams(dimension_semantics=sem, vmem_limit_bytes=vmem_mb * 1024 * 1024)


def _nt(a, b, **kw):
    return lax.dot_general(a, b, (((1,), (1,)), ((), ())), preferred_element_type=F32, **kw)


def _tn(a, b):
    return lax.dot_general(a, b, (((0,), (0,)), ((), ())), preferred_element_type=F32)


def _standardize(x):
    mu = jnp.mean(x, -1, keepdims=True)
    xc = x - mu
    var = jnp.mean(xc * xc, -1, keepdims=True)
    return xc * lax.rsqrt(var + EPS)


def _mm_body(x_ref, w_ref, b_ref, *rest, rope, tn):
    acc = jnp.dot(x_ref[...], w_ref[...], preferred_element_type=F32) + b_ref[...]
    if rope:
        cos_ref, sin_ref, o_ref = rest
        cos = cos_ref[...]
        sin = sin_ref[...]
        for c in range(tn // N_HD):
            ch = acc[:, c * N_HD:(c + 1) * N_HD]
            o_ref[:, c * N_HD:(c + 1) * N_HD] = (ch * cos + pltpu.roll(ch, N_HD // 2, 1) * sin).astype(o_ref.dtype)
    else:
        (o_ref,) = rest
        o_ref[...] = acc.astype(o_ref.dtype)


def _matmul(x, w, b, out_dtype, tn, rope=None, seq=None):
    m, k = x.shape
    n = w.shape[1]
    tm = MM_TM
    grid = (n // tn, m // tm)
    in_specs = [pl.BlockSpec((tm, k), lambda j, i: (i, 0)),
                pl.BlockSpec((k, tn), lambda j, i: (0, j)),
                pl.BlockSpec((1, tn), lambda j, i: (0, j))]
    args = [x, w, b.reshape(1, n)]
    if rope is not None:
        nt = seq // tm
        in_specs += [pl.BlockSpec((tm, N_HD), lambda j, i: (i % nt, 0))] * 2
        args += list(rope)
    return pl.pallas_call(
        functools.partial(_mm_body, rope=rope is not None, tn=tn),
        grid=grid, in_specs=in_specs,
        out_specs=pl.BlockSpec((tm, tn), lambda j, i: (i, j)),
        out_shape=jax.ShapeDtypeStruct((m, n), out_dtype),
        compiler_params=_params(("parallel", "parallel"), 48),
        name="proj_matmul",
    )(*args)


def _log_sigmoid(x):
    return jnp.minimum(x, 0.0) - jnp.log1p(jnp.exp(-jnp.abs(x)))


def _mlstm_body(q_ref, k_ref, v_ref, og_ref, g_ref, gt_ref, nw_ref, out_ref, c_ref, n_ref, m_ref, *, L):
    @pl.when(pl.program_id(1) == 0)
    def _():
        c_ref[...] = jnp.zeros_like(c_ref)
        n_ref[...] = jnp.zeros_like(n_ref)
        m_ref[...] = jnp.zeros_like(m_ref)

    g = g_ref[...]
    gt = gt_ref[...]
    row = lax.broadcasted_iota(jnp.int32, (L, L), 0)
    col = lax.broadcasted_iota(jnp.int32, (L, L), 1)
    causal = row >= col
    b_all = jnp.dot(causal.astype(F32), _log_sigmoid(g), precision=HIGHEST, preferred_element_type=F32)
    bt_all = jnp.dot(_log_sigmoid(gt), (row <= col).astype(F32), precision=HIGHEST, preferred_element_type=F32)
    for h in range(M_HEADS):
        b_col = b_all[:, M_HEADS + h:M_HEADS + h + 1]
        li_col = g[:, h:h + 1]
        b_row = bt_all[M_HEADS + h:M_HEADS + h + 1, :]
        li_row = gt[h:h + 1, :]
        m_prev = m_ref[h][:, 0:1]
        dmat = jnp.where(causal, b_col - b_row + li_row, -jnp.inf)
        inter = b_col + m_prev
        m_t = jnp.maximum(inter, jnp.max(dmat, axis=1, keepdims=True))
        w_inter = jnp.exp(inter - m_t)
        q = q_ref[:, h * M_DQK:(h + 1) * M_DQK]
        k = k_ref[:, h * M_DQK:(h + 1) * M_DQK]
        v = v_ref[:, h * M_DV:(h + 1) * M_DV]
        s = _nt(q, k) * jnp.exp(dmat - m_t)
        ct = c_ref[h]
        n_row = n_ref[h]
        num = jnp.dot(s.astype(BF16), v, preferred_element_type=F32) + w_inter * jnp.dot(
            q, ct.astype(BF16), preferred_element_type=F32)
        qn = jnp.sum(q.astype(F32) * n_row, axis=1, keepdims=True)
        den = jnp.sum(s, axis=1, keepdims=True) + w_inter * qn
        hh = num / jnp.maximum(jnp.abs(den), jnp.exp(-m_t))
        hn = _standardize(hh) * nw_ref[:, h * M_DV:(h + 1) * M_DV]
        og = og_ref[:, h * M_DV:(h + 1) * M_DV].astype(F32)
        out_ref[:, h * M_DV:(h + 1) * M_DV] = (hn * jax.nn.sigmoid(og)).astype(out_ref.dtype)
        b_last = b_col[L - 1:L, :]
        g_col = b_last - b_col + li_col
        g_row = b_last - b_row + li_row
        m_new = jnp.maximum(b_last + m_prev, jnp.max(g_row, axis=1, keepdims=True))
        decay = jnp.exp(b_last + m_prev - m_new)
        ws_col = jnp.exp(g_col - m_new)
        kf = k.astype(F32)
        vw = (v.astype(F32) * ws_col).astype(BF16)
        c_ref[h] = decay * ct + _tn(k, vw)
        n_ref[h] = decay * n_row + jnp.sum(kf * ws_col, axis=0, keepdims=True)
        m_ref[h] = jnp.broadcast_to(m_new, (1, LANES))


def _mlstm(p1, p3, gt, norm_w, B, T):
    L = MLSTM_CHUNK
    p1 = p1.reshape(B, T, P1_W)
    p3 = p3.reshape(B, T, P3_W)
    return pl.pallas_call(
        functools.partial(_mlstm_body, L=L),
        grid=(B, T // L),
        in_specs=[pl.BlockSpec((None, L, M_QK_W), lambda b, c: (b, c, P1_MQ // M_QK_W)),
                  pl.BlockSpec((None, L, M_QK_W), lambda b, c: (b, c, P1_MK // M_QK_W)),
                  pl.BlockSpec((None, L, M_V_W), lambda b, c: (b, c, P1_MV // M_V_W)),
                  pl.BlockSpec((None, L, M_V_W), lambda b, c: (b, c, P1_MO // M_V_W)),
                  pl.BlockSpec((None, L, LANES), lambda b, c: (b, c, 0)),
                  pl.BlockSpec((None, 8, L), lambda b, c: (b, 0, c)),
                  pl.BlockSpec((1, M_V_W), lambda b, c: (0, 0))],
        out_specs=pl.BlockSpec((None, L, M_V_W), lambda b, c: (b, c, 0)),
        out_shape=jax.ShapeDtypeStruct((B, T, M_V_W), BF16),
        scratch_shapes=[pltpu.VMEM((M_HEADS, M_DQK, M_DV), F32),
                        pltpu.VMEM((M_HEADS, 1, M_DQK), F32),
                        pltpu.VMEM((M_HEADS, 1, LANES), F32)],
        compiler_params=_params(("parallel", "arbitrary"), 48),
        name="mlstm",
    )(p1, p1, p1, p1, p3, gt, norm_w.reshape(1, M_V_W))


def _compress_body(x_ref, pos_ref, w1_ref, b1_ref, w2_ref, b2_ref, cos_ref, sin_ref, o_ref, *, rope, nb):
    x = x_ref[...]
    half = CMP_STRIDE * N_HD
    lo = jnp.dot((x + pos_ref[:, :half]).astype(BF16), w1_ref[:half, :], preferred_element_type=F32)
    hi = jnp.dot((x + pos_ref[:, half:]).astype(BF16), w1_ref[half:, :], preferred_element_type=F32)
    pre = lo + pltpu.roll(hi, nb - 1, 0) + b1_ref[...]
    h = jax.nn.gelu(pre)
    y = jnp.dot(h.astype(BF16), w2_ref[...], preferred_element_type=F32) + b2_ref[...]
    if rope:
        y = y * cos_ref[...] + pltpu.roll(y, N_HD // 2, 1) * sin_ref[...]
    keep = lax.broadcasted_iota(jnp.int32, (nb, N_HD), 0) < nb - 1
    o_ref[...] = jnp.where(keep, y, 0.0).astype(o_ref.dtype)


def _compress(x2, pos, w1, b1, w2, b2, cos, sin, rope):
    B, G, nb, _ = x2.shape
    full = lambda shape: pl.BlockSpec(shape, lambda b, g: (0,) * len(shape))
    return pl.pallas_call(
        functools.partial(_compress_body, rope=rope, nb=nb),
        grid=(B, G),
        in_specs=[pl.BlockSpec((None, None, nb, CMP_STRIDE * N_HD), lambda b, g: (b, g, 0, 0)),
                  full((1, CMP_BLOCK * N_HD)), full((CMP_BLOCK * N_HD, N_HD)), full((1, N_HD)),
                  full((N_HD, N_HD)), full((1, N_HD)), full((nb, N_HD)), full((nb, N_HD))],
        out_specs=pl.BlockSpec((None, None, nb, N_HD), lambda b, g: (b, g, 0, 0)),
        out_shape=jax.ShapeDtypeStruct((B, G, nb, N_HD), BF16),
        compiler_params=_params(("parallel", "parallel"), 32),
        name="nsa_compress",
    )(x2, pos.reshape(1, -1), w1.reshape(CMP_BLOCK * N_HD, N_HD).astype(BF16), b1.reshape(1, N_HD),
      w2.astype(BF16), b2.reshape(1, N_HD), cos, sin)


def _nsa_body(q_ref, ks_ref, vst_ref, kw_ref, vwt_ref, kc_ref, vct_ref, gt_ref, ovt_ref, o_ref,
              acc_ref, m_ref, sel_ref, out_ref, s_ref, p_ref, *, tq, ncmp):
    qi = pl.program_id(2)
    t0 = qi * tq
    nsel = ovt_ref.shape[0]
    gates = jax.nn.sigmoid(gt_ref[...])
    q_heads = [q_ref[:, h * N_HD:(h + 1) * N_HD] for h in range(N_HPG)]

    kc = kc_ref[...]
    vct = vct_ref[...]
    n_idx = lax.broadcasted_iota(jnp.int32, (ncmp, tq), 0)
    t_cmp = t0 + lax.broadcasted_iota(jnp.int32, (ncmp, tq), 1)
    cmask = (n_idx * CMP_STRIDE + (CMP_BLOCK - 1) <= t_cmp) & (n_idx < ncmp - 1)
    psum = jnp.zeros((ncmp, tq), F32)
    for h in range(N_HPG):
        s_ref[0, h, :ncmp, :] = _nt(kc, q_heads[h])
    for h in range(N_HPG):
        s = jnp.where(cmask, s_ref[0, h, :ncmp, :], NEG)
        e = jnp.where(cmask, jnp.exp(s - jnp.max(s, axis=0, keepdims=True)), 0.0)
        p = e * (1.0 / jnp.maximum(jnp.sum(e, axis=0, keepdims=True), 1e-30))
        psum = psum + p
        p_ref[0, h, :ncmp, :] = p.astype(BF16)
    for h in range(N_HPG):
        out_ref[h] = gates[3 * h:3 * h + 1, :] * jnp.dot(vct, p_ref[0, h, :ncmp, :], preferred_element_type=F32)

    imp = jnp.dot(ovt_ref[...], psum, precision=HIGHEST, preferred_element_type=F32)
    j_idx = lax.broadcasted_iota(jnp.int32, (nsel, tq), 0)
    t_row = t0 + lax.broadcasted_iota(jnp.int32, (nsel, tq), 1)
    cur = lax.shift_right_logical(t_row, 6)
    forced = (j_idx == 0) | (j_idx == cur) | (j_idx == cur - 1)
    score = jnp.where(forced, jnp.inf, jnp.where(j_idx * SEL_BLOCK <= t_row, imp, -jnp.inf))

    def pick(_, carry):
        score, sel = carry
        mx = jnp.max(score, axis=0, keepdims=True)
        idx = jnp.min(jnp.where(score == mx, j_idx, nsel), axis=0, keepdims=True)
        hit = j_idx == idx
        return jnp.where(hit, -jnp.inf, score), jnp.where(hit, 1.0, sel)

    _, sel = lax.fori_loop(0, min(SEL_TOPK, nsel), pick, (score, jnp.zeros((nsel, tq), F32)), unroll=True)
    sel_ref[...] = sel

    def flash_init():
        acc_ref[...] = jnp.zeros_like(acc_ref)
        m_ref[...] = jnp.full(m_ref.shape, NEG, F32)

    def flash_tiles(k_ref, vt_ref, tiles):
        vts = []
        for n, (j, _) in enumerate(tiles):
            start = pl.multiple_of(j * tq, tq)
            k = k_ref[pl.ds(start, tq), :]
            vts.append(vt_ref[:, pl.ds(start, tq)])
            for h in range(N_HPG):
                s_ref[n, h] = _nt(k, q_heads[h])
        for n, (_, mask) in enumerate(tiles):
            alphas = []
            for h in range(N_HPG):
                s = s_ref[n, h]
                if mask is not None:
                    s = jnp.where(mask, s, NEG)
                m_old = m_ref[h]
                m_new = jnp.maximum(m_old, jnp.max(s, axis=0, keepdims=True))
                p_ref[n, h] = jnp.exp((s - m_new).astype(BF16))
                alpha = jnp.exp(m_old - m_new)
                m_ref[h] = m_new
                alphas.append(alpha)
            for h in range(N_HPG):
                acc_ref[h] = alphas[h] * acc_ref[h] + jnp.dot(vts[n], p_ref[n, h], preferred_element_type=F32)

    def flash_add(branch):
        for h in range(N_HPG):
            scale = gates[3 * h + branch:3 * h + branch + 1, :] / acc_ref[h, N_HD:N_HD + 1, :]
            out_ref[h] = out_ref[h] + scale * acc_ref[h, :N_HD, :]

    kpos = lax.broadcasted_iota(jnp.int32, (tq, tq), 0)
    qpos = lax.broadcasted_iota(jnp.int32, (tq, tq), 1)
    blk_per_tile = tq // SEL_BLOCK

    def sel_mask(j):
        rows = [jnp.broadcast_to(sel_ref[pl.ds(j * blk_per_tile + b, 1), :], (SEL_BLOCK, tq))
                for b in range(blk_per_tile)]
        return jnp.concatenate(rows, axis=0) > 0.5

    flash_init()

    group = s_ref.shape[0]

    def sel_group(p, carry):
        flash_tiles(ks_ref, vst_ref, [(group * p + n, sel_mask(group * p + n)) for n in range(group)])
        return carry

    lax.fori_loop(0, qi // group, sel_group, 0)
    causal = kpos <= qpos
    for rem in range(group):
        @pl.when(qi % group == rem)
        def _():
            tiles = [(qi - rem + n, sel_mask(qi - rem + n)) for n in range(rem)]
            flash_tiles(ks_ref, vst_ref, tiles + [(qi, sel_mask(qi) & causal)])

    flash_add(1)

    flash_init()
    n_back = WINDOW // tq
    win_tiles = [(qi, causal)] + [(qi - back, None) for back in range(1, n_back)] + [(qi - n_back, kpos > qpos)]
    for n_tiles in range(1, n_back + 2):
        last = n_tiles == n_back + 1

        @pl.when((qi >= n_tiles - 1) if last else (qi == n_tiles - 1))
        def _():
            flash_tiles(kw_ref, vwt_ref, win_tiles[:n_tiles])
    flash_add(2)

    for h in range(N_HPG):
        o_ref[:, h * N_HD:(h + 1) * N_HD] = out_ref[h].T.astype(o_ref.dtype)


def _nsa(p1, p2, p3, k_cmp, v_cmp, ovt, B, T):
    tq = NSA_TQ
    ncmp = k_cmp.shape[2]
    p1 = p1.reshape(B, T, P1_W)
    p2 = p2.reshape(B, T, P2_W)
    p3 = p3.reshape(B, T, P3_W)
    hw = N_HPG * N_HD
    assert WINDOW % tq == 0 and tq % SEL_BLOCK == 0 and ncmp <= tq and NSA_GROUP >= WINDOW // tq + 1
    ones = jnp.ones((B, N_KV, NSA_ONES, T), BF16)

    def values_t(off):
        vt = p1[:, :, off:off + N_KV_W].transpose(0, 2, 1).reshape(B, N_KV, N_HD, T)
        return jnp.concatenate([vt, ones], axis=2)

    vst = values_t(P1_NVS)
    vwt = values_t(P1_NVW)
    vct = v_cmp.transpose(0, 1, 3, 2)
    gt = p3[:, :, P3_NG:P3_NG + N_KV * LANES].reshape(B, T, N_KV, LANES)[..., :16].transpose(0, 2, 3, 1)
    k_spec = lambda off: pl.BlockSpec((None, T, N_HD), lambda b, g, i: (b, 0, off // N_HD + g))
    vt_spec = pl.BlockSpec((None, None, N_HD + NSA_ONES, T), lambda b, g, i: (b, g, 0, 0))
    return pl.pallas_call(
        functools.partial(_nsa_body, tq=tq, ncmp=ncmp),
        grid=(B, N_KV, T // tq),
        in_specs=[pl.BlockSpec((None, tq, hw), lambda b, g, i: (b, i, P2_NQ // hw + g)),
                  k_spec(P2_NKS), vt_spec, k_spec(P2_NKW), vt_spec,
                  pl.BlockSpec((None, None, ncmp, N_HD), lambda b, g, i: (b, g, 0, 0)),
                  pl.BlockSpec((None, None, N_HD, ncmp), lambda b, g, i: (b, g, 0, 0)),
                  pl.BlockSpec((None, None, 16, tq), lambda b, g, i: (b, g, 0, i)),
                  pl.BlockSpec(ovt.shape, lambda b, g, i: (0, 0))],
        out_specs=pl.BlockSpec((None, tq, hw), lambda b, g, i: (b, i, g)),
        out_shape=jax.ShapeDtypeStruct((B, T, N_Q_W), BF16),
        scratch_shapes=[pltpu.VMEM((N_HPG, N_HD + NSA_ONES, tq), F32),
                        pltpu.VMEM((N_HPG, 1, tq), F32), pltpu.VMEM((ovt.shape[0], tq), F32),
                        pltpu.VMEM((N_HPG, N_HD, tq), F32),
                        pltpu.VMEM((NSA_GROUP, N_HPG, tq, tq), F32),
                        pltpu.VMEM((NSA_GROUP, N_HPG, tq, tq), BF16)],
        compiler_params=_params(("parallel", "parallel", "arbitrary"), 48),
        name="nsa_attention",
    )(p2, p2, vst, p2, vwt, k_cmp, vct, gt, ovt)


def _merge_body(hm_ref, hn_ref, wm_ref, wn_ref, gm_ref, gn_ref, o_ref):
    ym = jnp.dot(hm_ref[...], wm_ref[...], preferred_element_type=F32)
    yn = jnp.dot(hn_ref[...], wn_ref[...], preferred_element_type=F32)
    gm = jax.nn.sigmoid(gm_ref[...].astype(F32))
    gn = jax.nn.sigmoid(gn_ref[...].astype(F32))
    o_ref[...] = (gm * ym + gn * yn).astype(o_ref.dtype)


def _merge(hm, hn, wm, wn, p1):
    m = hm.shape[0]
    tm, tn = MM_TM, 512
    return pl.pallas_call(
        _merge_body,
        grid=(D_MODEL // tn, m // tm),
        in_specs=[pl.BlockSpec((tm, M_V_W), lambda j, i: (i, 0)),
                  pl.BlockSpec((tm, N_Q_W), lambda j, i: (i, 0)),
                  pl.BlockSpec((M_V_W, tn), lambda j, i: (0, j)),
                  pl.BlockSpec((N_Q_W, tn), lambda j, i: (0, j)),
                  pl.BlockSpec((tm, tn), lambda j, i: (i, P1_GM // tn + j)),
                  pl.BlockSpec((tm, tn), lambda j, i: (i, (P1_GM + D_MODEL) // tn + j))],
        out_specs=pl.BlockSpec((tm, tn), lambda j, i: (i, j)),
        out_shape=jax.ShapeDtypeStruct((m, D_MODEL), BF16),
        compiler_params=_params(("parallel", "parallel"), 48),
        name="branch_merge",
    )(hm, hn, wm, wn, p1, p1)


def _out_ln_body(y_ref, w_ref, x_ref, g_ref, b_ref, o_ref, ob_ref):
    mix = jnp.dot(y_ref[...], w_ref[...], preferred_element_type=F32)
    out = _standardize(ALPHA * x_ref[...] + mix) * g_ref[...] + b_ref[...]
    o_ref[...] = out
    ob_ref[...] = out.astype(BF16)


def _out_ln(y, w, x, g, b):
    m = y.shape[0]
    tm = MM_TM
    row = pl.BlockSpec((tm, D_MODEL), lambda i: (i, 0))
    vec = pl.BlockSpec((1, D_MODEL), lambda i: (0, 0))
    return pl.pallas_call(
        _out_ln_body,
        grid=(m // tm,),
        in_specs=[row, pl.BlockSpec((D_MODEL, D_MODEL), lambda i: (0, 0)), row, vec, vec],
        out_specs=[row, row],
        out_shape=[jax.ShapeDtypeStruct((m, D_MODEL), F32), jax.ShapeDtypeStruct((m, D_MODEL), BF16)],
        compiler_params=_params(("parallel",), 48),
        name="out_proj_layernorm",
    )(y, w, x, g.reshape(1, -1), b.reshape(1, -1))


def _first_of4(vals, target):
    return jnp.where(vals[0] == target, 0.0, jnp.where(vals[1] == target, 1.0, jnp.where(vals[2] == target, 2.0, 3.0)))


def _select4(idx, vals):
    return jnp.where(idx == 0.0, vals[0], jnp.where(idx == 1.0, vals[1], jnp.where(idx == 2.0, vals[2], vals[3])))


def _router_body(x_ref, rw_ref, rb_ref, o_ref, cnt_ref, carry_ref, *, tm):
    @pl.when(pl.program_id(0) == 0)
    def _():
        carry_ref[...] = jnp.zeros_like(carry_ref)

    logits = _nt(rw_ref[...], x_ref[...], precision=HIGHEST)
    aff = jax.nn.sigmoid(logits)
    biased = aff + rb_ref[:, 0:1]
    a = [biased[i * N_GROUPS:(i + 1) * N_GROUPS, :] for i in range(EXP_PER_GROUP)]
    af = [aff[i * N_GROUPS:(i + 1) * N_GROUPS, :] for i in range(EXP_PER_GROUP)]
    m1 = jnp.maximum(jnp.maximum(a[0], a[1]), jnp.maximum(a[2], a[3]))
    i1 = _first_of4(a, m1)
    rest = [jnp.where(i1 == float(i), -jnp.inf, a[i]) for i in range(EXP_PER_GROUP)]
    m2 = jnp.maximum(jnp.maximum(rest[0], rest[1]), jnp.maximum(rest[2], rest[3]))
    i2 = _first_of4(rest, m2)
    gscore = m1 + m2
    g_iota = lax.broadcasted_iota(jnp.int32, (N_GROUPS, tm), 0).astype(F32)
    g_idx = jnp.min(jnp.where(gscore == jnp.max(gscore, axis=0, keepdims=True), g_iota, float(N_GROUPS)),
                    axis=0, keepdims=True)
    in_g = g_iota == g_idx
    take = lambda v: jnp.sum(jnp.where(in_g, v, 0.0), axis=0, keepdims=True)
    s0 = take(i1)
    s1 = take(i2)
    w0 = take(_select4(i1, af))
    w1 = take(_select4(i2, af))
    wsum = w0 + w1
    r0 = s0 * N_GROUPS + g_idx
    r1 = s1 * N_GROUPS + g_idx
    r_iota = lax.broadcasted_iota(jnp.int32, (N_EXPERTS, tm), 0).astype(F32)
    member = (r_iota == r0) | (r_iota == r1)
    tt = lax.broadcasted_iota(jnp.int32, (tm, tm), 0)
    tc = lax.broadcasted_iota(jnp.int32, (tm, tm), 1)
    before = jnp.dot(member.astype(BF16), (tt < tc).astype(BF16), preferred_element_type=F32)
    base = before + carry_ref[:, 0:1]
    rank0 = jnp.sum(jnp.where(r_iota == r0, base, 0.0), axis=0, keepdims=True)
    rank1 = jnp.sum(jnp.where(r_iota == r1, base, 0.0), axis=0, keepdims=True)
    new_carry = carry_ref[:, 0:1] + jnp.sum(member.astype(F32), axis=1, keepdims=True)
    carry_ref[...] = jnp.broadcast_to(new_carry, carry_ref.shape)
    cnt_ref[...] = jnp.broadcast_to(new_carry, cnt_ref.shape)
    o_ref[0:1, :] = g_idx * EXP_PER_GROUP + s0
    o_ref[1:2, :] = g_idx * EXP_PER_GROUP + s1
    o_ref[2:3, :] = w0 / wsum
    o_ref[3:4, :] = w1 / wsum
    o_ref[4:5, :] = rank0
    o_ref[5:6, :] = rank1
    o_ref[6:8, :] = jnp.zeros((2, tm), F32)


def _router(x, rw_t, rb):
    m = x.shape[0]
    tm = ROUTER_TM
    return pl.pallas_call(
        functools.partial(_router_body, tm=tm),
        grid=(m // tm,),
        in_specs=[pl.BlockSpec((tm, D_MODEL), lambda i: (i, 0)),
                  pl.BlockSpec((N_EXPERTS, D_MODEL), lambda i: (0, 0)),
                  pl.BlockSpec((N_EXPERTS, LANES), lambda i: (0, 0))],
        out_specs=[pl.BlockSpec((8, tm), lambda i: (0, i)),
                   pl.BlockSpec((N_EXPERTS, LANES), lambda i: (0, 0))],
        out_shape=[jax.ShapeDtypeStruct((8, m), F32), jax.ShapeDtypeStruct((N_EXPERTS, LANES), F32)],
        scratch_shapes=[pltpu.VMEM((N_EXPERTS, LANES), F32)],
        compiler_params=_params(("arbitrary",), 48),
        name="moe_router",
    )(x, rw_t, rb)


def _row_copy(src_ref, src_row, dst_ref, dst_row, sem):
    return pltpu.make_async_copy(src_ref.at[pl.ds(src_row, 1), :], dst_ref.at[pl.ds(dst_row, 1), :], sem)


def _dispatch_body(dest_ref, x_ref, init_ref, xs_ref, sem, *, td):
    del init_ref

    def issue(r, c):
        _row_copy(x_ref, r, xs_ref, dest_ref[0, 0, r], sem).start(priority=0)
        _row_copy(x_ref, r, xs_ref, dest_ref[0, 0, td + r], sem).start(priority=1)
        return c

    lax.fori_loop(0, td, issue, 0)
    for _ in range(2):
        pltpu.make_async_copy(x_ref, xs_ref.at[pl.ds(0, td), :], sem).wait()


def _dispatch(x, dest, rows):
    m = x.shape[0]
    td = MOE_TD
    return pl.pallas_call(
        functools.partial(_dispatch_body, td=td),
        grid=(m // td,),
        in_specs=[pl.BlockSpec((1, 1, 2 * td), lambda i: (i, 0, 0), memory_space=pltpu.SMEM),
                  pl.BlockSpec((td, D_MODEL), lambda i: (i, 0)),
                  pl.BlockSpec(memory_space=pl.ANY)],
        out_specs=pl.BlockSpec(memory_space=pl.ANY),
        out_shape=jax.ShapeDtypeStruct((rows, D_MODEL), F32),
        scratch_shapes=[pltpu.SemaphoreType.DMA(())],
        input_output_aliases={2: 0},
        compiler_params=_params(("arbitrary",), 32),
        name="moe_dispatch",
    )(dest, x, jnp.zeros((rows, D_MODEL), F32))


def _expert_changed(te_ref, i):
    return (i == 0) | (te_ref[i] != te_ref[jnp.maximum(i - 1, 0)])


def _experts_up_body(te_ref, nused_ref, x_ref, w1_ref, w3_ref, h_ref, w1b_ref, w3b_ref):
    i = pl.program_id(0)
    used = i < nused_ref[0]

    @pl.when(used & _expert_changed(te_ref, i))
    def _():
        w1b_ref[...] = w1_ref[...].astype(BF16)
        w3b_ref[...] = w3_ref[...].astype(BF16)

    @pl.when(used)
    def _():
        xb = x_ref[...].astype(BF16)
        a = jnp.dot(xb, w1b_ref[...], preferred_element_type=F32)
        b = jnp.dot(xb, w3b_ref[...], preferred_element_type=F32)
        h_ref[...] = (a * jax.nn.sigmoid(a) * b).astype(BF16)

    @pl.when(jnp.logical_not(used))
    def _():
        h_ref[...] = jnp.zeros_like(h_ref)


def _experts_down_body(te_ref, nused_ref, h_ref, w2_ref, o_ref, w2b_ref):
    i = pl.program_id(0)
    used = i < nused_ref[0]

    @pl.when(used & _expert_changed(te_ref, i))
    def _():
        w2b_ref[...] = w2_ref[...].astype(BF16)

    @pl.when(used)
    def _():
        o_ref[...] = jnp.dot(h_ref[...], w2b_ref[...], preferred_element_type=F32)

    @pl.when(jnp.logical_not(used))
    def _():
        o_ref[...] = jnp.zeros_like(o_ref)


def _experts(xs, w1, w3, w2, layer, tile_expert, n_used):
    rows = xs.shape[0]
    tm = MOE_TM
    by_expert = lambda r, c: pl.BlockSpec((None, None, r, c), lambda i, te, nu: (layer, te[i], 0, 0))
    by_tile = lambda c: pl.BlockSpec((tm, c), lambda i, te, nu: (i, 0))
    h = pl.pallas_call(
        _experts_up_body,
        grid_spec=pltpu.PrefetchScalarGridSpec(
            num_scalar_prefetch=2, grid=(rows // tm,),
            in_specs=[by_tile(D_MODEL), by_expert(D_MODEL, D_FF), by_expert(D_MODEL, D_FF)],
            out_specs=by_tile(D_FF),
            scratch_shapes=[pltpu.VMEM((D_MODEL, D_FF), BF16), pltpu.VMEM((D_MODEL, D_FF), BF16)]),
        out_shape=jax.ShapeDtypeStruct((rows, D_FF), BF16),
        compiler_params=_params(("arbitrary",), 56),
        name="moe_experts_up",
    )(tile_expert, n_used, xs, w1, w3)
    return pl.pallas_call(
        _experts_down_body,
        grid_spec=pltpu.PrefetchScalarGridSpec(
            num_scalar_prefetch=2, grid=(rows // tm,),
            in_specs=[by_tile(D_FF), by_expert(D_FF, D_MODEL)],
            out_specs=by_tile(D_MODEL),
            scratch_shapes=[pltpu.VMEM((D_FF, D_MODEL), BF16)]),
        out_shape=jax.ShapeDtypeStruct((rows, D_MODEL), F32),
        compiler_params=_params(("arbitrary",), 40),
        name="moe_experts_down",
    )(tile_expert, n_used, h, w2)


def _combine_body(dest_ref, next_ref, ys_ref, x_ref, w_ref, g_ref, b_ref, o_ref, ob_ref, buf_ref, sem, *, td):
    i = pl.program_id(0)
    slot = i % 2

    def gather(idx_ref, s):
        def issue(r, c):
            _row_copy(ys_ref, idx_ref[0, 0, r], buf_ref.at[s, 0], r, sem.at[s]).start(priority=0)
            _row_copy(ys_ref, idx_ref[0, 0, td + r], buf_ref.at[s, 1], r, sem.at[s]).start(priority=1)
            return c

        lax.fori_loop(0, td, issue, 0)

    @pl.when(i == 0)
    def _():
        gather(dest_ref, 0)

    @pl.when(i + 1 < pl.num_programs(0))
    def _():
        gather(next_ref, 1 - slot)

    for half in range(2):
        pltpu.make_async_copy(ys_ref.at[pl.ds(0, td), :], buf_ref.at[slot, half], sem.at[slot]).wait()
    w = w_ref[...]
    ff = w[:, 0:1] * buf_ref[slot, 0] + w[:, 1:2] * buf_ref[slot, 1]
    out = _standardize(ALPHA * x_ref[...] + ff) * g_ref[...] + b_ref[...]
    o_ref[...] = out
    ob_ref[...] = out.astype(BF16)


def _combine(ys, dest, x, w, g, b):
    m = x.shape[0]
    td = MOE_TD
    row = pl.BlockSpec((td, D_MODEL), lambda i: (i, 0))
    vec = pl.BlockSpec((1, D_MODEL), lambda i: (0, 0))
    last = m // td - 1
    return pl.pallas_call(
        functools.partial(_combine_body, td=td),
        grid=(m // td,),
        in_specs=[pl.BlockSpec((1, 1, 2 * td), lambda i: (i, 0, 0), memory_space=pltpu.SMEM),
                  pl.BlockSpec((1, 1, 2 * td), lambda i: (jnp.minimum(i + 1, last), 0, 0), memory_space=pltpu.SMEM),
                  pl.BlockSpec(memory_space=pl.ANY), row,
                  pl.BlockSpec((td, 8), lambda i: (i, 0)), vec, vec],
        out_specs=[row, row],
        out_shape=[jax.ShapeDtypeStruct((m, D_MODEL), F32), jax.ShapeDtypeStruct((m, D_MODEL), BF16)],
        scratch_shapes=[pltpu.VMEM((2, 2, td, D_MODEL), F32), pltpu.SemaphoreType.DMA((2,))],
        compiler_params=_params(("arbitrary",), 40),
        name="moe_combine",
    )(dest, dest, ys, x, w, g.reshape(1, -1), b.reshape(1, -1))


def _moe(x, xb, rw_t, rb, w1, w3, w2, layer, ln_g, ln_b):
    del xb
    m = x.shape[0]
    tm, td = MOE_TM, MOE_TD
    rows = 2 * m + N_EXPERTS * tm
    ro, cnt = _router(x, rw_t, rb)
    e0 = ro[0].astype(jnp.int32)
    e1 = ro[1].astype(jnp.int32)
    counts = cnt[:, 0].astype(jnp.int32).reshape(EXP_PER_GROUP, N_GROUPS).T.reshape(N_EXPERTS)
    padded = (counts + tm - 1) // tm * tm
    ends = jnp.cumsum(padded)
    offs = ends - padded
    d0 = offs[e0] + ro[4].astype(jnp.int32)
    d1 = offs[e1] + ro[5].astype(jnp.int32)
    dest = jnp.concatenate([d0.reshape(m // td, 1, td), d1.reshape(m // td, 1, td)], axis=-1)
    tile_start = jnp.arange(rows // tm, dtype=jnp.int32) * tm
    tile_expert = jnp.minimum(jnp.sum(tile_start[:, None] >= ends[None, :], axis=1), N_EXPERTS - 1).astype(jnp.int32)
    n_used = (ends[-1:] // tm).astype(jnp.int32)
    xs = _dispatch(x, dest, rows)
    ys = _experts(xs, w1, w3, w2, layer, tile_expert, n_used)
    wcol = jnp.pad(ro[2:4].T, ((0, 0), (0, 6)))
    return _combine(ys, dest, x, wcol, ln_g, ln_b)


def _rope_tables(pos):
    half = N_HD // 2
    inv = ROPE_THETA ** (-jnp.arange(half, dtype=F32) / half)
    ang = pos.astype(F32)[:, None] * inv[None, :]
    cos, sin = jnp.cos(ang), jnp.sin(ang)
    return jnp.concatenate([cos, cos], -1), jnp.concatenate([-sin, sin], -1)


def _overlap_t(T, ncmp_pad):
    n_cmp = (T - CMP_BLOCK) // CMP_STRIDE + 1
    n_sel = T // SEL_BLOCK
    cs = np.arange(n_cmp) * CMP_STRIDE
    ss = np.arange(n_sel) * SEL_BLOCK
    ov = ((cs[:, None] <= ss[None, :] + SEL_BLOCK - 1) & (cs[:, None] + CMP_BLOCK - 1 >= ss[None, :])).astype(np.float32)
    out = np.zeros((n_sel, ncmp_pad), np.float32)
    out[:, :n_cmp] = ov.T
    return jnp.asarray(out)


def _pack_body(w_ref, o1_ref, o2_ref, o3_ref):
    (mq, mk, mv, mo, mi, mf, nq, nkc, nvc, nks, nvs, nkw, nvw, ng, gm, end) = IN_OFFS

    def put(o_ref, off, a, b, scale=1.0):
        o_ref[:, off:off + b - a] = (w_ref[:, a:b] * scale).astype(o_ref.dtype)

    put(o1_ref, P1_MQ, mq, mk, M_DQK ** -0.5)
    put(o1_ref, P1_MK, mk, mo)
    put(o1_ref, P1_MO, mo, mi)
    put(o1_ref, P1_NVS, nvs, nkw)
    put(o1_ref, P1_NVW, nvw, ng)
    put(o1_ref, P1_GM, gm, end)
    put(o2_ref, P2_NQ, nq, nkc, N_HD ** -0.5)
    put(o2_ref, P2_NKS, nks, nvs)
    put(o2_ref, P2_NKW, nkw, nvw)
    o3_ref[:, 0:P3_NKC] = jnp.zeros((o3_ref.shape[0], P3_NKC), o3_ref.dtype)
    put(o3_ref, P3_GATES, mi, nq)
    for g in range(N_KV):
        put(o3_ref, P3_NG + g * LANES, ng + g * 3 * N_HPG, ng + (g + 1) * 3 * N_HPG)
    put(o3_ref, P3_NKC, nkc, nks)


def _pack(w, layer, out_dtype, tr):
    rows = w.shape[1]
    out = lambda n: pl.BlockSpec((tr, n), lambda i: (i, 0))
    return pl.pallas_call(
        _pack_body,
        grid=(rows // tr,),
        in_specs=[pl.BlockSpec((None, tr, IN_OFFS[-1]), lambda i: (layer, i, 0))],
        out_specs=[out(P1_W), out(P2_W), out(P3_W)],
        out_shape=[jax.ShapeDtypeStruct((rows, n), out_dtype) for n in (P1_W, P2_W, P3_W)],
        compiler_params=_params(("parallel",), 56),
        name="pack_in_proj",
    )(w)


def kernel(x, w_in, b_in, m_norm_w, cmp_pos_k, cmp_w1_k, cmp_b1_k, cmp_w2_k, cmp_b2_k, cmp_pos_v, cmp_w1_v,
           cmp_b1_v, cmp_w2_v, cmp_b2_v, w_branch_m, w_branch_n, w_out, ln1_g, ln1_b, router_w, router_b,
           exp_w1, exp_w3, exp_w2, ln2_g, ln2_b):
    B, T, D = x.shape
    m = B * T
    nb = T // CMP_STRIDE
    cos_t, sin_t = _rope_tables(jnp.arange(T))
    cos_c, sin_c = _rope_tables(jnp.arange(nb) * CMP_STRIDE + CMP_BLOCK - 1)
    ovt = _overlap_t(T, nb)
    perm = (np.arange(N_GROUPS)[None, :] * EXP_PER_GROUP + np.arange(EXP_PER_GROUP)[:, None]).reshape(-1)
    rw_t = router_w.T[perm]
    rb = jnp.broadcast_to(router_b[perm][:, None], (N_EXPERTS, LANES))

    b_rows = jnp.pad(b_in[:, None, :], ((0, 0), (0, 7), (0, 0)))
    xf = x.reshape(m, D)
    xb = xf.astype(BF16)
    for l in range(DEPTH):
        w1p, w2p, w3p = _pack(w_in, l, BF16, 256)
        b1p, b2p, b3p = (b[0] for b in _pack(b_rows, l, F32, 8))
        p1 = _matmul(xb, w1p, b1p, BF16, MM_TN)
        p2 = _matmul(xb, w2p, b2p, BF16, MM_TN, rope=(cos_t, sin_t), seq=T)
        p3 = _matmul(xb, w3p, b3p, F32, P3_W)
        gt = p3.reshape(B, T, P3_W)[:, :, :8].transpose(0, 2, 1)
        h_m = _mlstm(p1, p3, gt, m_norm_w[l], B, T).reshape(m, M_V_W)

        def blocks16(off):
            a = p3.reshape(B, nb, CMP_STRIDE, P3_W)[..., off:off + N_KV_W]
            return a.reshape(B, nb, CMP_STRIDE, N_KV, N_HD).transpose(0, 3, 1, 2, 4).reshape(B, N_KV, nb, CMP_STRIDE * N_HD)

        k_cmp = _compress(blocks16(P3_NKC), cmp_pos_k[l], cmp_w1_k[l], cmp_b1_k[l], cmp_w2_k[l], cmp_b2_k[l],
                          cos_c, sin_c, True)
        v_cmp = _compress(blocks16(P3_NVC), cmp_pos_v[l], cmp_w1_v[l], cmp_b1_v[l], cmp_w2_v[l], cmp_b2_v[l],
                          cos_c, sin_c, False)
        h_n = _nsa(p1, p2, p3, k_cmp, v_cmp, ovt, B, T).reshape(m, N_Q_W)
        merged = _merge(h_m, h_n, w_branch_m[l].astype(BF16), w_branch_n[l].astype(BF16), p1)
        xf, xb = _out_ln(merged, w_out[l].astype(BF16), xf, ln1_g[l], ln1_b[l])
        xf, xb = _moe(xf, xb, rw_t, rb, exp_w1, exp_w3, exp_w2, l, ln2_g[l], ln2_b[l])
    return xf.reshape(B, T, D)
```

```python
import functools

import numpy as np
import jax
import jax.numpy as jnp
from jax import lax
from jax.experimental import pallas as pl
from jax.experimental.pallas import tpu as pltpu

F32 = jnp.float32
BF16 = jnp.bfloat16
HIGHEST = lax.Precision.HIGHEST

D_MODEL = 2048
DEPTH = 2
M_HEADS = 4
M_DQK = 256
M_DV = 512
N_KV = 4
N_HPG = 4
N_HEADS = N_KV * N_HPG
N_HD = 128
CMP_BLOCK = 32
CMP_STRIDE = 16
SEL_BLOCK = 64
SEL_TOPK = 16
WINDOW = 512
ROPE_THETA = 10000.0
N_EXPERTS = 32
N_GROUPS = 8
EXP_PER_GROUP = N_EXPERTS // N_GROUPS
D_FF = 1024
ALPHA = (2 * DEPTH) ** 0.25
EPS = 1e-5
NEG = -1e30

M_QK_W = M_HEADS * M_DQK
M_V_W = M_HEADS * M_DV
N_Q_W = N_HEADS * N_HD
N_KV_W = N_KV * N_HD
IN_SIZES = (M_QK_W, M_QK_W, M_V_W, M_V_W, M_HEADS, M_HEADS, N_Q_W, N_KV_W, N_KV_W, N_KV_W, N_KV_W,
            N_KV_W, N_KV_W, 3 * N_HEADS, 2 * D_MODEL)
IN_OFFS = tuple(int(v) for v in np.cumsum((0,) + IN_SIZES))

LANES = 128
MLSTM_CHUNK = 256
NSA_TQ = 256
NSA_GROUP = 4
NSA_ONES = 16
MM_TM = 512
PROJ_TN = 512
MOE_TM = 256
MOE_TD = 256
ROUTER_TM = 512

P1_MQ, P1_MK, P1_MV, P1_MO, P1_NVS, P1_NVW, P1_GM = 0, 1024, 2048, 4096, 6144, 6656, 7168
P1_W = 7168 + 2 * D_MODEL
P2_NQ, P2_NKS, P2_NKW = 0, 2048, 2560
P2_W = 3072


def _params(sem, vmem_mb):
    return pltpu.CompilerParams(dimension_semantics=sem, vmem_limit_bytes=vmem_mb * 1024 * 1024)


def _nt(a, b, **kw):
    return lax.dot_general(a, b, (((1,), (1,)), ((), ())), preferred_element_type=F32, **kw)


def _tn(a, b):
    return lax.dot_general(a, b, (((0,), (0,)), ((), ())), preferred_element_type=F32)


def _standardize(x):
    mu = jnp.mean(x, -1, keepdims=True)
    xc = x - mu
    var = jnp.mean(xc * xc, -1, keepdims=True)
    return xc * lax.rsqrt(var + EPS)


def _proj_body(offs_ref, x_ref, w_hbm, s_ref, b_ref, *rest, rope, tn, layer):
    if rope:
        cos_ref, sin_ref, o_ref, wbuf_ref, wb_ref, sem = rest
    else:
        o_ref, wbuf_ref, wb_ref, sem = rest
    j = pl.program_id(0)
    slot = j % 2

    def fetch(col, s):
        start = pl.multiple_of(offs_ref[col], 8)
        return pltpu.make_async_copy(w_hbm.at[layer, pl.ds(start, tn), :], wbuf_ref.at[s], sem.at[s])

    @pl.when(pl.program_id(1) == 0)
    def _():
        @pl.when(j == 0)
        def _():
            fetch(0, 0).start()

        fetch(j, slot).wait()
        wb_ref[...] = wbuf_ref[slot].astype(BF16)

        @pl.when(j + 1 < pl.num_programs(0))
        def _():
            fetch(j + 1, 1 - slot).start()

    acc = _nt(x_ref[...], wb_ref[...]) * s_ref[...] + b_ref[...]
    if rope:
        cos = cos_ref[...]
        sin = sin_ref[...]
        for c in range(tn // N_HD):
            ch = acc[:, c * N_HD:(c + 1) * N_HD]
            o_ref[:, c * N_HD:(c + 1) * N_HD] = (ch * cos + pltpu.roll(ch, N_HD // 2, 1) * sin).astype(o_ref.dtype)
    else:
        o_ref[...] = acc.astype(o_ref.dtype)


def _proj(x, w_t, layer, offs, scale, bias, out_dtype, tn, rope=None, seq=None):
    m, k = x.shape
    n = len(offs) * tn
    tm = MM_TM
    in_specs = [pl.BlockSpec((tm, k), lambda j, i, o: (i, 0)),
                pl.BlockSpec(memory_space=pl.ANY),
                pl.BlockSpec((1, tn), lambda j, i, o: (0, j)),
                pl.BlockSpec((1, tn), lambda j, i, o: (0, j))]
    args = [x, w_t, scale.reshape(1, n), bias.reshape(1, n)]
    if rope is not None:
        nt = seq // tm
        in_specs += [pl.BlockSpec((tm, N_HD), lambda j, i, o: (i % nt, 0))] * 2
        args += list(rope)
    return pl.pallas_call(
        functools.partial(_proj_body, rope=rope is not None, tn=tn, layer=layer),
        grid_spec=pltpu.PrefetchScalarGridSpec(
            num_scalar_prefetch=1, grid=(len(offs), m // tm), in_specs=in_specs,
            out_specs=pl.BlockSpec((tm, tn), lambda j, i, o: (i, j)),
            scratch_shapes=[pltpu.VMEM((2, tn, k), F32), pltpu.VMEM((tn, k), BF16), pltpu.SemaphoreType.DMA((2,))]),
        out_shape=jax.ShapeDtypeStruct((m, n), out_dtype),
        compiler_params=_params(("arbitrary", "arbitrary"), 48),
        name="proj_matmul",
    )(jnp.asarray(offs, jnp.int32), *args)


def _log_sigmoid(x):
    return jnp.minimum(x, 0.0) - jnp.log1p(jnp.exp(-jnp.abs(x)))


def _mlstm_body(q_ref, k_ref, v_ref, og_ref, g_ref, gt_ref, nw_ref, out_ref, c_ref, n_ref, m_ref, *, L):
    @pl.when(pl.program_id(1) == 0)
    def _():
        c_ref[...] = jnp.zeros_like(c_ref)
        n_ref[...] = jnp.zeros_like(n_ref)
        m_ref[...] = jnp.zeros_like(m_ref)

    g = g_ref[...]
    gt = gt_ref[...]
    row = lax.broadcasted_iota(jnp.int32, (L, L), 0)
    col = lax.broadcasted_iota(jnp.int32, (L, L), 1)
    causal = row >= col
    b_all = jnp.dot(causal.astype(F32), _log_sigmoid(g), precision=HIGHEST, preferred_element_type=F32)
    bt_all = jnp.dot(_log_sigmoid(gt), (row <= col).astype(F32), precision=HIGHEST, preferred_element_type=F32)
    for h in range(M_HEADS):
        b_col = b_all[:, M_HEADS + h:M_HEADS + h + 1]
        li_col = g[:, h:h + 1]
        b_row = bt_all[M_HEADS + h:M_HEADS + h + 1, :]
        li_row = gt[h:h + 1, :]
        m_prev = m_ref[h][:, 0:1]
        dmat = jnp.where(causal, b_col - b_row + li_row, -jnp.inf)
        inter = b_col + m_prev
        m_t = jnp.maximum(inter, jnp.max(dmat, axis=1, keepdims=True))
        w_inter = jnp.exp(inter - m_t)
        q = q_ref[:, h * M_DQK:(h + 1) * M_DQK]
        k = k_ref[:, h * M_DQK:(h + 1) * M_DQK]
        v = v_ref[:, h * M_DV:(h + 1) * M_DV]
        s = _nt(q, k) * jnp.exp(dmat - m_t)
        ct = c_ref[h]
        n_row = n_ref[h]
        num = jnp.dot(s.astype(BF16), v, preferred_element_type=F32) + w_inter * jnp.dot(
            q, ct.astype(BF16), preferred_element_type=F32)
        qn = jnp.sum(q.astype(F32) * n_row, axis=1, keepdims=True)
        den = jnp.sum(s, axis=1, keepdims=True) + w_inter * qn
        hh = num / jnp.maximum(jnp.abs(den), jnp.exp(-m_t))
        hn = _standardize(hh) * nw_ref[:, h * M_DV:(h + 1) * M_DV]
        og = og_ref[:, h * M_DV:(h + 1) * M_DV].astype(F32)
        out_ref[:, h * M_DV:(h + 1) * M_DV] = (hn * jax.nn.sigmoid(og)).astype(out_ref.dtype)
        b_last = b_col[L - 1:L, :]
        g_col = b_last - b_col + li_col
        g_row = b_last - b_row + li_row
        m_new = jnp.maximum(b_last + m_prev, jnp.max(g_row, axis=1, keepdims=True))
        decay = jnp.exp(b_last + m_prev - m_new)
        ws_col = jnp.exp(g_col - m_new)
        kf = k.astype(F32)
        vw = (v.astype(F32) * ws_col).astype(BF16)
        c_ref[h] = decay * ct + _tn(k, vw)
        n_ref[h] = decay * n_row + jnp.sum(kf * ws_col, axis=0, keepdims=True)
        m_ref[h] = jnp.broadcast_to(m_new, (1, LANES))


def _mlstm(p1, p3, gt, norm_w, B, T):
    L = MLSTM_CHUNK
    p1 = p1.reshape(B, T, P1_W)
    return pl.pallas_call(
        functools.partial(_mlstm_body, L=L),
        grid=(B, T // L),
        in_specs=[pl.BlockSpec((None, L, M_QK_W), lambda b, c: (b, c, P1_MQ // M_QK_W)),
                  pl.BlockSpec((None, L, M_QK_W), lambda b, c: (b, c, P1_MK // M_QK_W)),
                  pl.BlockSpec((None, L, M_V_W), lambda b, c: (b, c, P1_MV // M_V_W)),
                  pl.BlockSpec((None, L, M_V_W), lambda b, c: (b, c, P1_MO // M_V_W)),
                  pl.BlockSpec((None, L, LANES), lambda b, c: (b, c, 0)),
                  pl.BlockSpec((None, 8, L), lambda b, c: (b, 0, c)),
                  pl.BlockSpec((1, M_V_W), lambda b, c: (0, 0))],
        out_specs=pl.BlockSpec((None, L, M_V_W), lambda b, c: (b, c, 0)),
        out_shape=jax.ShapeDtypeStruct((B, T, M_V_W), BF16),
        scratch_shapes=[pltpu.VMEM((M_HEADS, M_DQK, M_DV), F32),
                        pltpu.VMEM((M_HEADS, 1, M_DQK), F32),
                        pltpu.VMEM((M_HEADS, 1, LANES), F32)],
        compiler_params=_params(("parallel", "arbitrary"), 48),
        name="mlstm",
    )(p1, p1, p1, p1, p3, gt, norm_w.reshape(1, M_V_W))


def _compress_body(x_ref, pos_ref, w1_ref, b1_ref, w2_ref, b2_ref, cos_ref, sin_ref, o_ref, *, rope, nb):
    x = x_ref[...]
    half = CMP_STRIDE * N_HD
    lo = jnp.dot((x + pos_ref[:, :half]).astype(BF16), w1_ref[:half, :], preferred_element_type=F32)
    hi = jnp.dot((x + pos_ref[:, half:]).astype(BF16), w1_ref[half:, :], preferred_element_type=F32)
    pre = lo + pltpu.roll(hi, nb - 1, 0) + b1_ref[...]
    h = jax.nn.gelu(pre)
    y = jnp.dot(h.astype(BF16), w2_ref[...], preferred_element_type=F32) + b2_ref[...]
    if rope:
        y = y * cos_ref[...] + pltpu.roll(y, N_HD // 2, 1) * sin_ref[...]
    keep = lax.broadcasted_iota(jnp.int32, (nb, N_HD), 0) < nb - 1
    o_ref[...] = jnp.where(keep, y, 0.0).astype(o_ref.dtype)


def _compress(x2, pos, w1, b1, w2, b2, cos, sin, rope):
    B, G, nb, _ = x2.shape
    full = lambda shape: pl.BlockSpec(shape, lambda b, g: (0,) * len(shape))
    return pl.pallas_call(
        functools.partial(_compress_body, rope=rope, nb=nb),
        grid=(B, G),
        in_specs=[pl.BlockSpec((None, None, nb, CMP_STRIDE * N_HD), lambda b, g: (b, g, 0, 0)),
                  full((1, CMP_BLOCK * N_HD)), full((CMP_BLOCK * N_HD, N_HD)), full((1, N_HD)),
                  full((N_HD, N_HD)), full((1, N_HD)), full((nb, N_HD)), full((nb, N_HD))],
        out_specs=pl.BlockSpec((None, None, nb, N_HD), lambda b, g: (b, g, 0, 0)),
        out_shape=jax.ShapeDtypeStruct((B, G, nb, N_HD), BF16),
        compiler_params=_params(("parallel", "parallel"), 32),
        name="nsa_compress",
    )(x2, pos.reshape(1, -1), w1.reshape(CMP_BLOCK * N_HD, N_HD).astype(BF16), b1.reshape(1, N_HD),
      w2.astype(BF16), b2.reshape(1, N_HD), cos, sin)


def _nsa_body(q_ref, ks_ref, vst_ref, kw_ref, vwt_ref, kc_ref, vct_ref, gt_ref, ovt_ref, o_ref,
              acc_ref, m_ref, sel_ref, out_ref, s_ref, p_ref, *, tq, ncmp):
    qi = pl.program_id(2)
    t0 = qi * tq
    nsel = ovt_ref.shape[0]
    gates = jax.nn.sigmoid(gt_ref[...])
    q_heads = [q_ref[:, h * N_HD:(h + 1) * N_HD] for h in range(N_HPG)]

    kc = kc_ref[...]
    vct = vct_ref[...]
    n_idx = lax.broadcasted_iota(jnp.int32, (ncmp, tq), 0)
    t_cmp = t0 + lax.broadcasted_iota(jnp.int32, (ncmp, tq), 1)
    cmask = (n_idx * CMP_STRIDE + (CMP_BLOCK - 1) <= t_cmp) & (n_idx < ncmp - 1)
    psum = jnp.zeros((ncmp, tq), F32)
    for h in range(N_HPG):
        s_ref[0, h, :ncmp, :] = _nt(kc, q_heads[h])
    for h in range(N_HPG):
        s = jnp.where(cmask, s_ref[0, h, :ncmp, :], NEG)
        e = jnp.where(cmask, jnp.exp(s - jnp.max(s, axis=0, keepdims=True)), 0.0)
        p = e * (1.0 / jnp.maximum(jnp.sum(e, axis=0, keepdims=True), 1e-30))
        psum = psum + p
        p_ref[0, h, :ncmp, :] = p.astype(BF16)
    for h in range(N_HPG):
        out_ref[h] = gates[3 * h:3 * h + 1, :] * jnp.dot(vct, p_ref[0, h, :ncmp, :], preferred_element_type=F32)

    imp = jnp.dot(ovt_ref[...], psum, precision=HIGHEST, preferred_element_type=F32)
    j_idx = lax.broadcasted_iota(jnp.int32, (nsel, tq), 0)
    t_row = t0 + lax.broadcasted_iota(jnp.int32, (nsel, tq), 1)
    cur = lax.shift_right_logical(t_row, 6)
    forced = (j_idx == 0) | (j_idx == cur) | (j_idx == cur - 1)
    score = jnp.where(forced, jnp.inf, jnp.where(j_idx * SEL_BLOCK <= t_row, imp, -jnp.inf))

    def pick(_, carry):
        score, sel = carry
        mx = jnp.max(score, axis=0, keepdims=True)
        idx = jnp.min(jnp.where(score == mx, j_idx, nsel), axis=0, keepdims=True)
        hit = j_idx == idx
        return jnp.where(hit, -jnp.inf, score), jnp.where(hit, 1.0, sel)

    _, sel = lax.fori_loop(0, min(SEL_TOPK, nsel), pick, (score, jnp.zeros((nsel, tq), F32)), unroll=True)
    sel_ref[...] = sel

    def flash_init():
        acc_ref[...] = jnp.zeros_like(acc_ref)
        m_ref[...] = jnp.full(m_ref.shape, NEG, F32)

    def flash_tiles(k_ref, vt_ref, tiles):
        vts = []
        for n, (j, _) in enumerate(tiles):
            start = pl.multiple_of(j * tq, tq)
            k = k_ref[pl.ds(start, tq), :]
            vts.append(vt_ref[:, pl.ds(start, tq)])
            for h in range(N_HPG):
                s_ref[n, h] = _nt(k, q_heads[h])
        for n, (_, mask) in enumerate(tiles):
            alphas = []
            for h in range(N_HPG):
                s = s_ref[n, h]
                if mask is not None:
                    s = jnp.where(mask, s, NEG)
                m_old = m_ref[h]
                m_new = jnp.maximum(m_old, jnp.max(s, axis=0, keepdims=True))
                p_ref[n, h] = jnp.exp((s - m_new).astype(BF16))
                alpha = jnp.exp(m_old - m_new)
                m_ref[h] = m_new
                alphas.append(alpha)
            for h in range(N_HPG):
                acc_ref[h] = alphas[h] * acc_ref[h] + jnp.dot(vts[n], p_ref[n, h], preferred_element_type=F32)

    def flash_add(branch):
        for h in range(N_HPG):
            scale = gates[3 * h + branch:3 * h + branch + 1, :] / acc_ref[h, N_HD:N_HD + 1, :]
            out_ref[h] = out_ref[h] + scale * acc_ref[h, :N_HD, :]

    kpos = lax.broadcasted_iota(jnp.int32, (tq, tq), 0)
    qpos = lax.broadcasted_iota(jnp.int32, (tq, tq), 1)
    blk_per_tile = tq // SEL_BLOCK

    def sel_mask(j):
        rows = [jnp.broadcast_to(sel_ref[pl.ds(j * blk_per_tile + b, 1), :], (SEL_BLOCK, tq))
                for b in range(blk_per_tile)]
        return jnp.concatenate(rows, axis=0) > 0.5

    flash_init()

    group = s_ref.shape[0]

    def sel_group(p, carry):
        flash_tiles(ks_ref, vst_ref, [(group * p + n, sel_mask(group * p + n)) for n in range(group)])
        return carry

    lax.fori_loop(0, qi // group, sel_group, 0)
    causal = kpos <= qpos
    for rem in range(group):
        @pl.when(qi % group == rem)
        def _():
            tiles = [(qi - rem + n, sel_mask(qi - rem + n)) for n in range(rem)]
            flash_tiles(ks_ref, vst_ref, tiles + [(qi, sel_mask(qi) & causal)])

    flash_add(1)

    flash_init()
    n_back = WINDOW // tq
    win_tiles = [(qi, causal)] + [(qi - back, None) for back in range(1, n_back)] + [(qi - n_back, kpos > qpos)]
    for n_tiles in range(1, n_back + 2):
        last = n_tiles == n_back + 1

        @pl.when((qi >= n_tiles - 1) if last else (qi == n_tiles - 1))
        def _():
            flash_tiles(kw_ref, vwt_ref, win_tiles[:n_tiles])
    flash_add(2)

    for h in range(N_HPG):
        o_ref[:, h * N_HD:(h + 1) * N_HD] = out_ref[h].T.astype(o_ref.dtype)


def _nsa(p1, p2, gt, k_cmp, v_cmp, ovt, B, T):
    tq = NSA_TQ
    ncmp = k_cmp.shape[2]
    p1 = p1.reshape(B, T, P1_W)
    p2 = p2.reshape(B, T, P2_W)
    hw = N_HPG * N_HD
    assert WINDOW % tq == 0 and tq % SEL_BLOCK == 0 and ncmp <= tq and NSA_GROUP >= WINDOW // tq + 1
    ones = jnp.ones((B, N_KV, NSA_ONES, T), BF16)

    def values_t(off):
        vt = p1[:, :, off:off + N_KV_W].transpose(0, 2, 1).reshape(B, N_KV, N_HD, T)
        return jnp.concatenate([vt, ones], axis=2)

    vst = values_t(P1_NVS)
    vwt = values_t(P1_NVW)
    vct = v_cmp.transpose(0, 1, 3, 2)
    k_spec = lambda off: pl.BlockSpec((None, T, N_HD), lambda b, g, i: (b, 0, off // N_HD + g))
    vt_spec = pl.BlockSpec((None, None, N_HD + NSA_ONES, T), lambda b, g, i: (b, g, 0, 0))
    return pl.pallas_call(
        functools.partial(_nsa_body, tq=tq, ncmp=ncmp),
        grid=(B, N_KV, T // tq),
        in_specs=[pl.BlockSpec((None, tq, hw), lambda b, g, i: (b, i, P2_NQ // hw + g)),
                  k_spec(P2_NKS), vt_spec, k_spec(P2_NKW), vt_spec,
                  pl.BlockSpec((None, None, ncmp, N_HD), lambda b, g, i: (b, g, 0, 0)),
                  pl.BlockSpec((None, None, N_HD, ncmp), lambda b, g, i: (b, g, 0, 0)),
                  pl.BlockSpec((None, None, 16, tq), lambda b, g, i: (b, g, 0, i)),
                  pl.BlockSpec(ovt.shape, lambda b, g, i: (0, 0))],
        out_specs=pl.BlockSpec((None, tq, hw), lambda b, g, i: (b, i, g)),
        out_shape=jax.ShapeDtypeStruct((B, T, N_Q_W), BF16),
        scratch_shapes=[pltpu.VMEM((N_HPG, N_HD + NSA_ONES, tq), F32),
                        pltpu.VMEM((N_HPG, 1, tq), F32), pltpu.VMEM((ovt.shape[0], tq), F32),
                        pltpu.VMEM((N_HPG, N_HD, tq), F32),
                        pltpu.VMEM((NSA_GROUP, N_HPG, tq, tq), F32),
                        pltpu.VMEM((NSA_GROUP, N_HPG, tq, tq), BF16)],
        compiler_params=_params(("parallel", "parallel", "arbitrary"), 48),
        name="nsa_attention",
    )(p2, p2, vst, p2, vwt, k_cmp, vct, gt, ovt)


def _merge_body(hm_ref, hn_ref, wm_ref, wn_ref, gm_ref, gn_ref, o_ref):
    ym = jnp.dot(hm_ref[...], wm_ref[...], preferred_element_type=F32)
    yn = jnp.dot(hn_ref[...], wn_ref[...], preferred_element_type=F32)
    gm = jax.nn.sigmoid(gm_ref[...].astype(F32))
    gn = jax.nn.sigmoid(gn_ref[...].astype(F32))
    o_ref[...] = (gm * ym + gn * yn).astype(o_ref.dtype)


def _merge(hm, hn, wm, wn, p1):
    m = hm.shape[0]
    tm, tn = MM_TM, 512
    return pl.pallas_call(
        _merge_body,
        grid=(D_MODEL // tn, m // tm),
        in_specs=[pl.BlockSpec((tm, M_V_W), lambda j, i: (i, 0)),
                  pl.BlockSpec((tm, N_Q_W), lambda j, i: (i, 0)),
                  pl.BlockSpec((M_V_W, tn), lambda j, i: (0, j)),
                  pl.BlockSpec((N_Q_W, tn), lambda j, i: (0, j)),
                  pl.BlockSpec((tm, tn), lambda j, i: (i, P1_GM // tn + j)),
                  pl.BlockSpec((tm, tn), lambda j, i: (i, (P1_GM + D_MODEL) // tn + j))],
        out_specs=pl.BlockSpec((tm, tn), lambda j, i: (i, j)),
        out_shape=jax.ShapeDtypeStruct((m, D_MODEL), BF16),
        compiler_params=_params(("parallel", "parallel"), 48),
        name="branch_merge",
    )(hm, hn, wm, wn, p1, p1)


def _out_ln_body(y_ref, w_ref, x_ref, g_ref, b_ref, o_ref, ob_ref):
    mix = jnp.dot(y_ref[...], w_ref[...], preferred_element_type=F32)
    out = _standardize(ALPHA * x_ref[...] + mix) * g_ref[...] + b_ref[...]
    o_ref[...] = out
    ob_ref[...] = out.astype(BF16)


def _out_ln(y, w, x, g, b):
    m = y.shape[0]
    tm = MM_TM
    row = pl.BlockSpec((tm, D_MODEL), lambda i: (i, 0))
    vec = pl.BlockSpec((1, D_MODEL), lambda i: (0, 0))
    return pl.pallas_call(
        _out_ln_body,
        grid=(m // tm,),
        in_specs=[row, pl.BlockSpec((D_MODEL, D_MODEL), lambda i: (0, 0)), row, vec, vec],
        out_specs=[row, row],
        out_shape=[jax.ShapeDtypeStruct((m, D_MODEL), F32), jax.ShapeDtypeStruct((m, D_MODEL), BF16)],
        compiler_params=_params(("parallel",), 48),
        name="out_proj_layernorm",
    )(y, w, x, g.reshape(1, -1), b.reshape(1, -1))


def _first_of4(vals, target):
    return jnp.where(vals[0] == target, 0.0, jnp.where(vals[1] == target, 1.0, jnp.where(vals[2] == target, 2.0, 3.0)))


def _select4(idx, vals):
    return jnp.where(idx == 0.0, vals[0], jnp.where(idx == 1.0, vals[1], jnp.where(idx == 2.0, vals[2], vals[3])))


def _router_body(x_ref, rw_ref, rb_ref, o_ref, cnt_ref, carry_ref, *, tm):
    @pl.when(pl.program_id(0) == 0)
    def _():
        carry_ref[...] = jnp.zeros_like(carry_ref)

    logits = _nt(rw_ref[...], x_ref[...], precision=HIGHEST)
    aff = jax.nn.sigmoid(logits)
    biased = aff + rb_ref[:, 0:1]
    a = [biased[i * N_GROUPS:(i + 1) * N_GROUPS, :] for i in range(EXP_PER_GROUP)]
    af = [aff[i * N_GROUPS:(i + 1) * N_GROUPS, :] for i in range(EXP_PER_GROUP)]
    m1 = jnp.maximum(jnp.maximum(a[0], a[1]), jnp.maximum(a[2], a[3]))
    i1 = _first_of4(a, m1)
    rest = [jnp.where(i1 == float(i), -jnp.inf, a[i]) for i in range(EXP_PER_GROUP)]
    m2 = jnp.maximum(jnp.maximum(rest[0], rest[1]), jnp.maximum(rest[2], rest[3]))
    i2 = _first_of4(rest, m2)
    gscore = m1 + m2
    g_iota = lax.broadcasted_iota(jnp.int32, (N_GROUPS, tm), 0).astype(F32)
    g_idx = jnp.min(jnp.where(gscore == jnp.max(gscore, axis=0, keepdims=True), g_iota, float(N_GROUPS)),
                    axis=0, keepdims=True)
    in_g = g_iota == g_idx
    take = lambda v: jnp.sum(jnp.where(in_g, v, 0.0), axis=0, keepdims=True)
    s0 = take(i1)
    s1 = take(i2)
    w0 = take(_select4(i1, af))
    w1 = take(_select4(i2, af))
    wsum = w0 + w1
    r0 = s0 * N_GROUPS + g_idx
    r1 = s1 * N_GROUPS + g_idx
    r_iota = lax.broadcasted_iota(jnp.int32, (N_EXPERTS, tm), 0).astype(F32)
    member = (r_iota == r0) | (r_iota == r1)
    tt = lax.broadcasted_iota(jnp.int32, (tm, tm), 0)
    tc = lax.broadcasted_iota(jnp.int32, (tm, tm), 1)
    before = jnp.dot(member.astype(BF16), (tt < tc).astype(BF16), preferred_element_type=F32)
    base = before + carry_ref[:, 0:1]
    rank0 = jnp.sum(jnp.where(r_iota == r0, base, 0.0), axis=0, keepdims=True)
    rank1 = jnp.sum(jnp.where(r_iota == r1, base, 0.0), axis=0, keepdims=True)
    new_carry = carry_ref[:, 0:1] + jnp.sum(member.astype(F32), axis=1, keepdims=True)
    carry_ref[...] = jnp.broadcast_to(new_carry, carry_ref.shape)
    cnt_ref[...] = jnp.broadcast_to(new_carry, cnt_ref.shape)
    o_ref[0:1, :] = g_idx * EXP_PER_GROUP + s0
    o_ref[1:2, :] = g_idx * EXP_PER_GROUP + s1
    o_ref[2:3, :] = w0 / wsum
    o_ref[3:4, :] = w1 / wsum
    o_ref[4:5, :] = rank0
    o_ref[5:6, :] = rank1
    o_ref[6:8, :] = jnp.zeros((2, tm), F32)


def _router(x, rw_t, rb):
    m = x.shape[0]
    tm = ROUTER_TM
    return pl.pallas_call(
        functools.partial(_router_body, tm=tm),
        grid=(m // tm,),
        in_specs=[pl.BlockSpec((tm, D_MODEL), lambda i: (i, 0)),
                  pl.BlockSpec((N_EXPERTS, D_MODEL), lambda i: (0, 0)),
                  pl.BlockSpec((N_EXPERTS, LANES), lambda i: (0, 0))],
        out_specs=[pl.BlockSpec((8, tm), lambda i: (0, i)),
                   pl.BlockSpec((N_EXPERTS, LANES), lambda i: (0, 0))],
        out_shape=[jax.ShapeDtypeStruct((8, m), F32), jax.ShapeDtypeStruct((N_EXPERTS, LANES), F32)],
        scratch_shapes=[pltpu.VMEM((N_EXPERTS, LANES), F32)],
        compiler_params=_params(("arbitrary",), 48),
        name="moe_router",
    )(x, rw_t, rb)


def _row_copy(src_ref, src_row, dst_ref, dst_row, sem):
    return pltpu.make_async_copy(src_ref.at[pl.ds(src_row, 1), :], dst_ref.at[pl.ds(dst_row, 1), :], sem)


def _dispatch_body(dest_ref, x_ref, init_ref, xs_ref, sem, *, td):
    del init_ref

    def issue(r, c):
        _row_copy(x_ref, r, xs_ref, dest_ref[0, 0, r], sem).start(priority=0)
        _row_copy(x_ref, r, xs_ref, dest_ref[0, 0, td + r], sem).start(priority=1)
        return c

    lax.fori_loop(0, td, issue, 0)
    for _ in range(2):
        pltpu.make_async_copy(x_ref, xs_ref.at[pl.ds(0, td), :], sem).wait()


def _dispatch(x, dest, rows):
    m = x.shape[0]
    td = MOE_TD
    return pl.pallas_call(
        functools.partial(_dispatch_body, td=td),
        grid=(m // td,),
        in_specs=[pl.BlockSpec((1, 1, 2 * td), lambda i: (i, 0, 0), memory_space=pltpu.SMEM),
                  pl.BlockSpec((td, D_MODEL), lambda i: (i, 0)),
                  pl.BlockSpec(memory_space=pl.ANY)],
        out_specs=pl.BlockSpec(memory_space=pl.ANY),
        out_shape=jax.ShapeDtypeStruct((rows, D_MODEL), F32),
        scratch_shapes=[pltpu.SemaphoreType.DMA(())],
        input_output_aliases={2: 0},
        compiler_params=_params(("arbitrary",), 32),
        name="moe_dispatch",
    )(dest, x, jnp.zeros((rows, D_MODEL), F32))


def _expert_changed(te_ref, i):
    return (i == 0) | (te_ref[i] != te_ref[jnp.maximum(i - 1, 0)])


def _experts_up_body(te_ref, nused_ref, x_ref, w1_ref, w3_ref, h_ref, w1b_ref, w3b_ref):
    i = pl.program_id(0)
    used = i < nused_ref[0]

    @pl.when(used & _expert_changed(te_ref, i))
    def _():
        w1b_ref[...] = w1_ref[...].astype(BF16)
        w3b_ref[...] = w3_ref[...].astype(BF16)

    @pl.when(used)
    def _():
        xb = x_ref[...].astype(BF16)
        a = jnp.dot(xb, w1b_ref[...], preferred_element_type=F32)
        b = jnp.dot(xb, w3b_ref[...], preferred_element_type=F32)
        h_ref[...] = (a * jax.nn.sigmoid(a) * b).astype(BF16)

    @pl.when(jnp.logical_not(used))
    def _():
        h_ref[...] = jnp.zeros_like(h_ref)


def _experts_down_body(te_ref, nused_ref, h_ref, w2_ref, o_ref, w2b_ref):
    i = pl.program_id(0)
    used = i < nused_ref[0]

    @pl.when(used & _expert_changed(te_ref, i))
    def _():
        w2b_ref[...] = w2_ref[...].astype(BF16)

    @pl.when(used)
    def _():
        o_ref[...] = jnp.dot(h_ref[...], w2b_ref[...], preferred_element_type=F32)

    @pl.when(jnp.logical_not(used))
    def _():
        o_ref[...] = jnp.zeros_like(o_ref)


def _experts(xs, w1, w3, w2, layer, tile_expert, n_used):
    rows = xs.shape[0]
    tm = MOE_TM
    by_expert = lambda r, c: pl.BlockSpec((None, None, r, c), lambda i, te, nu: (layer, te[i], 0, 0))
    by_tile = lambda c: pl.BlockSpec((tm, c), lambda i, te, nu: (i, 0))
    h = pl.pallas_call(
        _experts_up_body,
        grid_spec=pltpu.PrefetchScalarGridSpec(
            num_scalar_prefetch=2, grid=(rows // tm,),
            in_specs=[by_tile(D_MODEL), by_expert(D_MODEL, D_FF), by_expert(D_MODEL, D_FF)],
            out_specs=by_tile(D_FF),
            scratch_shapes=[pltpu.VMEM((D_MODEL, D_FF), BF16), pltpu.VMEM((D_MODEL, D_FF), BF16)]),
        out_shape=jax.ShapeDtypeStruct((rows, D_FF), BF16),
        compiler_params=_params(("arbitrary",), 56),
        name="moe_experts_up",
    )(tile_expert, n_used, xs, w1, w3)
    return pl.pallas_call(
        _experts_down_body,
        grid_spec=pltpu.PrefetchScalarGridSpec(
            num_scalar_prefetch=2, grid=(rows // tm,),
            in_specs=[by_tile(D_FF), by_expert(D_FF, D_MODEL)],
            out_specs=by_tile(D_MODEL),
            scratch_shapes=[pltpu.VMEM((D_FF, D_MODEL), BF16)]),
        out_shape=jax.ShapeDtypeStruct((rows, D_MODEL), F32),
        compiler_params=_params(("arbitrary",), 40),
        name="moe_experts_down",
    )(tile_expert, n_used, h, w2)


def _combine_body(dest_ref, next_ref, ys_ref, x_ref, w_ref, g_ref, b_ref, o_ref, ob_ref, buf_ref, sem, *, td):
    i = pl.program_id(0)
    slot = i % 2

    def gather(idx_ref, s):
        def issue(r, c):
            _row_copy(ys_ref, idx_ref[0, 0, r], buf_ref.at[s, 0], r, sem.at[s]).start(priority=0)
            _row_copy(ys_ref, idx_ref[0, 0, td + r], buf_ref.at[s, 1], r, sem.at[s]).start(priority=1)
            return c

        lax.fori_loop(0, td, issue, 0)

    @pl.when(i == 0)
    def _():
        gather(dest_ref, 0)

    @pl.when(i + 1 < pl.num_programs(0))
    def _():
        gather(next_ref, 1 - slot)

    for half in range(2):
        pltpu.make_async_copy(ys_ref.at[pl.ds(0, td), :], buf_ref.at[slot, half], sem.at[slot]).wait()
    w = w_ref[...]
    ff = w[:, 0:1] * buf_ref[slot, 0] + w[:, 1:2] * buf_ref[slot, 1]
    out = _standardize(ALPHA * x_ref[...] + ff) * g_ref[...] + b_ref[...]
    o_ref[...] = out
    ob_ref[...] = out.astype(BF16)


def _combine(ys, dest, x, w, g, b):
    m = x.shape[0]
    td = MOE_TD
    row = pl.BlockSpec((td, D_MODEL), lambda i: (i, 0))
    vec = pl.BlockSpec((1, D_MODEL), lambda i: (0, 0))
    last = m // td - 1
    return pl.pallas_call(
        functools.partial(_combine_body, td=td),
        grid=(m // td,),
        in_specs=[pl.BlockSpec((1, 1, 2 * td), lambda i: (i, 0, 0), memory_space=pltpu.SMEM),
                  pl.BlockSpec((1, 1, 2 * td), lambda i: (jnp.minimum(i + 1, last), 0, 0), memory_space=pltpu.SMEM),
                  pl.BlockSpec(memory_space=pl.ANY), row,
                  pl.BlockSpec((td, 8), lambda i: (i, 0)), vec, vec],
        out_specs=[row, row],
        out_shape=[jax.ShapeDtypeStruct((m, D_MODEL), F32), jax.ShapeDtypeStruct((m, D_MODEL), BF16)],
        scratch_shapes=[pltpu.VMEM((2, 2, td, D_MODEL), F32), pltpu.SemaphoreType.DMA((2,))],
        compiler_params=_params(("arbitrary",), 40),
        name="moe_combine",
    )(dest, dest, ys, x, w, g.reshape(1, -1), b.reshape(1, -1))


def _moe(x, xb, rw_t, rb, w1, w3, w2, layer, ln_g, ln_b):
    del xb
    m = x.shape[0]
    tm, td = MOE_TM, MOE_TD
    rows = 2 * m + N_EXPERTS * tm
    ro, cnt = _router(x, rw_t, rb)
    e0 = ro[0].astype(jnp.int32)
    e1 = ro[1].astype(jnp.int32)
    counts = cnt[:, 0].astype(jnp.int32).reshape(EXP_PER_GROUP, N_GROUPS).T.reshape(N_EXPERTS)
    padded = (counts + tm - 1) // tm * tm
    ends = jnp.cumsum(padded)
    offs = ends - padded
    d0 = offs[e0] + ro[4].astype(jnp.int32)
    d1 = offs[e1] + ro[5].astype(jnp.int32)
    dest = jnp.concatenate([d0.reshape(m // td, 1, td), d1.reshape(m // td, 1, td)], axis=-1)
    tile_start = jnp.arange(rows // tm, dtype=jnp.int32) * tm
    tile_expert = jnp.minimum(jnp.sum(tile_start[:, None] >= ends[None, :], axis=1), N_EXPERTS - 1).astype(jnp.int32)
    n_used = (ends[-1:] // tm).astype(jnp.int32)
    xs = _dispatch(x, dest, rows)
    ys = _experts(xs, w1, w3, w2, layer, tile_expert, n_used)
    wcol = jnp.pad(ro[2:4].T, ((0, 0), (0, 6)))
    return _combine(ys, dest, x, wcol, ln_g, ln_b)


def _rope_tables(pos):
    half = N_HD // 2
    inv = ROPE_THETA ** (-jnp.arange(half, dtype=F32) / half)
    ang = pos.astype(F32)[:, None] * inv[None, :]
    cos, sin = jnp.cos(ang), jnp.sin(ang)
    return jnp.concatenate([cos, cos], -1), jnp.concatenate([-sin, sin], -1)


def _overlap_t(T, ncmp_pad):
    n_cmp = (T - CMP_BLOCK) // CMP_STRIDE + 1
    n_sel = T // SEL_BLOCK
    cs = np.arange(n_cmp) * CMP_STRIDE
    ss = np.arange(n_sel) * SEL_BLOCK
    ov = ((cs[:, None] <= ss[None, :] + SEL_BLOCK - 1) & (cs[:, None] + CMP_BLOCK - 1 >= ss[None, :])).astype(np.float32)
    out = np.zeros((n_sel, ncmp_pad), np.float32)
    out[:, :n_cmp] = ov.T
    return jnp.asarray(out)


def _proj_plan():
    (mq, mk, mv, mo, mi, mf, nq, nkc, nvc, nks, nvs, nkw, nvw, ng, gm, end) = IN_OFFS
    seg = lambda a, b: list(range(a, b, PROJ_TN))
    p1 = seg(mq, mi) + seg(nvs, nkw) + seg(nvw, ng) + seg(gm, end)
    p2 = seg(nq, nkc) + seg(nks, nvs) + seg(nkw, nvw)
    pc = seg(nkc, nks)
    s1 = np.ones(len(p1) * PROJ_TN, np.float32)
    s1[:M_QK_W] = M_DQK ** -0.5
    s2 = np.ones(len(p2) * PROJ_TN, np.float32)
    s2[:N_Q_W] = N_HD ** -0.5
    return (p1, s1), (p2, s2), (pc, np.ones(len(pc) * PROJ_TN, np.float32))


def _columns(offs):
    return np.concatenate([np.arange(o, o + PROJ_TN) for o in offs])


def kernel(x, w_in, b_in, m_norm_w, cmp_pos_k, cmp_w1_k, cmp_b1_k, cmp_w2_k, cmp_b2_k, cmp_pos_v, cmp_w1_v,
           cmp_b1_v, cmp_w2_v, cmp_b2_v, w_branch_m, w_branch_n, w_out, ln1_g, ln1_b, router_w, router_b,
           exp_w1, exp_w3, exp_w2, ln2_g, ln2_b):
    B, T, D = x.shape
    m = B * T
    nb = T // CMP_STRIDE
    cos_t, sin_t = _rope_tables(jnp.arange(T))
    cos_c, sin_c = _rope_tables(jnp.arange(nb) * CMP_STRIDE + CMP_BLOCK - 1)
    ovt = _overlap_t(T, nb)
    perm = (np.arange(N_GROUPS)[None, :] * EXP_PER_GROUP + np.arange(EXP_PER_GROUP)[:, None]).reshape(-1)
    rw_t = router_w.T[perm]
    rb = jnp.broadcast_to(router_b[perm][:, None], (N_EXPERTS, LANES))

    w_t = jnp.swapaxes(w_in, 1, 2)
    (mi, nq, ng, gm) = (IN_OFFS[4], IN_OFFS[6], IN_OFFS[13], IN_OFFS[14])
    n_gate = (nq - mi) + (gm - ng)
    plans = _proj_plan()
    xf = x.reshape(m, D)
    xb = xf.astype(BF16)
    for l in range(DEPTH):
        (o1, s1), (o2, s2), (oc, sc) = plans
        bias = lambda offs, s: b_in[l][_columns(offs)] * s
        p1 = _proj(xb, w_t, l, o1, jnp.asarray(s1), bias(o1, s1), BF16, PROJ_TN)
        p2 = _proj(xb, w_t, l, o2, jnp.asarray(s2), bias(o2, s2), BF16, PROJ_TN, rope=(cos_t, sin_t), seq=T)
        pc = _proj(xb, w_t, l, oc, jnp.asarray(sc), bias(oc, sc), F32, PROJ_TN)
        w_gate = jnp.concatenate([w_t[l, mi:nq], w_t[l, ng:gm], jnp.zeros((LANES - n_gate, D), F32)])[None]
        b_gate = jnp.concatenate([b_in[l, mi:nq], b_in[l, ng:gm], jnp.zeros((LANES - n_gate,), F32)])
        pg = _proj(xb, w_gate, 0, [0], jnp.ones((LANES,), F32), b_gate, F32, LANES).reshape(B, T, LANES)
        gt_m = pg[:, :, :nq - mi].transpose(0, 2, 1)
        gt_n = jnp.pad(pg[:, :, nq - mi:n_gate].reshape(B, T, N_KV, 3 * N_HPG).transpose(0, 2, 3, 1),
                       ((0, 0), (0, 0), (0, 16 - 3 * N_HPG), (0, 0)))
        h_m = _mlstm(p1, pg, gt_m, m_norm_w[l], B, T).reshape(m, M_V_W)

        def blocks16(which):
            a = pc.reshape(B, nb, CMP_STRIDE, 2, N_KV, N_HD)[:, :, :, which]
            return a.transpose(0, 3, 1, 2, 4).reshape(B, N_KV, nb, CMP_STRIDE * N_HD)

        k_cmp = _compress(blocks16(0), cmp_pos_k[l], cmp_w1_k[l], cmp_b1_k[l], cmp_w2_k[l], cmp_b2_k[l],
                          cos_c, sin_c, True)
        v_cmp = _compress(blocks16(1), cmp_pos_v[l], cmp_w1_v[l], cmp_b1_v[l], cmp_w2_v[l], cmp_b2_v[l],
                          cos_c, sin_c, False)
        h_n = _nsa(p1, p2, gt_n, k_cmp, v_cmp, ovt, B, T).reshape(m, N_Q_W)
        merged = _merge(h_m, h_n, w_branch_m[l].astype(BF16), w_branch_n[l].astype(BF16), p1)
        xf, xb = _out_ln(merged, w_out[l].astype(BF16), xf, ln1_g[l], ln1_b[l])
        xf, xb = _moe(xf, xb, rw_t, rb, exp_w1, exp_w3, exp_w2, l, ln2_g[l], ln2_b[l])
    return xf.reshape(B, T, D)
```

```python
import functools

import numpy as np
import jax
import jax.numpy as jnp
from jax import lax
from jax.experimental import pallas as pl
from jax.experimental.pallas import tpu as pltpu

F32 = jnp.float32
BF16 = jnp.bfloat16
HIGHEST = lax.Precision.HIGHEST

D_MODEL = 2048
DEPTH = 2
M_HEADS = 4
M_DQK = 256
M_DV = 512
N_KV = 4
N_HPG = 4
N_HEADS = N_KV * N_HPG
N_HD = 128
CMP_BLOCK = 32
CMP_STRIDE = 16
SEL_BLOCK = 64
SEL_TOPK = 16
WINDOW = 512
ROPE_THETA = 10000.0
N_EXPERTS = 32
N_GROUPS = 8
EXP_PER_GROUP = N_EXPERTS // N_GROUPS
D_FF = 1024
ALPHA = (2 * DEPTH) ** 0.25
EPS = 1e-5
NEG = -1e30

M_QK_W = M_HEADS * M_DQK
M_V_W = M_HEADS * M_DV
N_Q_W = N_HEADS * N_HD
N_KV_W = N_KV * N_HD
IN_SIZES = (M_QK_W, M_QK_W, M_V_W, M_V_W, M_HEADS, M_HEADS, N_Q_W, N_KV_W, N_KV_W, N_KV_W, N_KV_W,
            N_KV_W, N_KV_W, 3 * N_HEADS, 2 * D_MODEL)
IN_OFFS = tuple(int(v) for v in np.cumsum((0,) + IN_SIZES))

LANES = 128
MLSTM_CHUNK = 256
NSA_TQ = 256
NSA_GROUP = 4
NSA_ONES = 16
MM_TM = 512
MM_TN = 1024
MOE_TM = 256
MOE_TD = 256
ROUTER_TM = 512

P1_MQ, P1_MK, P1_MV, P1_MO, P1_NVS, P1_NVW, P1_GM = 0, 1024, 2048, 4096, 6144, 6656, 7168
P1_W = 7168 + 2 * D_MODEL
P2_NQ, P2_NKS, P2_NKW = 0, 2048, 2560
P2_W = 3072


def _params(sem, vmem_mb):
    return pltpu.CompilerParams(dimension_semantics=sem, vmem_limit_bytes=vmem_mb * 1024 * 1024)


def _nt(a, b, **kw):
    return lax.dot_general(a, b, (((1,), (1,)), ((), ())), preferred_element_type=F32, **kw)


def _tn(a, b):
    return lax.dot_general(a, b, (((0,), (0,)), ((), ())), preferred_element_type=F32)


def _standardize(x):
    mu = jnp.mean(x, -1, keepdims=True)
    xc = x - mu
    var = jnp.mean(xc * xc, -1, keepdims=True)
    return xc * lax.rsqrt(var + EPS)


def _mm_body(x_ref, w_ref, b_ref, *rest, rope, tn):
    acc = jnp.dot(x_ref[...], w_ref[...], preferred_element_type=F32) + b_ref[...]
    if rope:
        cos_ref, sin_ref, o_ref = rest
        cos = cos_ref[...]
        sin = sin_ref[...]
        for c in range(tn // N_HD):
            ch = acc[:, c * N_HD:(c + 1) * N_HD]
            o_ref[:, c * N_HD:(c + 1) * N_HD] = (ch * cos + pltpu.roll(ch, N_HD // 2, 1) * sin).astype(o_ref.dtype)
    else:
        (o_ref,) = rest
        o_ref[...] = acc.astype(o_ref.dtype)


def _matmul(x, w, b, out_dtype, tn, rope=None, seq=None):
    m, k = x.shape
    n = w.shape[1]
    tm = MM_TM
    grid = (n // tn, m // tm)
    in_specs = [pl.BlockSpec((tm, k), lambda j, i: (i, 0)),
                pl.BlockSpec((k, tn), lambda j, i: (0, j)),
                pl.BlockSpec((1, tn), lambda j, i: (0, j))]
    args = [x, w, b.reshape(1, n)]
    if rope is not None:
        nt = seq // tm
        in_specs += [pl.BlockSpec((tm, N_HD), lambda j, i: (i % nt, 0))] * 2
        args += list(rope)
    return pl.pallas_call(
        functools.partial(_mm_body, rope=rope is not None, tn=tn),
        grid=grid, in_specs=in_specs,
        out_specs=pl.BlockSpec((tm, tn), lambda j, i: (i, j)),
        out_shape=jax.ShapeDtypeStruct((m, n), out_dtype),
        compiler_params=_params(("parallel", "parallel"), 48),
        name="proj_matmul",
    )(*args)


def _log_sigmoid(x):
    return jnp.minimum(x, 0.0) - jnp.log1p(jnp.exp(-jnp.abs(x)))


def _mlstm_body(q_ref, k_ref, v_ref, og_ref, g_ref, gt_ref, nw_ref, out_ref, c_ref, n_ref, m_ref, *, L):
    @pl.when(pl.program_id(1) == 0)
    def _():
        c_ref[...] = jnp.zeros_like(c_ref)
        n_ref[...] = jnp.zeros_like(n_ref)
        m_ref[...] = jnp.zeros_like(m_ref)

    g = g_ref[...]
    gt = gt_ref[...]
    row = lax.broadcasted_iota(jnp.int32, (L, L), 0)
    col = lax.broadcasted_iota(jnp.int32, (L, L), 1)
    causal = row >= col
    b_all = jnp.dot(causal.astype(F32), _log_sigmoid(g), precision=HIGHEST, preferred_element_type=F32)
    bt_all = jnp.dot(_log_sigmoid(gt), (row <= col).astype(F32), precision=HIGHEST, preferred_element_type=F32)
    for h in range(M_HEADS):
        b_col = b_all[:, M_HEADS + h:M_HEADS + h + 1]
        li_col = g[:, h:h + 1]
        b_row = bt_all[M_HEADS + h:M_HEADS + h + 1, :]
        li_row = gt[h:h + 1, :]
        m_prev = m_ref[h][:, 0:1]
        dmat = jnp.where(causal, b_col - b_row + li_row, -jnp.inf)
        inter = b_col + m_prev
        m_t = jnp.maximum(inter, jnp.max(dmat, axis=1, keepdims=True))
        w_inter = jnp.exp(inter - m_t)
        q = q_ref[:, h * M_DQK:(h + 1) * M_DQK]
        k = k_ref[:, h * M_DQK:(h + 1) * M_DQK]
        v = v_ref[:, h * M_DV:(h + 1) * M_DV]
        s = _nt(q, k) * jnp.exp(dmat - m_t)
        ct = c_ref[h]
        n_row = n_ref[h]
        num = jnp.dot(s.astype(BF16), v, preferred_element_type=F32) + w_inter * jnp.dot(
            q, ct.astype(BF16), preferred_element_type=F32)
        qn = jnp.sum(q.astype(F32) * n_row, axis=1, keepdims=True)
        den = jnp.sum(s, axis=1, keepdims=True) + w_inter * qn
        hh = num / jnp.maximum(jnp.abs(den), jnp.exp(-m_t))
        hn = _standardize(hh) * nw_ref[:, h * M_DV:(h + 1) * M_DV]
        og = og_ref[:, h * M_DV:(h + 1) * M_DV].astype(F32)
        out_ref[:, h * M_DV:(h + 1) * M_DV] = (hn * jax.nn.sigmoid(og)).astype(out_ref.dtype)
        b_last = b_col[L - 1:L, :]
        g_col = b_last - b_col + li_col
        g_row = b_last - b_row + li_row
        m_new = jnp.maximum(b_last + m_prev, jnp.max(g_row, axis=1, keepdims=True))
        decay = jnp.exp(b_last + m_prev - m_new)
        ws_col = jnp.exp(g_col - m_new)
        kf = k.astype(F32)
        vw = (v.astype(F32) * ws_col).astype(BF16)
        c_ref[h] = decay * ct + _tn(k, vw)
        n_ref[h] = decay * n_row + jnp.sum(kf * ws_col, axis=0, keepdims=True)
        m_ref[h] = jnp.broadcast_to(m_new, (1, LANES))


def _mlstm(p1, p3, gt, norm_w, B, T):
    L = MLSTM_CHUNK
    p1 = p1.reshape(B, T, P1_W)
    return pl.pallas_call(
        functools.partial(_mlstm_body, L=L),
        grid=(B, T // L),
        in_specs=[pl.BlockSpec((None, L, M_QK_W), lambda b, c: (b, c, P1_MQ // M_QK_W)),
                  pl.BlockSpec((None, L, M_QK_W), lambda b, c: (b, c, P1_MK // M_QK_W)),
                  pl.BlockSpec((None, L, M_V_W), lambda b, c: (b, c, P1_MV // M_V_W)),
                  pl.BlockSpec((None, L, M_V_W), lambda b, c: (b, c, P1_MO // M_V_W)),
                  pl.BlockSpec((None, L, LANES), lambda b, c: (b, c, 0)),
                  pl.BlockSpec((None, 8, L), lambda b, c: (b, 0, c)),
                  pl.BlockSpec((1, M_V_W), lambda b, c: (0, 0))],
        out_specs=pl.BlockSpec((None, L, M_V_W), lambda b, c: (b, c, 0)),
        out_shape=jax.ShapeDtypeStruct((B, T, M_V_W), BF16),
        scratch_shapes=[pltpu.VMEM((M_HEADS, M_DQK, M_DV), F32),
                        pltpu.VMEM((M_HEADS, 1, M_DQK), F32),
                        pltpu.VMEM((M_HEADS, 1, LANES), F32)],
        compiler_params=_params(("parallel", "arbitrary"), 48),
        name="mlstm",
    )(p1, p1, p1, p1, p3, gt, norm_w.reshape(1, M_V_W))


def _compress_body(x_ref, pos_ref, w1_ref, b1_ref, w2_ref, b2_ref, cos_ref, sin_ref, o_ref, *, rope, nb):
    x = x_ref[...]
    half = CMP_STRIDE * N_HD
    lo = jnp.dot((x + pos_ref[:, :half]).astype(BF16), w1_ref[:half, :], preferred_element_type=F32)
    hi = jnp.dot((x + pos_ref[:, half:]).astype(BF16), w1_ref[half:, :], preferred_element_type=F32)
    pre = lo + pltpu.roll(hi, nb - 1, 0) + b1_ref[...]
    h = jax.nn.gelu(pre)
    y = jnp.dot(h.astype(BF16), w2_ref[...], preferred_element_type=F32) + b2_ref[...]
    if rope:
        y = y * cos_ref[...] + pltpu.roll(y, N_HD // 2, 1) * sin_ref[...]
    keep = lax.broadcasted_iota(jnp.int32, (nb, N_HD), 0) < nb - 1
    o_ref[...] = jnp.where(keep, y, 0.0).astype(o_ref.dtype)


def _compress(x2, pos, w1, b1, w2, b2, cos, sin, rope):
    B, G, nb, _ = x2.shape
    full = lambda shape: pl.BlockSpec(shape, lambda b, g: (0,) * len(shape))
    return pl.pallas_call(
        functools.partial(_compress_body, rope=rope, nb=nb),
        grid=(B, G),
        in_specs=[pl.BlockSpec((None, None, nb, CMP_STRIDE * N_HD), lambda b, g: (b, g, 0, 0)),
                  full((1, CMP_BLOCK * N_HD)), full((CMP_BLOCK * N_HD, N_HD)), full((1, N_HD)),
                  full((N_HD, N_HD)), full((1, N_HD)), full((nb, N_HD)), full((nb, N_HD))],
        out_specs=pl.BlockSpec((None, None, nb, N_HD), lambda b, g: (b, g, 0, 0)),
        out_shape=jax.ShapeDtypeStruct((B, G, nb, N_HD), BF16),
        compiler_params=_params(("parallel", "parallel"), 32),
        name="nsa_compress",
    )(x2, pos.reshape(1, -1), w1.reshape(CMP_BLOCK * N_HD, N_HD).astype(BF16), b1.reshape(1, N_HD),
      w2.astype(BF16), b2.reshape(1, N_HD), cos, sin)


def _nsa_body(q_ref, ks_ref, vst_ref, kw_ref, vwt_ref, kc_ref, vct_ref, gt_ref, ovt_ref, o_ref,
              acc_ref, m_ref, sel_ref, out_ref, s_ref, p_ref, *, tq, ncmp):
    qi = pl.program_id(2)
    t0 = qi * tq
    nsel = ovt_ref.shape[0]
    gates = jax.nn.sigmoid(gt_ref[...])
    q_heads = [q_ref[:, h * N_HD:(h + 1) * N_HD] for h in range(N_HPG)]

    kc = kc_ref[...]
    vct = vct_ref[...]
    n_idx = lax.broadcasted_iota(jnp.int32, (ncmp, tq), 0)
    t_cmp = t0 + lax.broadcasted_iota(jnp.int32, (ncmp, tq), 1)
    cmask = (n_idx * CMP_STRIDE + (CMP_BLOCK - 1) <= t_cmp) & (n_idx < ncmp - 1)
    psum = jnp.zeros((ncmp, tq), F32)
    for h in range(N_HPG):
        s_ref[0, h, :ncmp, :] = _nt(kc, q_heads[h])
    for h in range(N_HPG):
        s = jnp.where(cmask, s_ref[0, h, :ncmp, :], NEG)
        e = jnp.where(cmask, jnp.exp(s - jnp.max(s, axis=0, keepdims=True)), 0.0)
        p = e * (1.0 / jnp.maximum(jnp.sum(e, axis=0, keepdims=True), 1e-30))
        psum = psum + p
        p_ref[0, h, :ncmp, :] = p.astype(BF16)
    for h in range(N_HPG):
        out_ref[h] = gates[3 * h:3 * h + 1, :] * jnp.dot(vct, p_ref[0, h, :ncmp, :], preferred_element_type=F32)

    imp = jnp.dot(ovt_ref[...], psum, precision=HIGHEST, preferred_element_type=F32)
    j_idx = lax.broadcasted_iota(jnp.int32, (nsel, tq), 0)
    t_row = t0 + lax.broadcasted_iota(jnp.int32, (nsel, tq), 1)
    cur = lax.shift_right_logical(t_row, 6)
    forced = (j_idx == 0) | (j_idx == cur) | (j_idx == cur - 1)
    score = jnp.where(forced, jnp.inf, jnp.where(j_idx * SEL_BLOCK <= t_row, imp, -jnp.inf))

    def pick(_, carry):
        score, sel = carry
        mx = jnp.max(score, axis=0, keepdims=True)
        idx = jnp.min(jnp.where(score == mx, j_idx, nsel), axis=0, keepdims=True)
        hit = j_idx == idx
        return jnp.where(hit, -jnp.inf, score), jnp.where(hit, 1.0, sel)

    _, sel = lax.fori_loop(0, min(SEL_TOPK, nsel), pick, (score, jnp.zeros((nsel, tq), F32)), unroll=True)
    sel_ref[...] = sel

    def flash_init():
        acc_ref[...] = jnp.zeros_like(acc_ref)
        m_ref[...] = jnp.full(m_ref.shape, NEG, F32)

    def flash_tiles(k_ref, vt_ref, tiles):
        vts = []
        for n, (j, _) in enumerate(tiles):
            start = pl.multiple_of(j * tq, tq)
            k = k_ref[pl.ds(start, tq), :]
            vts.append(vt_ref[:, pl.ds(start, tq)])
            for h in range(N_HPG):
                s_ref[n, h] = _nt(k, q_heads[h])
        for n, (_, mask) in enumerate(tiles):
            alphas = []
            for h in range(N_HPG):
                s = s_ref[n, h]
                if mask is not None:
                    s = jnp.where(mask, s, NEG)
                m_old = m_ref[h]
                m_new = jnp.maximum(m_old, jnp.max(s, axis=0, keepdims=True))
                p_ref[n, h] = jnp.exp((s - m_new).astype(BF16))
                alpha = jnp.exp(m_old - m_new)
                m_ref[h] = m_new
                alphas.append(alpha)
            for h in range(N_HPG):
                acc_ref[h] = alphas[h] * acc_ref[h] + jnp.dot(vts[n], p_ref[n, h], preferred_element_type=F32)

    def flash_add(branch):
        for h in range(N_HPG):
            scale = gates[3 * h + branch:3 * h + branch + 1, :] / acc_ref[h, N_HD:N_HD + 1, :]
            out_ref[h] = out_ref[h] + scale * acc_ref[h, :N_HD, :]

    kpos = lax.broadcasted_iota(jnp.int32, (tq, tq), 0)
    qpos = lax.broadcasted_iota(jnp.int32, (tq, tq), 1)
    blk_per_tile = tq // SEL_BLOCK

    def sel_mask(j):
        rows = [jnp.broadcast_to(sel_ref[pl.ds(j * blk_per_tile + b, 1), :], (SEL_BLOCK, tq))
                for b in range(blk_per_tile)]
        return jnp.concatenate(rows, axis=0) > 0.5

    flash_init()

    group = s_ref.shape[0]

    def sel_group(p, carry):
        flash_tiles(ks_ref, vst_ref, [(group * p + n, sel_mask(group * p + n)) for n in range(group)])
        return carry

    lax.fori_loop(0, qi // group, sel_group, 0)
    causal = kpos <= qpos
    for rem in range(group):
        @pl.when(qi % group == rem)
        def _():
            tiles = [(qi - rem + n, sel_mask(qi - rem + n)) for n in range(rem)]
            flash_tiles(ks_ref, vst_ref, tiles + [(qi, sel_mask(qi) & causal)])

    flash_add(1)

    flash_init()
    n_back = WINDOW // tq
    win_tiles = [(qi, causal)] + [(qi - back, None) for back in range(1, n_back)] + [(qi - n_back, kpos > qpos)]
    for n_tiles in range(1, n_back + 2):
        last = n_tiles == n_back + 1

        @pl.when((qi >= n_tiles - 1) if last else (qi == n_tiles - 1))
        def _():
            flash_tiles(kw_ref, vwt_ref, win_tiles[:n_tiles])
    flash_add(2)

    for h in range(N_HPG):
        o_ref[:, h * N_HD:(h + 1) * N_HD] = out_ref[h].T.astype(o_ref.dtype)


def _nsa(p1, p2, gt, k_cmp, v_cmp, ovt, B, T):
    tq = NSA_TQ
    ncmp = k_cmp.shape[2]
    p1 = p1.reshape(B, T, P1_W)
    p2 = p2.reshape(B, T, P2_W)
    hw = N_HPG * N_HD
    assert WINDOW % tq == 0 and tq % SEL_BLOCK == 0 and ncmp <= tq and NSA_GROUP >= WINDOW // tq + 1
    ones = jnp.ones((B, N_KV, NSA_ONES, T), BF16)

    def values_t(off):
        vt = p1[:, :, off:off + N_KV_W].transpose(0, 2, 1).reshape(B, N_KV, N_HD, T)
        return jnp.concatenate([vt, ones], axis=2)

    vst = values_t(P1_NVS)
    vwt = values_t(P1_NVW)
    vct = v_cmp.transpose(0, 1, 3, 2)
    k_spec = lambda off: pl.BlockSpec((None, T, N_HD), lambda b, g, i: (b, 0, off // N_HD + g))
    vt_spec = pl.BlockSpec((None, None, N_HD + NSA_ONES, T), lambda b, g, i: (b, g, 0, 0))
    return pl.pallas_call(
        functools.partial(_nsa_body, tq=tq, ncmp=ncmp),
        grid=(B, N_KV, T // tq),
        in_specs=[pl.BlockSpec((None, tq, hw), lambda b, g, i: (b, i, P2_NQ // hw + g)),
                  k_spec(P2_NKS), vt_spec, k_spec(P2_NKW), vt_spec,
                  pl.BlockSpec((None, None, ncmp, N_HD), lambda b, g, i: (b, g, 0, 0)),
                  pl.BlockSpec((None, None, N_HD, ncmp), lambda b, g, i: (b, g, 0, 0)),
                  pl.BlockSpec((None, None, 16, tq), lambda b, g, i: (b, g, 0, i)),
                  pl.BlockSpec(ovt.shape, lambda b, g, i: (0, 0))],
        out_specs=pl.BlockSpec((None, tq, hw), lambda b, g, i: (b, i, g)),
        out_shape=jax.ShapeDtypeStruct((B, T, N_Q_W), BF16),
        scratch_shapes=[pltpu.VMEM((N_HPG, N_HD + NSA_ONES, tq), F32),
                        pltpu.VMEM((N_HPG, 1, tq), F32), pltpu.VMEM((ovt.shape[0], tq), F32),
                        pltpu.VMEM((N_HPG, N_HD, tq), F32),
                        pltpu.VMEM((NSA_GROUP, N_HPG, tq, tq), F32),
                        pltpu.VMEM((NSA_GROUP, N_HPG, tq, tq), BF16)],
        compiler_params=_params(("parallel", "parallel", "arbitrary"), 48),
        name="nsa_attention",
    )(p2, p2, vst, p2, vwt, k_cmp, vct, gt, ovt)


def _merge_body(hm_ref, hn_ref, wm_ref, wn_ref, gm_ref, gn_ref, o_ref):
    ym = jnp.dot(hm_ref[...], wm_ref[...], preferred_element_type=F32)
    yn = jnp.dot(hn_ref[...], wn_ref[...], preferred_element_type=F32)
    gm = jax.nn.sigmoid(gm_ref[...].astype(F32))
    gn = jax.nn.sigmoid(gn_ref[...].astype(F32))
    o_ref[...] = (gm * ym + gn * yn).astype(o_ref.dtype)


def _merge(hm, hn, wm, wn, p1):
    m = hm.shape[0]
    tm, tn = MM_TM, 512
    return pl.pallas_call(
        _merge_body,
        grid=(D_MODEL // tn, m // tm),
        in_specs=[pl.BlockSpec((tm, M_V_W), lambda j, i: (i, 0)),
                  pl.BlockSpec((tm, N_Q_W), lambda j, i: (i, 0)),
                  pl.BlockSpec((M_V_W, tn), lambda j, i: (0, j)),
                  pl.BlockSpec((N_Q_W, tn), lambda j, i: (0, j)),
                  pl.BlockSpec((tm, tn), lambda j, i: (i, P1_GM // tn + j)),
                  pl.BlockSpec((tm, tn), lambda j, i: (i, (P1_GM + D_MODEL) // tn + j))],
        out_specs=pl.BlockSpec((tm, tn), lambda j, i: (i, j)),
        out_shape=jax.ShapeDtypeStruct((m, D_MODEL), BF16),
        compiler_params=_params(("parallel", "parallel"), 48),
        name="branch_merge",
    )(hm, hn, wm, wn, p1, p1)


def _out_ln_body(y_ref, w_ref, x_ref, g_ref, b_ref, o_ref, ob_ref):
    mix = jnp.dot(y_ref[...], w_ref[...], preferred_element_type=F32)
    out = _standardize(ALPHA * x_ref[...] + mix) * g_ref[...] + b_ref[...]
    o_ref[...] = out
    ob_ref[...] = out.astype(BF16)


def _out_ln(y, w, x, g, b):
    m = y.shape[0]
    tm = MM_TM
    row = pl.BlockSpec((tm, D_MODEL), lambda i: (i, 0))
    vec = pl.BlockSpec((1, D_MODEL), lambda i: (0, 0))
    return pl.pallas_call(
        _out_ln_body,
        grid=(m // tm,),
        in_specs=[row, pl.BlockSpec((D_MODEL, D_MODEL), lambda i: (0, 0)), row, vec, vec],
        out_specs=[row, row],
        out_shape=[jax.ShapeDtypeStruct((m, D_MODEL), F32), jax.ShapeDtypeStruct((m, D_MODEL), BF16)],
        compiler_params=_params(("parallel",), 48),
        name="out_proj_layernorm",
    )(y, w, x, g.reshape(1, -1), b.reshape(1, -1))


def _first_of4(vals, target):
    return jnp.where(vals[0] == target, 0.0, jnp.where(vals[1] == target, 1.0, jnp.where(vals[2] == target, 2.0, 3.0)))


def _select4(idx, vals):
    return jnp.where(idx == 0.0, vals[0], jnp.where(idx == 1.0, vals[1], jnp.where(idx == 2.0, vals[2], vals[3])))


def _router_body(x_ref, rw_ref, rb_ref, o_ref, cnt_ref, carry_ref, *, tm):
    @pl.when(pl.program_id(0) == 0)
    def _():
        carry_ref[...] = jnp.zeros_like(carry_ref)

    logits = _nt(rw_ref[...], x_ref[...], precision=HIGHEST)
    aff = jax.nn.sigmoid(logits)
    biased = aff + rb_ref[:, 0:1]
    a = [biased[i * N_GROUPS:(i + 1) * N_GROUPS, :] for i in range(EXP_PER_GROUP)]
    af = [aff[i * N_GROUPS:(i + 1) * N_GROUPS, :] for i in range(EXP_PER_GROUP)]
    m1 = jnp.maximum(jnp.maximum(a[0], a[1]), jnp.maximum(a[2], a[3]))
    i1 = _first_of4(a, m1)
    rest = [jnp.where(i1 == float(i), -jnp.inf, a[i]) for i in range(EXP_PER_GROUP)]
    m2 = jnp.maximum(jnp.maximum(rest[0], rest[1]), jnp.maximum(rest[2], rest[3]))
    i2 = _first_of4(rest, m2)
    gscore = m1 + m2
    g_iota = lax.broadcasted_iota(jnp.int32, (N_GROUPS, tm), 0).astype(F32)
    g_idx = jnp.min(jnp.where(gscore == jnp.max(gscore, axis=0, keepdims=True), g_iota, float(N_GROUPS)),
                    axis=0, keepdims=True)
    in_g = g_iota == g_idx
    take = lambda v: jnp.sum(jnp.where(in_g, v, 0.0), axis=0, keepdims=True)
    s0 = take(i1)
    s1 = take(i2)
    w0 = take(_select4(i1, af))
    w1 = take(_select4(i2, af))
    wsum = w0 + w1
    r0 = s0 * N_GROUPS + g_idx
    r1 = s1 * N_GROUPS + g_idx
    r_iota = lax.broadcasted_iota(jnp.int32, (N_EXPERTS, tm), 0).astype(F32)
    member = (r_iota == r0) | (r_iota == r1)
    tt = lax.broadcasted_iota(jnp.int32, (tm, tm), 0)
    tc = lax.broadcasted_iota(jnp.int32, (tm, tm), 1)
    before = jnp.dot(member.astype(BF16), (tt < tc).astype(BF16), preferred_element_type=F32)
    base = before + carry_ref[:, 0:1]
    rank0 = jnp.sum(jnp.where(r_iota == r0, base, 0.0), axis=0, keepdims=True)
    rank1 = jnp.sum(jnp.where(r_iota == r1, base, 0.0), axis=0, keepdims=True)
    new_carry = carry_ref[:, 0:1] + jnp.sum(member.astype(F32), axis=1, keepdims=True)
    carry_ref[...] = jnp.broadcast_to(new_carry, carry_ref.shape)
    cnt_ref[...] = jnp.broadcast_to(new_carry, cnt_ref.shape)
    o_ref[0:1, :] = g_idx * EXP_PER_GROUP + s0
    o_ref[1:2, :] = g_idx * EXP_PER_GROUP + s1
    o_ref[2:3, :] = w0 / wsum
    o_ref[3:4, :] = w1 / wsum
    o_ref[4:5, :] = rank0
    o_ref[5:6, :] = rank1
    o_ref[6:8, :] = jnp.zeros((2, tm), F32)


def _router(x, rw_t, rb):
    m = x.shape[0]
    tm = ROUTER_TM
    return pl.pallas_call(
        functools.partial(_router_body, tm=tm),
        grid=(m // tm,),
        in_specs=[pl.BlockSpec((tm, D_MODEL), lambda i: (i, 0)),
                  pl.BlockSpec((N_EXPERTS, D_MODEL), lambda i: (0, 0)),
                  pl.BlockSpec((N_EXPERTS, LANES), lambda i: (0, 0))],
        out_specs=[pl.BlockSpec((8, tm), lambda i: (0, i)),
                   pl.BlockSpec((N_EXPERTS, LANES), lambda i: (0, 0))],
        out_shape=[jax.ShapeDtypeStruct((8, m), F32), jax.ShapeDtypeStruct((N_EXPERTS, LANES), F32)],
        scratch_shapes=[pltpu.VMEM((N_EXPERTS, LANES), F32)],
        compiler_params=_params(("arbitrary",), 48),
        name="moe_router",
    )(x, rw_t, rb)


def _row_copy(src_ref, src_row, dst_ref, dst_row, sem):
    return pltpu.make_async_copy(src_ref.at[pl.ds(src_row, 1), :], dst_ref.at[pl.ds(dst_row, 1), :], sem)


def _dispatch_body(dest_ref, x_ref, init_ref, xs_ref, sem, *, td):
    del init_ref

    def issue(r, c):
        _row_copy(x_ref, r, xs_ref, dest_ref[0, 0, r], sem).start(priority=0)
        _row_copy(x_ref, r, xs_ref, dest_ref[0, 0, td + r], sem).start(priority=1)
        return c

    lax.fori_loop(0, td, issue, 0)
    for _ in range(2):
        pltpu.make_async_copy(x_ref, xs_ref.at[pl.ds(0, td), :], sem).wait()


def _dispatch(x, dest, init):
    rows = init.shape[0]
    m = x.shape[0]
    td = MOE_TD
    return pl.pallas_call(
        functools.partial(_dispatch_body, td=td),
        grid=(m // td,),
        in_specs=[pl.BlockSpec((1, 1, 2 * td), lambda i: (i, 0, 0), memory_space=pltpu.SMEM),
                  pl.BlockSpec((td, D_MODEL), lambda i: (i, 0)),
                  pl.BlockSpec(memory_space=pl.ANY)],
        out_specs=pl.BlockSpec(memory_space=pl.ANY),
        out_shape=jax.ShapeDtypeStruct((rows, D_MODEL), F32),
        scratch_shapes=[pltpu.SemaphoreType.DMA(())],
        input_output_aliases={2: 0},
        compiler_params=_params(("arbitrary",), 32),
        name="moe_dispatch",
    )(dest, x, init)


def _expert_changed(te_ref, i):
    return (i == 0) | (te_ref[i] != te_ref[jnp.maximum(i - 1, 0)])


def _experts_up_body(te_ref, nused_ref, x_ref, w1_ref, w3_ref, h_ref, w1b_ref, w3b_ref):
    i = pl.program_id(0)
    used = i < nused_ref[0]

    @pl.when(used & _expert_changed(te_ref, i))
    def _():
        w1b_ref[...] = w1_ref[...].astype(BF16)
        w3b_ref[...] = w3_ref[...].astype(BF16)

    @pl.when(used)
    def _():
        xb = x_ref[...].astype(BF16)
        a = jnp.dot(xb, w1b_ref[...], preferred_element_type=F32)
        b = jnp.dot(xb, w3b_ref[...], preferred_element_type=F32)
        h_ref[...] = (a * jax.nn.sigmoid(a) * b).astype(BF16)

    @pl.when(jnp.logical_not(used))
    def _():
        h_ref[...] = jnp.zeros_like(h_ref)


def _experts_down_body(te_ref, nused_ref, h_ref, w2_ref, o_ref, w2b_ref):
    i = pl.program_id(0)
    used = i < nused_ref[0]

    @pl.when(used & _expert_changed(te_ref, i))
    def _():
        w2b_ref[...] = w2_ref[...].astype(BF16)

    @pl.when(used)
    def _():
        o_ref[...] = jnp.dot(h_ref[...], w2b_ref[...], preferred_element_type=F32)

    @pl.when(jnp.logical_not(used))
    def _():
        o_ref[...] = jnp.zeros_like(o_ref)


def _experts(xs, w1, w3, w2, layer, tile_expert, n_used):
    rows = xs.shape[0]
    tm = MOE_TM
    by_expert = lambda r, c: pl.BlockSpec((None, None, r, c), lambda i, te, nu: (layer, te[i], 0, 0))
    by_tile = lambda c: pl.BlockSpec((tm, c), lambda i, te, nu: (i, 0))
    h = pl.pallas_call(
        _experts_up_body,
        grid_spec=pltpu.PrefetchScalarGridSpec(
            num_scalar_prefetch=2, grid=(rows // tm,),
            in_specs=[by_tile(D_MODEL), by_expert(D_MODEL, D_FF), by_expert(D_MODEL, D_FF)],
            out_specs=by_tile(D_FF),
            scratch_shapes=[pltpu.VMEM((D_MODEL, D_FF), BF16), pltpu.VMEM((D_MODEL, D_FF), BF16)]),
        out_shape=jax.ShapeDtypeStruct((rows, D_FF), BF16),
        compiler_params=_params(("arbitrary",), 56),
        name="moe_experts_up",
    )(tile_expert, n_used, xs, w1, w3)
    return pl.pallas_call(
        _experts_down_body,
        grid_spec=pltpu.PrefetchScalarGridSpec(
            num_scalar_prefetch=2, grid=(rows // tm,),
            in_specs=[by_tile(D_FF), by_expert(D_FF, D_MODEL)],
            out_specs=by_tile(D_MODEL),
            scratch_shapes=[pltpu.VMEM((D_FF, D_MODEL), BF16)]),
        out_shape=jax.ShapeDtypeStruct((rows, D_MODEL), F32),
        compiler_params=_params(("arbitrary",), 40),
        name="moe_experts_down",
    )(tile_expert, n_used, h, w2)


def _combine_body(dest_ref, next_ref, ys_ref, x_ref, w_ref, g_ref, b_ref, o_ref, ob_ref, buf_ref, sem, *, td):
    i = pl.program_id(0)
    slot = i % 2

    def gather(idx_ref, s):
        def issue(r, c):
            _row_copy(ys_ref, idx_ref[0, 0, r], buf_ref.at[s, 0], r, sem.at[s]).start(priority=0)
            _row_copy(ys_ref, idx_ref[0, 0, td + r], buf_ref.at[s, 1], r, sem.at[s]).start(priority=1)
            return c

        lax.fori_loop(0, td, issue, 0)

    @pl.when(i == 0)
    def _():
        gather(dest_ref, 0)

    @pl.when(i + 1 < pl.num_programs(0))
    def _():
        gather(next_ref, 1 - slot)

    for half in range(2):
        pltpu.make_async_copy(ys_ref.at[pl.ds(0, td), :], buf_ref.at[slot, half], sem.at[slot]).wait()
    w = w_ref[...]
    ff = w[:, 0:1] * buf_ref[slot, 0] + w[:, 1:2] * buf_ref[slot, 1]
    out = _standardize(ALPHA * x_ref[...] + ff) * g_ref[...] + b_ref[...]
    o_ref[...] = out
    ob_ref[...] = out.astype(BF16)


def _combine(ys, dest, x, w, g, b):
    m = x.shape[0]
    td = MOE_TD
    row = pl.BlockSpec((td, D_MODEL), lambda i: (i, 0))
    vec = pl.BlockSpec((1, D_MODEL), lambda i: (0, 0))
    last = m // td - 1
    return pl.pallas_call(
        functools.partial(_combine_body, td=td),
        grid=(m // td,),
        in_specs=[pl.BlockSpec((1, 1, 2 * td), lambda i: (i, 0, 0), memory_space=pltpu.SMEM),
                  pl.BlockSpec((1, 1, 2 * td), lambda i: (jnp.minimum(i + 1, last), 0, 0), memory_space=pltpu.SMEM),
                  pl.BlockSpec(memory_space=pl.ANY), row,
                  pl.BlockSpec((td, 8), lambda i: (i, 0)), vec, vec],
        out_specs=[row, row],
        out_shape=[jax.ShapeDtypeStruct((m, D_MODEL), F32), jax.ShapeDtypeStruct((m, D_MODEL), BF16)],
        scratch_shapes=[pltpu.VMEM((2, 2, td, D_MODEL), F32), pltpu.SemaphoreType.DMA((2,))],
        compiler_params=_params(("arbitrary",), 40),
        name="moe_combine",
    )(dest, dest, ys, x, w, g.reshape(1, -1), b.reshape(1, -1))


def _moe(x, sorted_buf, rw_t, rb, w1, w3, w2, layer, ln_g, ln_b):
    m = x.shape[0]
    tm, td = MOE_TM, MOE_TD
    rows = sorted_buf.shape[0]
    ro, cnt = _router(x, rw_t, rb)
    e0 = ro[0].astype(jnp.int32)
    e1 = ro[1].astype(jnp.int32)
    counts = cnt[:, 0].astype(jnp.int32).reshape(EXP_PER_GROUP, N_GROUPS).T.reshape(N_EXPERTS)
    padded = (counts + tm - 1) // tm * tm
    ends = jnp.cumsum(padded)
    offs = ends - padded
    d0 = offs[e0] + ro[4].astype(jnp.int32)
    d1 = offs[e1] + ro[5].astype(jnp.int32)
    dest = jnp.concatenate([d0.reshape(m // td, 1, td), d1.reshape(m // td, 1, td)], axis=-1)
    tile_start = jnp.arange(rows // tm, dtype=jnp.int32) * tm
    tile_expert = jnp.minimum(jnp.sum(tile_start[:, None] >= ends[None, :], axis=1), N_EXPERTS - 1).astype(jnp.int32)
    n_used = (ends[-1:] // tm).astype(jnp.int32)
    xs = _dispatch(x, dest, sorted_buf)
    ys = _experts(xs, w1, w3, w2, layer, tile_expert, n_used)
    wcol = jnp.pad(ro[2:4].T, ((0, 0), (0, 6)))
    out, out_bf16 = _combine(ys, dest, x, wcol, ln_g, ln_b)
    return out, out_bf16, xs


def _rope_tables(pos):
    half = N_HD // 2
    inv = ROPE_THETA ** (-jnp.arange(half, dtype=F32) / half)
    ang = pos.astype(F32)[:, None] * inv[None, :]
    cos, sin = jnp.cos(ang), jnp.sin(ang)
    return jnp.concatenate([cos, cos], -1), jnp.concatenate([-sin, sin], -1)


def _overlap_t(T, ncmp_pad):
    n_cmp = (T - CMP_BLOCK) // CMP_STRIDE + 1
    n_sel = T // SEL_BLOCK
    cs = np.arange(n_cmp) * CMP_STRIDE
    ss = np.arange(n_sel) * SEL_BLOCK
    ov = ((cs[:, None] <= ss[None, :] + SEL_BLOCK - 1) & (cs[:, None] + CMP_BLOCK - 1 >= ss[None, :])).astype(np.float32)
    out = np.zeros((n_sel, ncmp_pad), np.float32)
    out[:, :n_cmp] = ov.T
    return jnp.asarray(out)


def _pack_body(w_ref, o1_ref, o2_ref, oc_ref, og_ref):
    (mq, mk, mv, mo, mi, mf, nq, nkc, nvc, nks, nvs, nkw, nvw, ng, gm, end) = IN_OFFS

    def put(o_ref, off, a, b, scale=1.0):
        o_ref[:, off:off + b - a] = (w_ref[:, a:b] * scale).astype(o_ref.dtype)

    put(o1_ref, P1_MQ, mq, mk, M_DQK ** -0.5)
    put(o1_ref, P1_MK, mk, mo)
    put(o1_ref, P1_MO, mo, mi)
    put(o1_ref, P1_NVS, nvs, nkw)
    put(o1_ref, P1_NVW, nvw, ng)
    put(o1_ref, P1_GM, gm, end)
    put(o2_ref, P2_NQ, nq, nkc, N_HD ** -0.5)
    put(o2_ref, P2_NKS, nks, nvs)
    put(o2_ref, P2_NKW, nkw, nvw)
    put(oc_ref, 0, nkc, nks)
    og_ref[...] = jnp.zeros(og_ref.shape, og_ref.dtype)
    put(og_ref, 0, mi, nq)
    put(og_ref, nq - mi, ng, gm)


def _pack(w, layer, out_dtype, tr):
    rows = w.shape[1]
    widths = (P1_W, P2_W, 2 * N_KV_W, LANES)
    return pl.pallas_call(
        _pack_body,
        grid=(rows // tr,),
        in_specs=[pl.BlockSpec((None, tr, IN_OFFS[-1]), lambda i: (layer, i, 0))],
        out_specs=[pl.BlockSpec((tr, n), lambda i: (i, 0)) for n in widths],
        out_shape=[jax.ShapeDtypeStruct((rows, n), out_dtype) for n in widths],
        compiler_params=_params(("parallel",), 56),
        name="pack_in_proj",
    )(w)


def kernel(x, w_in, b_in, m_norm_w, cmp_pos_k, cmp_w1_k, cmp_b1_k, cmp_w2_k, cmp_b2_k, cmp_pos_v, cmp_w1_v,
           cmp_b1_v, cmp_w2_v, cmp_b2_v, w_branch_m, w_branch_n, w_out, ln1_g, ln1_b, router_w, router_b,
           exp_w1, exp_w3, exp_w2, ln2_g, ln2_b):
    B, T, D = x.shape
    m = B * T
    nb = T // CMP_STRIDE
    cos_t, sin_t = _rope_tables(jnp.arange(T))
    cos_c, sin_c = _rope_tables(jnp.arange(nb) * CMP_STRIDE + CMP_BLOCK - 1)
    ovt = _overlap_t(T, nb)
    perm = (np.arange(N_GROUPS)[None, :] * EXP_PER_GROUP + np.arange(EXP_PER_GROUP)[:, None]).reshape(-1)
    rw_t = router_w.T[perm]
    rb = jnp.broadcast_to(router_b[perm][:, None], (N_EXPERTS, LANES))

    (mi, nq, ng, gm) = (IN_OFFS[4], IN_OFFS[6], IN_OFFS[13], IN_OFFS[14])
    n_gate = (nq - mi) + (gm - ng)
    b_rows = jnp.pad(b_in[:, None, :], ((0, 0), (0, 7), (0, 0)))
    xf = x.reshape(m, D)
    xb = xf.astype(BF16)
    sorted_buf = jnp.zeros((2 * m + N_EXPERTS * MOE_TM, D), F32)
    for l in range(DEPTH):
        w1p, w2p, wcp, wgp = _pack(w_in, l, BF16, 256)
        b1p, b2p, bcp, bgp = (b[0] for b in _pack(b_rows, l, F32, 8))
        p1 = _matmul(xb, w1p, b1p, BF16, MM_TN)
        p2 = _matmul(xb, w2p, b2p, BF16, MM_TN, rope=(cos_t, sin_t), seq=T)
        pc = _matmul(xb, wcp, bcp, F32, MM_TN)
        pg = _matmul(xb, wgp, bgp, F32, LANES).reshape(B, T, LANES)
        gt_m = pg[:, :, :nq - mi].transpose(0, 2, 1)
        gt_n = jnp.pad(pg[:, :, nq - mi:n_gate].reshape(B, T, N_KV, 3 * N_HPG).transpose(0, 2, 3, 1),
                       ((0, 0), (0, 0), (0, 16 - 3 * N_HPG), (0, 0)))
        h_m = _mlstm(p1, pg, gt_m, m_norm_w[l], B, T).reshape(m, M_V_W)

        def blocks16(which):
            a = pc.reshape(B, nb, CMP_STRIDE, 2, N_KV, N_HD)[:, :, :, which]
            return a.transpose(0, 3, 1, 2, 4).reshape(B, N_KV, nb, CMP_STRIDE * N_HD)

        k_cmp = _compress(blocks16(0), cmp_pos_k[l], cmp_w1_k[l], cmp_b1_k[l], cmp_w2_k[l], cmp_b2_k[l],
                          cos_c, sin_c, True)
        v_cmp = _compress(blocks16(1), cmp_pos_v[l], cmp_w1_v[l], cmp_b1_v[l], cmp_w2_v[l], cmp_b2_v[l],
                          cos_c, sin_c, False)
        h_n = _nsa(p1, p2, gt_n, k_cmp, v_cmp, ovt, B, T).reshape(m, N_Q_W)
        merged = _merge(h_m, h_n, w_branch_m[l].astype(BF16), w_branch_n[l].astype(BF16), p1)
        xf, xb = _out_ln(merged, w_out[l].astype(BF16), xf, ln1_g[l], ln1_b[l])
        xf, xb, sorted_buf = _moe(xf, sorted_buf, rw_t, rb, exp_w1, exp_w3, exp_w2, l, ln2_g[l], ln2_b[l])
    return xf.reshape(B, T, D)
```

```python
import functools

import numpy as np
import jax
import jax.numpy as jnp
from jax import lax
from jax.experimental import pallas as pl
from jax.experimental.pallas import tpu as pltpu

F32 = jnp.float32
BF16 = jnp.bfloat16
HIGHEST = lax.Precision.HIGHEST

D_MODEL = 2048
DEPTH = 2
M_HEADS = 4
M_DQK = 256
M_DV = 512
N_KV = 4
N_HPG = 4
N_HEADS = N_KV * N_HPG
N_HD = 128
CMP_BLOCK = 32
CMP_STRIDE = 16
SEL_BLOCK = 64
SEL_TOPK = 16
WINDOW = 512
ROPE_THETA = 10000.0
N_EXPERTS = 32
N_GROUPS = 8
EXP_PER_GROUP = N_EXPERTS // N_GROUPS
D_FF = 1024
ALPHA = (2 * DEPTH) ** 0.25
EPS = 1e-5
NEG = -1e30

M_QK_W = M_HEADS * M_DQK
M_V_W = M_HEADS * M_DV
N_Q_W = N_HEADS * N_HD
N_KV_W = N_KV * N_HD
IN_SIZES = (M_QK_W, M_QK_W, M_V_W, M_V_W, M_HEADS, M_HEADS, N_Q_W, N_KV_W, N_KV_W, N_KV_W, N_KV_W,
            N_KV_W, N_KV_W, 3 * N_HEADS, 2 * D_MODEL)
IN_OFFS = tuple(int(v) for v in np.cumsum((0,) + IN_SIZES))

LANES = 128
MLSTM_CHUNK = 256
NSA_TQ = 256
NSA_GROUP = 4
NSA_ONES = 16
MM_TM = 512
MM_TN = 1024
MOE_TM = 256
MOE_TD = 256
ROUTER_TM = 512

P1_MQ, P1_MK, P1_MV, P1_MO, P1_NVS, P1_NVW, P1_GM = 0, 1024, 2048, 4096, 6144, 6656, 7168
P1_W = 7168 + 2 * D_MODEL
P2_NQ, P2_NKS, P2_NKW = 0, 2048, 2560
P2_W = 3072


def _params(sem, vmem_mb):
    return pltpu.CompilerParams(dimension_semantics=sem, vmem_limit_bytes=vmem_mb * 1024 * 1024)


def _nt(a, b, **kw):
    return lax.dot_general(a, b, (((1,), (1,)), ((), ())), preferred_element_type=F32, **kw)


def _tn(a, b):
    return lax.dot_general(a, b, (((0,), (0,)), ((), ())), preferred_element_type=F32)


def _standardize(x):
    mu = jnp.mean(x, -1, keepdims=True)
    xc = x - mu
    var = jnp.mean(xc * xc, -1, keepdims=True)
    return xc * lax.rsqrt(var + EPS)


def _mm_body(x_ref, w_ref, b_ref, *rest, rope, tn):
    acc = jnp.dot(x_ref[...], w_ref[...], preferred_element_type=F32) + b_ref[...]
    if rope:
        cos_ref, sin_ref, o_ref = rest
        cos = cos_ref[...]
        sin = sin_ref[...]
        for c in range(tn // N_HD):
            ch = acc[:, c * N_HD:(c + 1) * N_HD]
            o_ref[:, c * N_HD:(c + 1) * N_HD] = (ch * cos + pltpu.roll(ch, N_HD // 2, 1) * sin).astype(o_ref.dtype)
    else:
        (o_ref,) = rest
        o_ref[...] = acc.astype(o_ref.dtype)


def _matmul(x, w, b, out_dtype, tn, rope=None, seq=None):
    m, k = x.shape
    n = w.shape[1]
    tm = MM_TM
    grid = (n // tn, m // tm)
    in_specs = [pl.BlockSpec((tm, k), lambda j, i: (i, 0)),
                pl.BlockSpec((k, tn), lambda j, i: (0, j)),
                pl.BlockSpec((1, tn), lambda j, i: (0, j))]
    args = [x, w, b.reshape(1, n)]
    if rope is not None:
        nt = seq // tm
        in_specs += [pl.BlockSpec((tm, N_HD), lambda j, i: (i % nt, 0))] * 2
        args += list(rope)
    return pl.pallas_call(
        functools.partial(_mm_body, rope=rope is not None, tn=tn),
        grid=grid, in_specs=in_specs,
        out_specs=pl.BlockSpec((tm, tn), lambda j, i: (i, j)),
        out_shape=jax.ShapeDtypeStruct((m, n), out_dtype),
        compiler_params=_params(("parallel", "parallel"), 48),
        name="proj_matmul",
    )(*args)


def _log_sigmoid(x):
    return jnp.minimum(x, 0.0) - jnp.log1p(jnp.exp(-jnp.abs(x)))


def _mlstm_body(q_ref, k_ref, v_ref, og_ref, g_ref, gt_ref, nw_ref, out_ref, c_ref, n_ref, m_ref, *, L):
    @pl.when(pl.program_id(1) == 0)
    def _():
        c_ref[...] = jnp.zeros_like(c_ref)
        n_ref[...] = jnp.zeros_like(n_ref)
        m_ref[...] = jnp.zeros_like(m_ref)

    g = g_ref[...]
    gt = gt_ref[...]
    row = lax.broadcasted_iota(jnp.int32, (L, L), 0)
    col = lax.broadcasted_iota(jnp.int32, (L, L), 1)
    causal = row >= col
    b_all = jnp.dot(causal.astype(F32), _log_sigmoid(g), precision=HIGHEST, preferred_element_type=F32)
    bt_all = jnp.dot(_log_sigmoid(gt), (row <= col).astype(F32), precision=HIGHEST, preferred_element_type=F32)
    for h in range(M_HEADS):
        b_col = b_all[:, M_HEADS + h:M_HEADS + h + 1]
        li_col = g[:, h:h + 1]
        b_row = bt_all[M_HEADS + h:M_HEADS + h + 1, :]
        li_row = gt[h:h + 1, :]
        m_prev = m_ref[h][:, 0:1]
        dmat = jnp.where(causal, b_col - b_row + li_row, -jnp.inf)
        inter = b_col + m_prev
        m_t = jnp.maximum(inter, jnp.max(dmat, axis=1, keepdims=True))
        w_inter = jnp.exp(inter - m_t)
        q = q_ref[:, h * M_DQK:(h + 1) * M_DQK]
        k = k_ref[:, h * M_DQK:(h + 1) * M_DQK]
        v = v_ref[:, h * M_DV:(h + 1) * M_DV]
        s = _nt(q, k) * jnp.exp(dmat - m_t)
        ct = c_ref[h]
        n_row = n_ref[h]
        num = jnp.dot(s.astype(BF16), v, preferred_element_type=F32) + w_inter * jnp.dot(
            q, ct.astype(BF16), preferred_element_type=F32)
        qn = jnp.sum(q.astype(F32) * n_row, axis=1, keepdims=True)
        den = jnp.sum(s, axis=1, keepdims=True) + w_inter * qn
        hh = num / jnp.maximum(jnp.abs(den), jnp.exp(-m_t))
        hn = _standardize(hh) * nw_ref[:, h * M_DV:(h + 1) * M_DV]
        og = og_ref[:, h * M_DV:(h + 1) * M_DV].astype(F32)
        out_ref[:, h * M_DV:(h + 1) * M_DV] = (hn * jax.nn.sigmoid(og)).astype(out_ref.dtype)
        b_last = b_col[L - 1:L, :]
        g_col = b_last - b_col + li_col
        g_row = b_last - b_row + li_row
        m_new = jnp.maximum(b_last + m_prev, jnp.max(g_row, axis=1, keepdims=True))
        decay = jnp.exp(b_last + m_prev - m_new)
        ws_col = jnp.exp(g_col - m_new)
        kf = k.astype(F32)
        vw = (v.astype(F32) * ws_col).astype(BF16)
        c_ref[h] = decay * ct + _tn(k, vw)
        n_ref[h] = decay * n_row + jnp.sum(kf * ws_col, axis=0, keepdims=True)
        m_ref[h] = jnp.broadcast_to(m_new, (1, LANES))


def _mlstm(p1, p3, gt, norm_w, B, T):
    L = MLSTM_CHUNK
    p1 = p1.reshape(B, T, P1_W)
    return pl.pallas_call(
        functools.partial(_mlstm_body, L=L),
        grid=(B, T // L),
        in_specs=[pl.BlockSpec((None, L, M_QK_W), lambda b, c: (b, c, P1_MQ // M_QK_W)),
                  pl.BlockSpec((None, L, M_QK_W), lambda b, c: (b, c, P1_MK // M_QK_W)),
                  pl.BlockSpec((None, L, M_V_W), lambda b, c: (b, c, P1_MV // M_V_W)),
                  pl.BlockSpec((None, L, M_V_W), lambda b, c: (b, c, P1_MO // M_V_W)),
                  pl.BlockSpec((None, L, LANES), lambda b, c: (b, c, 0)),
                  pl.BlockSpec((None, 8, L), lambda b, c: (b, 0, c)),
                  pl.BlockSpec((1, M_V_W), lambda b, c: (0, 0))],
        out_specs=pl.BlockSpec((None, L, M_V_W), lambda b, c: (b, c, 0)),
        out_shape=jax.ShapeDtypeStruct((B, T, M_V_W), BF16),
        scratch_shapes=[pltpu.VMEM((M_HEADS, M_DQK, M_DV), F32),
                        pltpu.VMEM((M_HEADS, 1, M_DQK), F32),
                        pltpu.VMEM((M_HEADS, 1, LANES), F32)],
        compiler_params=_params(("parallel", "arbitrary"), 48),
        name="mlstm",
    )(p1, p1, p1, p1, p3, gt, norm_w.reshape(1, M_V_W))


def _compress_body(x_ref, pos_ref, w1_ref, b1_ref, w2_ref, b2_ref, cos_ref, sin_ref, o_ref, *, rope, nb):
    lo = jnp.zeros((nb, N_HD), F32)
    hi = jnp.zeros((nb, N_HD), F32)
    for l in range(CMP_STRIDE):
        xl = x_ref[pl.ds(l, nb, stride=CMP_STRIDE), :]
        lo = lo + jnp.dot((xl + pos_ref[l:l + 1, :]).astype(BF16), w1_ref[l], preferred_element_type=F32)
        hi = hi + jnp.dot((xl + pos_ref[CMP_STRIDE + l:CMP_STRIDE + l + 1, :]).astype(BF16),
                          w1_ref[CMP_STRIDE + l], preferred_element_type=F32)
    pre = lo + pltpu.roll(hi, nb - 1, 0) + b1_ref[...]
    h = jax.nn.gelu(pre)
    y = jnp.dot(h.astype(BF16), w2_ref[...], preferred_element_type=F32) + b2_ref[...]
    if rope:
        y = y * cos_ref[...] + pltpu.roll(y, N_HD // 2, 1) * sin_ref[...]
    keep = lax.broadcasted_iota(jnp.int32, (nb, N_HD), 0) < nb - 1
    o_ref[...] = jnp.where(keep, y, 0.0).astype(o_ref.dtype)


def _compress(pc, which, pos, w1, b1, w2, b2, cos, sin, rope):
    B, T, _ = pc.shape
    nb = T // CMP_STRIDE
    full = lambda shape: pl.BlockSpec(shape, lambda b, g: (0,) * len(shape))
    return pl.pallas_call(
        functools.partial(_compress_body, rope=rope, nb=nb),
        grid=(B, N_KV),
        in_specs=[pl.BlockSpec((None, T, N_HD), lambda b, g: (b, 0, which * N_KV + g)),
                  full((CMP_BLOCK, N_HD)), full((CMP_BLOCK, N_HD, N_HD)), full((1, N_HD)),
                  full((N_HD, N_HD)), full((1, N_HD)), full((nb, N_HD)), full((nb, N_HD))],
        out_specs=pl.BlockSpec((None, None, nb, N_HD), lambda b, g: (b, g, 0, 0)),
        out_shape=jax.ShapeDtypeStruct((B, N_KV, nb, N_HD), BF16),
        compiler_params=_params(("parallel", "parallel"), 32),
        name="nsa_compress",
    )(pc, pos, w1.astype(BF16), b1.reshape(1, N_HD), w2.astype(BF16), b2.reshape(1, N_HD), cos, sin)


def _nsa_body(q_ref, ks_ref, vst_ref, kw_ref, vwt_ref, kc_ref, vct_ref, gt_ref, ovt_ref, o_ref,
              acc_ref, m_ref, sel_ref, out_ref, s_ref, p_ref, *, tq, ncmp):
    qi = pl.program_id(2)
    t0 = qi * tq
    nsel = ovt_ref.shape[0]
    gates = jax.nn.sigmoid(gt_ref[...])
    q_heads = [q_ref[:, h * N_HD:(h + 1) * N_HD] for h in range(N_HPG)]

    kc = kc_ref[...]
    vct = vct_ref[...]
    n_idx = lax.broadcasted_iota(jnp.int32, (ncmp, tq), 0)
    t_cmp = t0 + lax.broadcasted_iota(jnp.int32, (ncmp, tq), 1)
    cmask = (n_idx * CMP_STRIDE + (CMP_BLOCK - 1) <= t_cmp) & (n_idx < ncmp - 1)
    psum = jnp.zeros((ncmp, tq), F32)
    for h in range(N_HPG):
        s_ref[0, h, :ncmp, :] = _nt(kc, q_heads[h])
    for h in range(N_HPG):
        s = jnp.where(cmask, s_ref[0, h, :ncmp, :], NEG)
        e = jnp.where(cmask, jnp.exp(s - jnp.max(s, axis=0, keepdims=True)), 0.0)
        p = e * (1.0 / jnp.maximum(jnp.sum(e, axis=0, keepdims=True), 1e-30))
        psum = psum + p
        p_ref[0, h, :ncmp, :] = p.astype(BF16)
    for h in range(N_HPG):
        out_ref[h] = gates[3 * h:3 * h + 1, :] * jnp.dot(vct, p_ref[0, h, :ncmp, :], preferred_element_type=F32)

    imp = jnp.dot(ovt_ref[...], psum, precision=HIGHEST, preferred_element_type=F32)
    j_idx = lax.broadcasted_iota(jnp.int32, (nsel, tq), 0)
    t_row = t0 + lax.broadcasted_iota(jnp.int32, (nsel, tq), 1)
    cur = lax.shift_right_logical(t_row, 6)
    forced = (j_idx == 0) | (j_idx == cur) | (j_idx == cur - 1)
    score = jnp.where(forced, jnp.inf, jnp.where(j_idx * SEL_BLOCK <= t_row, imp, -jnp.inf))

    def pick(_, carry):
        score, sel = carry
        mx = jnp.max(score, axis=0, keepdims=True)
        idx = jnp.min(jnp.where(score == mx, j_idx, nsel), axis=0, keepdims=True)
        hit = j_idx == idx
        return jnp.where(hit, -jnp.inf, score), jnp.where(hit, 1.0, sel)

    _, sel = lax.fori_loop(0, min(SEL_TOPK, nsel), pick, (score, jnp.zeros((nsel, tq), F32)), unroll=True)
    sel_ref[...] = sel

    def flash_init():
        acc_ref[...] = jnp.zeros_like(acc_ref)
        m_ref[...] = jnp.full(m_ref.shape, NEG, F32)

    def flash_tiles(k_ref, vt_ref, tiles):
        vts = []
        for n, (j, _) in enumerate(tiles):
            start = pl.multiple_of(j * tq, tq)
            k = k_ref[pl.ds(start, tq), :]
            vts.append(vt_ref[:, pl.ds(start, tq)])
            for h in range(N_HPG):
                s_ref[n, h] = _nt(k, q_heads[h])
        for n, (_, mask) in enumerate(tiles):
            alphas = []
            for h in range(N_HPG):
                s = s_ref[n, h]
                if mask is not None:
                    s = jnp.where(mask, s, NEG)
                m_old = m_ref[h]
                m_new = jnp.maximum(m_old, jnp.max(s, axis=0, keepdims=True))
                p_ref[n, h] = jnp.exp((s - m_new).astype(BF16))
                alpha = jnp.exp(m_old - m_new)
                m_ref[h] = m_new
                alphas.append(alpha)
            for h in range(N_HPG):
                acc_ref[h] = alphas[h] * acc_ref[h] + jnp.dot(vts[n], p_ref[n, h], preferred_element_type=F32)

    def flash_add(branch):
        for h in range(N_HPG):
            scale = gates[3 * h + branch:3 * h + branch + 1, :] / acc_ref[h, N_HD:N_HD + 1, :]
            out_ref[h] = out_ref[h] + scale * acc_ref[h, :N_HD, :]

    kpos = lax.broadcasted_iota(jnp.int32, (tq, tq), 0)
    qpos = lax.broadcasted_iota(jnp.int32, (tq, tq), 1)
    blk_per_tile = tq // SEL_BLOCK

    def sel_mask(j):
        rows = [jnp.broadcast_to(sel_ref[pl.ds(j * blk_per_tile + b, 1), :], (SEL_BLOCK, tq))
                for b in range(blk_per_tile)]
        return jnp.concatenate(rows, axis=0) > 0.5

    flash_init()

    group = s_ref.shape[0]

    def sel_group(p, carry):
        flash_tiles(ks_ref, vst_ref, [(group * p + n, sel_mask(group * p + n)) for n in range(group)])
        return carry

    lax.fori_loop(0, qi // group, sel_group, 0)
    causal = kpos <= qpos
    for rem in range(group):
        @pl.when(qi % group == rem)
        def _():
            tiles = [(qi - rem + n, sel_mask(qi - rem + n)) for n in range(rem)]
            flash_tiles(ks_ref, vst_ref, tiles + [(qi, sel_mask(qi) & causal)])

    flash_add(1)

    flash_init()
    n_back = WINDOW // tq
    win_tiles = [(qi, causal)] + [(qi - back, None) for back in range(1, n_back)] + [(qi - n_back, kpos > qpos)]
    for n_tiles in range(1, n_back + 2):
        last = n_tiles == n_back + 1

        @pl.when((qi >= n_tiles - 1) if last else (qi == n_tiles - 1))
        def _():
            flash_tiles(kw_ref, vwt_ref, win_tiles[:n_tiles])
    flash_add(2)

    for h in range(N_HPG):
        o_ref[:, h * N_HD:(h + 1) * N_HD] = out_ref[h].T.astype(o_ref.dtype)


def _nsa(p1, p2, gt, k_cmp, v_cmp, ovt, B, T):
    tq = NSA_TQ
    ncmp = k_cmp.shape[2]
    p1 = p1.reshape(B, T, P1_W)
    p2 = p2.reshape(B, T, P2_W)
    hw = N_HPG * N_HD
    assert WINDOW % tq == 0 and tq % SEL_BLOCK == 0 and ncmp <= tq and NSA_GROUP >= WINDOW // tq + 1
    ones = jnp.ones((B, N_KV, NSA_ONES, T), BF16)

    def values_t(off):
        vt = p1[:, :, off:off + N_KV_W].transpose(0, 2, 1).reshape(B, N_KV, N_HD, T)
        return jnp.concatenate([vt, ones], axis=2)

    vst = values_t(P1_NVS)
    vwt = values_t(P1_NVW)
    vct = v_cmp.transpose(0, 1, 3, 2)
    k_spec = lambda off: pl.BlockSpec((None, T, N_HD), lambda b, g, i: (b, 0, off // N_HD + g))
    vt_spec = pl.BlockSpec((None, None, N_HD + NSA_ONES, T), lambda b, g, i: (b, g, 0, 0))
    return pl.pallas_call(
        functools.partial(_nsa_body, tq=tq, ncmp=ncmp),
        grid=(B, N_KV, T // tq),
        in_specs=[pl.BlockSpec((None, tq, hw), lambda b, g, i: (b, i, P2_NQ // hw + g)),
                  k_spec(P2_NKS), vt_spec, k_spec(P2_NKW), vt_spec,
                  pl.BlockSpec((None, None, ncmp, N_HD), lambda b, g, i: (b, g, 0, 0)),
                  pl.BlockSpec((None, None, N_HD, ncmp), lambda b, g, i: (b, g, 0, 0)),
                  pl.BlockSpec((None, None, 16, tq), lambda b, g, i: (b, g, 0, i)),
                  pl.BlockSpec(ovt.shape, lambda b, g, i: (0, 0))],
        out_specs=pl.BlockSpec((None, tq, hw), lambda b, g, i: (b, i, g)),
        out_shape=jax.ShapeDtypeStruct((B, T, N_Q_W), BF16),
        scratch_shapes=[pltpu.VMEM((N_HPG, N_HD + NSA_ONES, tq), F32),
                        pltpu.VMEM((N_HPG, 1, tq), F32), pltpu.VMEM((ovt.shape[0], tq), F32),
                        pltpu.VMEM((N_HPG, N_HD, tq), F32),
                        pltpu.VMEM((NSA_GROUP, N_HPG, tq, tq), F32),
                        pltpu.VMEM((NSA_GROUP, N_HPG, tq, tq), BF16)],
        compiler_params=_params(("parallel", "parallel", "arbitrary"), 48),
        name="nsa_attention",
    )(p2, p2, vst, p2, vwt, k_cmp, vct, gt, ovt)


def _merge_body(hm_ref, hn_ref, wm_ref, wn_ref, gm_ref, gn_ref, o_ref):
    ym = jnp.dot(hm_ref[...], wm_ref[...], preferred_element_type=F32)
    yn = jnp.dot(hn_ref[...], wn_ref[...], preferred_element_type=F32)
    gm = jax.nn.sigmoid(gm_ref[...].astype(F32))
    gn = jax.nn.sigmoid(gn_ref[...].astype(F32))
    o_ref[...] = (gm * ym + gn * yn).astype(o_ref.dtype)


def _merge(hm, hn, wm, wn, p1):
    m = hm.shape[0]
    tm, tn = MM_TM, 512
    return pl.pallas_call(
        _merge_body,
        grid=(D_MODEL // tn, m // tm),
        in_specs=[pl.BlockSpec((tm, M_V_W), lambda j, i: (i, 0)),
                  pl.BlockSpec((tm, N_Q_W), lambda j, i: (i, 0)),
                  pl.BlockSpec((M_V_W, tn), lambda j, i: (0, j)),
                  pl.BlockSpec((N_Q_W, tn), lambda j, i: (0, j)),
                  pl.BlockSpec((tm, tn), lambda j, i: (i, P1_GM // tn + j)),
                  pl.BlockSpec((tm, tn), lambda j, i: (i, (P1_GM + D_MODEL) // tn + j))],
        out_specs=pl.BlockSpec((tm, tn), lambda j, i: (i, j)),
        out_shape=jax.ShapeDtypeStruct((m, D_MODEL), BF16),
        compiler_params=_params(("parallel", "parallel"), 48),
        name="branch_merge",
    )(hm, hn, wm, wn, p1, p1)


def _out_ln_body(y_ref, w_ref, x_ref, g_ref, b_ref, o_ref, ob_ref):
    mix = jnp.dot(y_ref[...], w_ref[...], preferred_element_type=F32)
    out = _standardize(ALPHA * x_ref[...] + mix) * g_ref[...] + b_ref[...]
    o_ref[...] = out
    ob_ref[...] = out.astype(BF16)


def _out_ln(y, w, x, g, b):
    m = y.shape[0]
    tm = MM_TM
    row = pl.BlockSpec((tm, D_MODEL), lambda i: (i, 0))
    vec = pl.BlockSpec((1, D_MODEL), lambda i: (0, 0))
    return pl.pallas_call(
        _out_ln_body,
        grid=(m // tm,),
        in_specs=[row, pl.BlockSpec((D_MODEL, D_MODEL), lambda i: (0, 0)), row, vec, vec],
        out_specs=[row, row],
        out_shape=[jax.ShapeDtypeStruct((m, D_MODEL), F32), jax.ShapeDtypeStruct((m, D_MODEL), BF16)],
        compiler_params=_params(("parallel",), 48),
        name="out_proj_layernorm",
    )(y, w, x, g.reshape(1, -1), b.reshape(1, -1))


def _first_of4(vals, target):
    return jnp.where(vals[0] == target, 0.0, jnp.where(vals[1] == target, 1.0, jnp.where(vals[2] == target, 2.0, 3.0)))


def _select4(idx, vals):
    return jnp.where(idx == 0.0, vals[0], jnp.where(idx == 1.0, vals[1], jnp.where(idx == 2.0, vals[2], vals[3])))


def _router_body(x_ref, rw_ref, rb_ref, o_ref, cnt_ref, carry_ref, *, tm):
    @pl.when(pl.program_id(0) == 0)
    def _():
        carry_ref[...] = jnp.zeros_like(carry_ref)

    logits = _nt(rw_ref[...], x_ref[...], precision=HIGHEST)
    aff = jax.nn.sigmoid(logits)
    biased = aff + rb_ref[:, 0:1]
    a = [biased[i * N_GROUPS:(i + 1) * N_GROUPS, :] for i in range(EXP_PER_GROUP)]
    af = [aff[i * N_GROUPS:(i + 1) * N_GROUPS, :] for i in range(EXP_PER_GROUP)]
    m1 = jnp.maximum(jnp.maximum(a[0], a[1]), jnp.maximum(a[2], a[3]))
    i1 = _first_of4(a, m1)
    rest = [jnp.where(i1 == float(i), -jnp.inf, a[i]) for i in range(EXP_PER_GROUP)]
    m2 = jnp.maximum(jnp.maximum(rest[0], rest[1]), jnp.maximum(rest[2], rest[3]))
    i2 = _first_of4(rest, m2)
    gscore = m1 + m2
    g_iota = lax.broadcasted_iota(jnp.int32, (N_GROUPS, tm), 0).astype(F32)
    g_idx = jnp.min(jnp.where(gscore == jnp.max(gscore, axis=0, keepdims=True), g_iota, float(N_GROUPS)),
                    axis=0, keepdims=True)
    in_g = g_iota == g_idx
    take = lambda v: jnp.sum(jnp.where(in_g, v, 0.0), axis=0, keepdims=True)
    s0 = take(i1)
    s1 = take(i2)
    w0 = take(_select4(i1, af))
    w1 = take(_select4(i2, af))
    wsum = w0 + w1
    r0 = s0 * N_GROUPS + g_idx
    r1 = s1 * N_GROUPS + g_idx
    r_iota = lax.broadcasted_iota(jnp.int32, (N_EXPERTS, tm), 0).astype(F32)
    member = (r_iota == r0) | (r_iota == r1)
    tt = lax.broadcasted_iota(jnp.int32, (tm, tm), 0)
    tc = lax.broadcasted_iota(jnp.int32, (tm, tm), 1)
    before = jnp.dot(member.astype(BF16), (tt < tc).astype(BF16), preferred_element_type=F32)
    base = before + carry_ref[:, 0:1]
    rank0 = jnp.sum(jnp.where(r_iota == r0, base, 0.0), axis=0, keepdims=True)
    rank1 = jnp.sum(jnp.where(r_iota == r1, base, 0.0), axis=0, keepdims=True)
    new_carry = carry_ref[:, 0:1] + jnp.sum(member.astype(F32), axis=1, keepdims=True)
    carry_ref[...] = jnp.broadcast_to(new_carry, carry_ref.shape)
    cnt_ref[...] = jnp.broadcast_to(new_carry, cnt_ref.shape)
    o_ref[0:1, :] = g_idx * EXP_PER_GROUP + s0
    o_ref[1:2, :] = g_idx * EXP_PER_GROUP + s1
    o_ref[2:3, :] = w0 / wsum
    o_ref[3:4, :] = w1 / wsum
    o_ref[4:5, :] = rank0
    o_ref[5:6, :] = rank1
    o_ref[6:8, :] = jnp.zeros((2, tm), F32)


def _router(x, rw_t, rb):
    m = x.shape[0]
    tm = ROUTER_TM
    return pl.pallas_call(
        functools.partial(_router_body, tm=tm),
        grid=(m // tm,),
        in_specs=[pl.BlockSpec((tm, D_MODEL), lambda i: (i, 0)),
                  pl.BlockSpec((N_EXPERTS, D_MODEL), lambda i: (0, 0)),
                  pl.BlockSpec((N_EXPERTS, LANES), lambda i: (0, 0))],
        out_specs=[pl.BlockSpec((8, tm), lambda i: (0, i)),
                   pl.BlockSpec((N_EXPERTS, LANES), lambda i: (0, 0))],
        out_shape=[jax.ShapeDtypeStruct((8, m), F32), jax.ShapeDtypeStruct((N_EXPERTS, LANES), F32)],
        scratch_shapes=[pltpu.VMEM((N_EXPERTS, LANES), F32)],
        compiler_params=_params(("arbitrary",), 48),
        name="moe_router",
    )(x, rw_t, rb)


def _row_copy(src_ref, src_row, dst_ref, dst_row, sem):
    return pltpu.make_async_copy(src_ref.at[pl.ds(src_row, 1), :], dst_ref.at[pl.ds(dst_row, 1), :], sem)


def _dispatch_body(dest_ref, x_ref, init_ref, xs_ref, sem, *, td):
    del init_ref

    def issue(r, c):
        _row_copy(x_ref, r, xs_ref, dest_ref[0, 0, r], sem).start(priority=0)
        _row_copy(x_ref, r, xs_ref, dest_ref[0, 0, td + r], sem).start(priority=1)
        return c

    lax.fori_loop(0, td, issue, 0)
    for _ in range(2):
        pltpu.make_async_copy(x_ref, xs_ref.at[pl.ds(0, td), :], sem).wait()


def _dispatch(x, dest, init):
    rows = init.shape[0]
    m = x.shape[0]
    td = MOE_TD
    return pl.pallas_call(
        functools.partial(_dispatch_body, td=td),
        grid=(m // td,),
        in_specs=[pl.BlockSpec((1, 1, 2 * td), lambda i: (i, 0, 0), memory_space=pltpu.SMEM),
                  pl.BlockSpec((td, D_MODEL), lambda i: (i, 0)),
                  pl.BlockSpec(memory_space=pl.ANY)],
        out_specs=pl.BlockSpec(memory_space=pl.ANY),
        out_shape=jax.ShapeDtypeStruct((rows, D_MODEL), F32),
        scratch_shapes=[pltpu.SemaphoreType.DMA(())],
        input_output_aliases={2: 0},
        compiler_params=_params(("arbitrary",), 32),
        name="moe_dispatch",
    )(dest, x, init)


def _expert_changed(te_ref, i):
    return (i == 0) | (te_ref[i] != te_ref[jnp.maximum(i - 1, 0)])


def _experts_up_body(te_ref, nused_ref, x_ref, w1_ref, w3_ref, h_ref, w1b_ref, w3b_ref):
    i = pl.program_id(0)
    used = i < nused_ref[0]

    @pl.when(used & _expert_changed(te_ref, i))
    def _():
        w1b_ref[...] = w1_ref[...].astype(BF16)
        w3b_ref[...] = w3_ref[...].astype(BF16)

    @pl.when(used)
    def _():
        xb = x_ref[...].astype(BF16)
        a = jnp.dot(xb, w1b_ref[...], preferred_element_type=F32)
        b = jnp.dot(xb, w3b_ref[...], preferred_element_type=F32)
        h_ref[...] = (a * jax.nn.sigmoid(a) * b).astype(BF16)

    @pl.when(jnp.logical_not(used))
    def _():
        h_ref[...] = jnp.zeros_like(h_ref)


def _experts_down_body(te_ref, nused_ref, h_ref, w2_ref, o_ref, w2b_ref):
    i = pl.program_id(0)
    used = i < nused_ref[0]

    @pl.when(used & _expert_changed(te_ref, i))
    def _():
        w2b_ref[...] = w2_ref[...].astype(BF16)

    @pl.when(used)
    def _():
        o_ref[...] = jnp.dot(h_ref[...], w2b_ref[...], preferred_element_type=F32)

    @pl.when(jnp.logical_not(used))
    def _():
        o_ref[...] = jnp.zeros_like(o_ref)


def _experts(xs, w1, w3, w2, layer, tile_expert, n_used):
    rows = xs.shape[0]
    tm = MOE_TM
    by_expert = lambda r, c: pl.BlockSpec((None, None, r, c), lambda i, te, nu: (layer, te[i], 0, 0))
    by_tile = lambda c: pl.BlockSpec((tm, c), lambda i, te, nu: (i, 0))
    h = pl.pallas_call(
        _experts_up_body,
        grid_spec=pltpu.PrefetchScalarGridSpec(
            num_scalar_prefetch=2, grid=(rows // tm,),
            in_specs=[by_tile(D_MODEL), by_expert(D_MODEL, D_FF), by_expert(D_MODEL, D_FF)],
            out_specs=by_tile(D_FF),
            scratch_shapes=[pltpu.VMEM((D_MODEL, D_FF), BF16), pltpu.VMEM((D_MODEL, D_FF), BF16)]),
        out_shape=jax.ShapeDtypeStruct((rows, D_FF), BF16),
        compiler_params=_params(("arbitrary",), 56),
        name="moe_experts_up",
    )(tile_expert, n_used, xs, w1, w3)
    return pl.pallas_call(
        _experts_down_body,
        grid_spec=pltpu.PrefetchScalarGridSpec(
            num_scalar_prefetch=2, grid=(rows // tm,),
            in_specs=[by_tile(D_FF), by_expert(D_FF, D_MODEL)],
            out_specs=by_tile(D_MODEL),
            scratch_shapes=[pltpu.VMEM((D_FF, D_MODEL), BF16)]),
        out_shape=jax.ShapeDtypeStruct((rows, D_MODEL), F32),
        compiler_params=_params(("arbitrary",), 40),
        name="moe_experts_down",
    )(tile_expert, n_used, h, w2)


def _combine_body(dest_ref, next_ref, ys_ref, x_ref, w_ref, g_ref, b_ref, o_ref, ob_ref, buf_ref, sem, *, td):
    i = pl.program_id(0)
    slot = i % 2

    def gather(idx_ref, s):
        def issue(r, c):
            _row_copy(ys_ref, idx_ref[0, 0, r], buf_ref.at[s, 0], r, sem.at[s]).start(priority=0)
            _row_copy(ys_ref, idx_ref[0, 0, td + r], buf_ref.at[s, 1], r, sem.at[s]).start(priority=1)
            return c

        lax.fori_loop(0, td, issue, 0)

    @pl.when(i == 0)
    def _():
        gather(dest_ref, 0)

    @pl.when(i + 1 < pl.num_programs(0))
    def _():
        gather(next_ref, 1 - slot)

    for half in range(2):
        pltpu.make_async_copy(ys_ref.at[pl.ds(0, td), :], buf_ref.at[slot, half], sem.at[slot]).wait()
    w = w_ref[...]
    ff = w[:, 0:1] * buf_ref[slot, 0] + w[:, 1:2] * buf_ref[slot, 1]
    out = _standardize(ALPHA * x_ref[...] + ff) * g_ref[...] + b_ref[...]
    o_ref[...] = out
    ob_ref[...] = out.astype(BF16)


def _combine(ys, dest, x, w, g, b):
    m = x.shape[0]
    td = MOE_TD
    row = pl.BlockSpec((td, D_MODEL), lambda i: (i, 0))
    vec = pl.BlockSpec((1, D_MODEL), lambda i: (0, 0))
    last = m // td - 1
    return pl.pallas_call(
        functools.partial(_combine_body, td=td),
        grid=(m // td,),
        in_specs=[pl.BlockSpec((1, 1, 2 * td), lambda i: (i, 0, 0), memory_space=pltpu.SMEM),
                  pl.BlockSpec((1, 1, 2 * td), lambda i: (jnp.minimum(i + 1, last), 0, 0), memory_space=pltpu.SMEM),
                  pl.BlockSpec(memory_space=pl.ANY), row,
                  pl.BlockSpec((td, 8), lambda i: (i, 0)), vec, vec],
        out_specs=[row, row],
        out_shape=[jax.ShapeDtypeStruct((m, D_MODEL), F32), jax.ShapeDtypeStruct((m, D_MODEL), BF16)],
        scratch_shapes=[pltpu.VMEM((2, 2, td, D_MODEL), F32), pltpu.SemaphoreType.DMA((2,))],
        compiler_params=_params(("arbitrary",), 40),
        name="moe_combine",
    )(dest, dest, ys, x, w, g.reshape(1, -1), b.reshape(1, -1))


def _moe(x, sorted_buf, rw_t, rb, w1, w3, w2, layer, ln_g, ln_b):
    m = x.shape[0]
    tm, td = MOE_TM, MOE_TD
    rows = sorted_buf.shape[0]
    ro, cnt = _router(x, rw_t, rb)
    e0 = ro[0].astype(jnp.int32)
    e1 = ro[1].astype(jnp.int32)
    counts = cnt[:, 0].astype(jnp.int32).reshape(EXP_PER_GROUP, N_GROUPS).T.reshape(N_EXPERTS)
    padded = (counts + tm - 1) // tm * tm
    ends = jnp.cumsum(padded)
    offs = ends - padded
    d0 = offs[e0] + ro[4].astype(jnp.int32)
    d1 = offs[e1] + ro[5].astype(jnp.int32)
    dest = jnp.concatenate([d0.reshape(m // td, 1, td), d1.reshape(m // td, 1, td)], axis=-1)
    tile_start = jnp.arange(rows // tm, dtype=jnp.int32) * tm
    tile_expert = jnp.minimum(jnp.sum(tile_start[:, None] >= ends[None, :], axis=1), N_EXPERTS - 1).astype(jnp.int32)
    n_used = (ends[-1:] // tm).astype(jnp.int32)
    xs = _dispatch(x, dest, sorted_buf)
    ys = _experts(xs, w1, w3, w2, layer, tile_expert, n_used)
    wcol = jnp.pad(ro[2:4].T, ((0, 0), (0, 6)))
    out, out_bf16 = _combine(ys, dest, x, wcol, ln_g, ln_b)
    return out, out_bf16, xs


def _rope_tables(pos):
    half = N_HD // 2
    inv = ROPE_THETA ** (-jnp.arange(half, dtype=F32) / half)
    ang = pos.astype(F32)[:, None] * inv[None, :]
    cos, sin = jnp.cos(ang), jnp.sin(ang)
    return jnp.concatenate([cos, cos], -1), jnp.concatenate([-sin, sin], -1)


def _overlap_t(T, ncmp_pad):
    n_cmp = (T - CMP_BLOCK) // CMP_STRIDE + 1
    n_sel = T // SEL_BLOCK
    cs = np.arange(n_cmp) * CMP_STRIDE
    ss = np.arange(n_sel) * SEL_BLOCK
    ov = ((cs[:, None] <= ss[None, :] + SEL_BLOCK - 1) & (cs[:, None] + CMP_BLOCK - 1 >= ss[None, :])).astype(np.float32)
    out = np.zeros((n_sel, ncmp_pad), np.float32)
    out[:, :n_cmp] = ov.T
    return jnp.asarray(out)


def _pack_body(w_ref, o1_ref, o2_ref, oc_ref, og_ref):
    (mq, mk, mv, mo, mi, mf, nq, nkc, nvc, nks, nvs, nkw, nvw, ng, gm, end) = IN_OFFS

    def put(o_ref, off, a, b, scale=1.0):
        o_ref[:, off:off + b - a] = (w_ref[:, a:b] * scale).astype(o_ref.dtype)

    put(o1_ref, P1_MQ, mq, mk, M_DQK ** -0.5)
    put(o1_ref, P1_MK, mk, mo)
    put(o1_ref, P1_MO, mo, mi)
    put(o1_ref, P1_NVS, nvs, nkw)
    put(o1_ref, P1_NVW, nvw, ng)
    put(o1_ref, P1_GM, gm, end)
    put(o2_ref, P2_NQ, nq, nkc, N_HD ** -0.5)
    put(o2_ref, P2_NKS, nks, nvs)
    put(o2_ref, P2_NKW, nkw, nvw)
    put(oc_ref, 0, nkc, nks)
    og_ref[...] = jnp.zeros(og_ref.shape, og_ref.dtype)
    put(og_ref, 0, mi, nq)
    put(og_ref, nq - mi, ng, gm)


def _pack(w, layer, out_dtype, tr):
    rows = w.shape[1]
    widths = (P1_W, P2_W, 2 * N_KV_W, LANES)
    return pl.pallas_call(
        _pack_body,
        grid=(rows // tr,),
        in_specs=[pl.BlockSpec((None, tr, IN_OFFS[-1]), lambda i: (layer, i, 0))],
        out_specs=[pl.BlockSpec((tr, n), lambda i: (i, 0)) for n in widths],
        out_shape=[jax.ShapeDtypeStruct((rows, n), out_dtype) for n in widths],
        compiler_params=_params(("parallel",), 56),
        name="pack_in_proj",
    )(w)


def kernel(x, w_in, b_in, m_norm_w, cmp_pos_k, cmp_w1_k, cmp_b1_k, cmp_w2_k, cmp_b2_k, cmp_pos_v, cmp_w1_v,
           cmp_b1_v, cmp_w2_v, cmp_b2_v, w_branch_m, w_branch_n, w_out, ln1_g, ln1_b, router_w, router_b,
           exp_w1, exp_w3, exp_w2, ln2_g, ln2_b):
    B, T, D = x.shape
    m = B * T
    nb = T // CMP_STRIDE
    cos_t, sin_t = _rope_tables(jnp.arange(T))
    cos_c, sin_c = _rope_tables(jnp.arange(nb) * CMP_STRIDE + CMP_BLOCK - 1)
    ovt = _overlap_t(T, nb)
    perm = (np.arange(N_GROUPS)[None, :] * EXP_PER_GROUP + np.arange(EXP_PER_GROUP)[:, None]).reshape(-1)
    rw_t = router_w.T[perm]
    rb = jnp.broadcast_to(router_b[perm][:, None], (N_EXPERTS, LANES))

    (mi, nq, ng, gm) = (IN_OFFS[4], IN_OFFS[6], IN_OFFS[13], IN_OFFS[14])
    n_gate = (nq - mi) + (gm - ng)
    b_rows = jnp.pad(b_in[:, None, :], ((0, 0), (0, 7), (0, 0)))
    xf = x.reshape(m, D)
    xb = xf.astype(BF16)
    sorted_buf = jnp.zeros((2 * m + N_EXPERTS * MOE_TM, D), F32)
    for l in range(DEPTH):
        w1p, w2p, wcp, wgp = _pack(w_in, l, BF16, 256)
        b1p, b2p, bcp, bgp = (b[0] for b in _pack(b_rows, l, F32, 8))
        p1 = _matmul(xb, w1p, b1p, BF16, MM_TN)
        p2 = _matmul(xb, w2p, b2p, BF16, MM_TN, rope=(cos_t, sin_t), seq=T)
        pc = _matmul(xb, wcp, bcp, F32, MM_TN)
        pg = _matmul(xb, wgp, bgp, F32, LANES).reshape(B, T, LANES)
        gt_m = pg[:, :, :nq - mi].transpose(0, 2, 1)
        gt_n = jnp.pad(pg[:, :, nq - mi:n_gate].reshape(B, T, N_KV, 3 * N_HPG).transpose(0, 2, 3, 1),
                       ((0, 0), (0, 0), (0, 16 - 3 * N_HPG), (0, 0)))
        h_m = _mlstm(p1, pg, gt_m, m_norm_w[l], B, T).reshape(m, M_V_W)

        pc = pc.reshape(B, T, 2 * N_KV_W)
        k_cmp = _compress(pc, 0, cmp_pos_k[l], cmp_w1_k[l], cmp_b1_k[l], cmp_w2_k[l], cmp_b2_k[l],
                          cos_c, sin_c, True)
        v_cmp = _compress(pc, 1, cmp_pos_v[l], cmp_w1_v[l], cmp_b1_v[l], cmp_w2_v[l], cmp_b2_v[l],
                          cos_c, sin_c, False)
        h_n = _nsa(p1, p2, gt_n, k_cmp, v_cmp, ovt, B, T).reshape(m, N_Q_W)
        merged = _merge(h_m, h_n, w_branch_m[l].astype(BF16), w_branch_n[l].astype(BF16), p1)
        xf, xb = _out_ln(merged, w_out[l].astype(BF16), xf, ln1_g[l], ln1_b[l])
        xf, xb, sorted_buf = _moe(xf, sorted_buf, rw_t, rb, exp_w1, exp_w3, exp_w2, l, ln2_g[l], ln2_b[l])
    return xf.reshape(B, T, D)
```

```python
import functools

import numpy as np
import jax
import jax.numpy as jnp
from jax import lax
from jax.experimental import pallas as pl
from jax.experimental.pallas import tpu as pltpu

F32 = jnp.float32
BF16 = jnp.bfloat16
HIGHEST = lax.Precision.HIGHEST

D_MODEL = 2048
DEPTH = 2
M_HEADS = 4
M_DQK = 256
M_DV = 512
N_KV = 4
N_HPG = 4
N_HEADS = N_KV * N_HPG
N_HD = 128
CMP_BLOCK = 32
CMP_STRIDE = 16
SEL_BLOCK = 64
SEL_TOPK = 16
WINDOW = 512
ROPE_THETA = 10000.0
N_EXPERTS = 32
N_GROUPS = 8
EXP_PER_GROUP = N_EXPERTS // N_GROUPS
D_FF = 1024
ALPHA = (2 * DEPTH) ** 0.25
EPS = 1e-5
NEG = -1e30

M_QK_W = M_HEADS * M_DQK
M_V_W = M_HEADS * M_DV
N_Q_W = N_HEADS * N_HD
N_KV_W = N_KV * N_HD
IN_SIZES = (M_QK_W, M_QK_W, M_V_W, M_V_W, M_HEADS, M_HEADS, N_Q_W, N_KV_W, N_KV_W, N_KV_W, N_KV_W,
            N_KV_W, N_KV_W, 3 * N_HEADS, 2 * D_MODEL)
IN_OFFS = tuple(int(v) for v in np.cumsum((0,) + IN_SIZES))

LANES = 128
MLSTM_CHUNK = 256
NSA_TQ = 256
NSA_GROUP = 4
NSA_ONES = 16
MM_TM = 512
MM_TN = 1024
MOE_TM = 256
MOE_TD = 256
ROUTER_TM = 512

P1_MQ, P1_MK, P1_MV, P1_MO, P1_NVS, P1_NVW, P1_GM = 0, 1024, 2048, 4096, 6144, 6656, 7168
P1_W = 7168 + 2 * D_MODEL
P2_NQ, P2_NKS, P2_NKW = 0, 2048, 2560
P2_W = 3072


def _params(sem, vmem_mb):
    return pltpu.CompilerParams(dimension_semantics=sem, vmem_limit_bytes=vmem_mb * 1024 * 1024)


def _nt(a, b, **kw):
    return lax.dot_general(a, b, (((1,), (1,)), ((), ())), preferred_element_type=F32, **kw)


def _tn(a, b):
    return lax.dot_general(a, b, (((0,), (0,)), ((), ())), preferred_element_type=F32)


def _standardize(x):
    mu = jnp.mean(x, -1, keepdims=True)
    xc = x - mu
    var = jnp.mean(xc * xc, -1, keepdims=True)
    return xc * lax.rsqrt(var + EPS)


def _mm_body(x_ref, w_ref, b_ref, *rest, rope, tn):
    acc = jnp.dot(x_ref[...], w_ref[...], preferred_element_type=F32) + b_ref[...]
    if rope:
        cos_ref, sin_ref, o_ref = rest
        cos = cos_ref[...]
        sin = sin_ref[...]
        for c in range(tn // N_HD):
            ch = acc[:, c * N_HD:(c + 1) * N_HD]
            o_ref[:, c * N_HD:(c + 1) * N_HD] = (ch * cos + pltpu.roll(ch, N_HD // 2, 1) * sin).astype(o_ref.dtype)
    else:
        (o_ref,) = rest
        o_ref[...] = acc.astype(o_ref.dtype)


def _matmul(x, w, b, out_dtype, tn, rope=None, seq=None):
    m, k = x.shape
    n = w.shape[1]
    tm = MM_TM
    grid = (n // tn, m // tm)
    in_specs = [pl.BlockSpec((tm, k), lambda j, i: (i, 0)),
                pl.BlockSpec((k, tn), lambda j, i: (0, j)),
                pl.BlockSpec((1, tn), lambda j, i: (0, j))]
    args = [x, w, b.reshape(1, n)]
    if rope is not None:
        nt = seq // tm
        in_specs += [pl.BlockSpec((tm, N_HD), lambda j, i: (i % nt, 0))] * 2
        args += list(rope)
    return pl.pallas_call(
        functools.partial(_mm_body, rope=rope is not None, tn=tn),
        grid=grid, in_specs=in_specs,
        out_specs=pl.BlockSpec((tm, tn), lambda j, i: (i, j)),
        out_shape=jax.ShapeDtypeStruct((m, n), out_dtype),
        compiler_params=_params(("parallel", "parallel"), 48),
        name="proj_matmul",
    )(*args)


def _log_sigmoid(x):
    return jnp.minimum(x, 0.0) - jnp.log1p(jnp.exp(-jnp.abs(x)))


def _mlstm_body(q_ref, k_ref, v_ref, og_ref, g_ref, gt_ref, nw_ref, out_ref, c_ref, n_ref, m_ref, *, L):
    @pl.when(pl.program_id(1) == 0)
    def _():
        c_ref[...] = jnp.zeros_like(c_ref)
        n_ref[...] = jnp.zeros_like(n_ref)
        m_ref[...] = jnp.zeros_like(m_ref)

    g = g_ref[...]
    gt = gt_ref[...]
    row = lax.broadcasted_iota(jnp.int32, (L, L), 0)
    col = lax.broadcasted_iota(jnp.int32, (L, L), 1)
    causal = row >= col
    b_all = jnp.dot(causal.astype(F32), _log_sigmoid(g), precision=HIGHEST, preferred_element_type=F32)
    bt_all = jnp.dot(_log_sigmoid(gt), (row <= col).astype(F32), precision=HIGHEST, preferred_element_type=F32)
    for h in range(M_HEADS):
        b_col = b_all[:, M_HEADS + h:M_HEADS + h + 1]
        li_col = g[:, h:h + 1]
        b_row = bt_all[M_HEADS + h:M_HEADS + h + 1, :]
        li_row = gt[h:h + 1, :]
        m_prev = m_ref[h][:, 0:1]
        dmat = jnp.where(causal, b_col - b_row + li_row, -jnp.inf)
        inter = b_col + m_prev
        m_t = jnp.maximum(inter, jnp.max(dmat, axis=1, keepdims=True))
        w_inter = jnp.exp(inter - m_t)
        q = q_ref[:, h * M_DQK:(h + 1) * M_DQK]
        k = k_ref[:, h * M_DQK:(h + 1) * M_DQK]
        v = v_ref[:, h * M_DV:(h + 1) * M_DV]
        s = _nt(q, k) * jnp.exp(dmat - m_t)
        ct = c_ref[h]
        n_row = n_ref[h]
        num = jnp.dot(s.astype(BF16), v, preferred_element_type=F32) + w_inter * jnp.dot(
            q, ct.astype(BF16), preferred_element_type=F32)
        qn = jnp.sum(q.astype(F32) * n_row, axis=1, keepdims=True)
        den = jnp.sum(s, axis=1, keepdims=True) + w_inter * qn
        hh = num / jnp.maximum(jnp.abs(den), jnp.exp(-m_t))
        hn = _standardize(hh) * nw_ref[:, h * M_DV:(h + 1) * M_DV]
        og = og_ref[:, h * M_DV:(h + 1) * M_DV].astype(F32)
        out_ref[:, h * M_DV:(h + 1) * M_DV] = (hn * jax.nn.sigmoid(og)).astype(out_ref.dtype)
        b_last = b_col[L - 1:L, :]
        g_col = b_last - b_col + li_col
        g_row = b_last - b_row + li_row
        m_new = jnp.maximum(b_last + m_prev, jnp.max(g_row, axis=1, keepdims=True))
        decay = jnp.exp(b_last + m_prev - m_new)
        ws_col = jnp.exp(g_col - m_new)
        kf = k.astype(F32)
        vw = (v.astype(F32) * ws_col).astype(BF16)
        c_ref[h] = decay * ct + _tn(k, vw)
        n_ref[h] = decay * n_row + jnp.sum(kf * ws_col, axis=0, keepdims=True)
        m_ref[h] = jnp.broadcast_to(m_new, (1, LANES))


def _mlstm(p1, p3, gt, norm_w, B, T):
    L = MLSTM_CHUNK
    p1 = p1.reshape(B, T, P1_W)
    return pl.pallas_call(
        functools.partial(_mlstm_body, L=L),
        grid=(B, T // L),
        in_specs=[pl.BlockSpec((None, L, M_QK_W), lambda b, c: (b, c, P1_MQ // M_QK_W)),
                  pl.BlockSpec((None, L, M_QK_W), lambda b, c: (b, c, P1_MK // M_QK_W)),
                  pl.BlockSpec((None, L, M_V_W), lambda b, c: (b, c, P1_MV // M_V_W)),
                  pl.BlockSpec((None, L, M_V_W), lambda b, c: (b, c, P1_MO // M_V_W)),
                  pl.BlockSpec((None, L, LANES), lambda b, c: (b, c, 0)),
                  pl.BlockSpec((None, 8, L), lambda b, c: (b, 0, c)),
                  pl.BlockSpec((1, M_V_W), lambda b, c: (0, 0))],
        out_specs=pl.BlockSpec((None, L, M_V_W), lambda b, c: (b, c, 0)),
        out_shape=jax.ShapeDtypeStruct((B, T, M_V_W), BF16),
        scratch_shapes=[pltpu.VMEM((M_HEADS, M_DQK, M_DV), F32),
                        pltpu.VMEM((M_HEADS, 1, M_DQK), F32),
                        pltpu.VMEM((M_HEADS, 1, LANES), F32)],
        compiler_params=_params(("parallel", "arbitrary"), 48),
        name="mlstm",
    )(p1, p1, p1, p1, p3, gt, norm_w.reshape(1, M_V_W))


def _compress_body(x_ref, pos_ref, w1_ref, b1_ref, w2_ref, b2_ref, cos_ref, sin_ref, o_ref, *, rope, nb):
    lo = jnp.zeros((nb, N_HD), F32)
    hi = jnp.zeros((nb, N_HD), F32)
    for l in range(CMP_STRIDE):
        xl = x_ref[pl.ds(l, nb, stride=CMP_STRIDE), :]
        lo = lo + jnp.dot((xl + pos_ref[l:l + 1, :]).astype(BF16), w1_ref[l], preferred_element_type=F32)
        hi = hi + jnp.dot((xl + pos_ref[CMP_STRIDE + l:CMP_STRIDE + l + 1, :]).astype(BF16),
                          w1_ref[CMP_STRIDE + l], preferred_element_type=F32)
    pre = lo + pltpu.roll(hi, nb - 1, 0) + b1_ref[...]
    h = jax.nn.gelu(pre)
    y = jnp.dot(h.astype(BF16), w2_ref[...], preferred_element_type=F32) + b2_ref[...]
    if rope:
        y = y * cos_ref[...] + pltpu.roll(y, N_HD // 2, 1) * sin_ref[...]
    keep = lax.broadcasted_iota(jnp.int32, (nb, N_HD), 0) < nb - 1
    o_ref[...] = jnp.where(keep, y, 0.0).astype(o_ref.dtype)


def _compress(pc, which, pos, w1, b1, w2, b2, cos, sin, rope):
    B, T, _ = pc.shape
    nb = T // CMP_STRIDE
    full = lambda shape: pl.BlockSpec(shape, lambda b, g: (0,) * len(shape))
    return pl.pallas_call(
        functools.partial(_compress_body, rope=rope, nb=nb),
        grid=(B, N_KV),
        in_specs=[pl.BlockSpec((None, T, N_HD), lambda b, g: (b, 0, which * N_KV + g)),
                  full((CMP_BLOCK, N_HD)), full((CMP_BLOCK, N_HD, N_HD)), full((1, N_HD)),
                  full((N_HD, N_HD)), full((1, N_HD)), full((nb, N_HD)), full((nb, N_HD))],
        out_specs=pl.BlockSpec((None, None, nb, N_HD), lambda b, g: (b, g, 0, 0)),
        out_shape=jax.ShapeDtypeStruct((B, N_KV, nb, N_HD), BF16),
        compiler_params=_params(("parallel", "parallel"), 32),
        name="nsa_compress",
    )(pc, pos, w1.astype(BF16), b1.reshape(1, N_HD), w2.astype(BF16), b2.reshape(1, N_HD), cos, sin)


def _nsa_body(q_ref, ks_ref, vst_ref, kw_ref, vwt_ref, kc_ref, vct_ref, gt_ref, ovt_ref, o_ref,
              acc_ref, m_ref, sel_ref, out_ref, s_ref, p_ref, *, tq, ncmp):
    qi = pl.program_id(2)
    t0 = qi * tq
    nsel = ovt_ref.shape[0]
    gates = jax.nn.sigmoid(gt_ref[...])
    q_heads = [q_ref[:, h * N_HD:(h + 1) * N_HD] for h in range(N_HPG)]

    kc = kc_ref[...]
    vct = vct_ref[...]
    n_idx = lax.broadcasted_iota(jnp.int32, (ncmp, tq), 0)
    t_cmp = t0 + lax.broadcasted_iota(jnp.int32, (ncmp, tq), 1)
    cmask = (n_idx * CMP_STRIDE + (CMP_BLOCK - 1) <= t_cmp) & (n_idx < ncmp - 1)
    psum = jnp.zeros((ncmp, tq), F32)
    for h in range(N_HPG):
        s_ref[0, h, :ncmp, :] = _nt(kc, q_heads[h])
    for h in range(N_HPG):
        s = jnp.where(cmask, s_ref[0, h, :ncmp, :], NEG)
        e = jnp.where(cmask, jnp.exp(s - jnp.max(s, axis=0, keepdims=True)), 0.0)
        p = e * (1.0 / jnp.maximum(jnp.sum(e, axis=0, keepdims=True), 1e-30))
        psum = psum + p
        p_ref[0, h, :ncmp, :] = p.astype(BF16)
    for h in range(N_HPG):
        out_ref[h] = gates[3 * h:3 * h + 1, :] * jnp.dot(vct, p_ref[0, h, :ncmp, :], preferred_element_type=F32)

    imp = jnp.dot(ovt_ref[...], psum, precision=HIGHEST, preferred_element_type=F32)
    j_idx = lax.broadcasted_iota(jnp.int32, (nsel, tq), 0)
    t_row = t0 + lax.broadcasted_iota(jnp.int32, (nsel, tq), 1)
    cur = lax.shift_right_logical(t_row, 6)
    forced = (j_idx == 0) | (j_idx == cur) | (j_idx == cur - 1)
    score = jnp.where(forced, jnp.inf, jnp.where(j_idx * SEL_BLOCK <= t_row, imp, -jnp.inf))

    def pick(_, carry):
        score, sel = carry
        mx = jnp.max(score, axis=0, keepdims=True)
        idx = jnp.min(jnp.where(score == mx, j_idx, nsel), axis=0, keepdims=True)
        hit = j_idx == idx
        return jnp.where(hit, -jnp.inf, score), jnp.where(hit, 1.0, sel)

    _, sel = lax.fori_loop(0, min(SEL_TOPK, nsel), pick, (score, jnp.zeros((nsel, tq), F32)), unroll=True)
    sel_ref[...] = sel

    def flash_init():
        acc_ref[...] = jnp.zeros_like(acc_ref)
        m_ref[...] = jnp.full(m_ref.shape, NEG, F32)

    def flash_tiles(k_ref, vt_ref, tiles):
        vts = []
        for n, (j, _) in enumerate(tiles):
            start = pl.multiple_of(j * tq, tq)
            k = k_ref[pl.ds(start, tq), :]
            vts.append(vt_ref[:, pl.ds(start, tq)])
            for h in range(N_HPG):
                s_ref[n, h] = _nt(k, q_heads[h])
        for n, (_, mask) in enumerate(tiles):
            alphas = []
            for h in range(N_HPG):
                s = s_ref[n, h]
                if mask is not None:
                    s = jnp.where(mask, s, NEG)
                m_old = m_ref[h]
                m_new = jnp.maximum(m_old, jnp.max(s, axis=0, keepdims=True))
                p_ref[n, h] = jnp.exp((s - m_new).astype(BF16))
                alpha = jnp.exp(m_old - m_new)
                m_ref[h] = m_new
                alphas.append(alpha)
            for h in range(N_HPG):
                acc_ref[h] = alphas[h] * acc_ref[h] + jnp.dot(vts[n], p_ref[n, h], preferred_element_type=F32)

    def flash_add(branch):
        for h in range(N_HPG):
            scale = gates[3 * h + branch:3 * h + branch + 1, :] / acc_ref[h, N_HD:N_HD + 1, :]
            out_ref[h] = out_ref[h] + scale * acc_ref[h, :N_HD, :]

    kpos = lax.broadcasted_iota(jnp.int32, (tq, tq), 0)
    qpos = lax.broadcasted_iota(jnp.int32, (tq, tq), 1)
    blk_per_tile = tq // SEL_BLOCK

    def sel_mask(j):
        rows = [jnp.broadcast_to(sel_ref[pl.ds(j * blk_per_tile + b, 1), :], (SEL_BLOCK, tq))
                for b in range(blk_per_tile)]
        return jnp.concatenate(rows, axis=0) > 0.5

    flash_init()

    group = s_ref.shape[0]

    def sel_group(p, carry):
        flash_tiles(ks_ref, vst_ref, [(group * p + n, sel_mask(group * p + n)) for n in range(group)])
        return carry

    lax.fori_loop(0, qi // group, sel_group, 0)
    causal = kpos <= qpos
    for rem in range(group):
        @pl.when(qi % group == rem)
        def _():
            tiles = [(qi - rem + n, sel_mask(qi - rem + n)) for n in range(rem)]
            flash_tiles(ks_ref, vst_ref, tiles + [(qi, sel_mask(qi) & causal)])

    flash_add(1)

    flash_init()
    n_back = WINDOW // tq
    win_tiles = [(qi, causal)] + [(qi - back, None) for back in range(1, n_back)] + [(qi - n_back, kpos > qpos)]
    for n_tiles in range(1, n_back + 2):
        last = n_tiles == n_back + 1

        @pl.when((qi >= n_tiles - 1) if last else (qi == n_tiles - 1))
        def _():
            flash_tiles(kw_ref, vwt_ref, win_tiles[:n_tiles])
    flash_add(2)

    for h in range(N_HPG):
        o_ref[:, h * N_HD:(h + 1) * N_HD] = out_ref[h].T.astype(o_ref.dtype)


def _nsa(p1, p2, gt, k_cmp, v_cmp, ovt, B, T):
    tq = NSA_TQ
    ncmp = k_cmp.shape[2]
    p1 = p1.reshape(B, T, P1_W)
    p2 = p2.reshape(B, T, P2_W)
    hw = N_HPG * N_HD
    assert WINDOW % tq == 0 and tq % SEL_BLOCK == 0 and ncmp <= tq and NSA_GROUP >= WINDOW // tq + 1
    ones = jnp.ones((B, N_KV, NSA_ONES, T), BF16)

    def values_t(off):
        vt = p1[:, :, off:off + N_KV_W].transpose(0, 2, 1).reshape(B, N_KV, N_HD, T)
        return jnp.concatenate([vt, ones], axis=2)

    vst = values_t(P1_NVS)
    vwt = values_t(P1_NVW)
    vct = v_cmp.transpose(0, 1, 3, 2)
    k_spec = lambda off: pl.BlockSpec((None, T, N_HD), lambda b, g, i: (b, 0, off // N_HD + g))
    vt_spec = pl.BlockSpec((None, None, N_HD + NSA_ONES, T), lambda b, g, i: (b, g, 0, 0))
    return pl.pallas_call(
        functools.partial(_nsa_body, tq=tq, ncmp=ncmp),
        grid=(B, N_KV, T // tq),
        in_specs=[pl.BlockSpec((None, tq, hw), lambda b, g, i: (b, i, P2_NQ // hw + g)),
                  k_spec(P2_NKS), vt_spec, k_spec(P2_NKW), vt_spec,
                  pl.BlockSpec((None, None, ncmp, N_HD), lambda b, g, i: (b, g, 0, 0)),
                  pl.BlockSpec((None, None, N_HD, ncmp), lambda b, g, i: (b, g, 0, 0)),
                  pl.BlockSpec((None, None, 16, tq), lambda b, g, i: (b, g, 0, i)),
                  pl.BlockSpec(ovt.shape, lambda b, g, i: (0, 0))],
        out_specs=pl.BlockSpec((None, tq, hw), lambda b, g, i: (b, i, g)),
        out_shape=jax.ShapeDtypeStruct((B, T, N_Q_W), BF16),
        scratch_shapes=[pltpu.VMEM((N_HPG, N_HD + NSA_ONES, tq), F32),
                        pltpu.VMEM((N_HPG, 1, tq), F32), pltpu.VMEM((ovt.shape[0], tq), F32),
                        pltpu.VMEM((N_HPG, N_HD, tq), F32),
                        pltpu.VMEM((NSA_GROUP, N_HPG, tq, tq), F32),
                        pltpu.VMEM((NSA_GROUP, N_HPG, tq, tq), BF16)],
        compiler_params=_params(("parallel", "parallel", "arbitrary"), 48),
        name="nsa_attention",
    )(p2, p2, vst, p2, vwt, k_cmp, vct, gt, ovt)


def _merge_body(hm_ref, hn_ref, wm_ref, wn_ref, gm_ref, gn_ref, o_ref):
    ym = jnp.dot(hm_ref[...], wm_ref[...], preferred_element_type=F32)
    yn = jnp.dot(hn_ref[...], wn_ref[...], preferred_element_type=F32)
    gm = jax.nn.sigmoid(gm_ref[...].astype(F32))
    gn = jax.nn.sigmoid(gn_ref[...].astype(F32))
    o_ref[...] = (gm * ym + gn * yn).astype(o_ref.dtype)


def _merge(hm, hn, wm, wn, p1):
    m = hm.shape[0]
    tm, tn = MM_TM, 512
    return pl.pallas_call(
        _merge_body,
        grid=(D_MODEL // tn, m // tm),
        in_specs=[pl.BlockSpec((tm, M_V_W), lambda j, i: (i, 0)),
                  pl.BlockSpec((tm, N_Q_W), lambda j, i: (i, 0)),
                  pl.BlockSpec((M_V_W, tn), lambda j, i: (0, j)),
                  pl.BlockSpec((N_Q_W, tn), lambda j, i: (0, j)),
                  pl.BlockSpec((tm, tn), lambda j, i: (i, P1_GM // tn + j)),
                  pl.BlockSpec((tm, tn), lambda j, i: (i, (P1_GM + D_MODEL) // tn + j))],
        out_specs=pl.BlockSpec((tm, tn), lambda j, i: (i, j)),
        out_shape=jax.ShapeDtypeStruct((m, D_MODEL), BF16),
        compiler_params=_params(("parallel", "parallel"), 48),
        name="branch_merge",
    )(hm, hn, wm, wn, p1, p1)


def _out_ln_body(y_ref, w_ref, x_ref, g_ref, b_ref, o_ref, ob_ref):
    mix = jnp.dot(y_ref[...], w_ref[...], preferred_element_type=F32)
    out = _standardize(ALPHA * x_ref[...] + mix) * g_ref[...] + b_ref[...]
    o_ref[...] = out
    ob_ref[...] = out.astype(BF16)


def _out_ln(y, w, x, g, b):
    m = y.shape[0]
    tm = MM_TM
    row = pl.BlockSpec((tm, D_MODEL), lambda i: (i, 0))
    vec = pl.BlockSpec((1, D_MODEL), lambda i: (0, 0))
    return pl.pallas_call(
        _out_ln_body,
        grid=(m // tm,),
        in_specs=[row, pl.BlockSpec((D_MODEL, D_MODEL), lambda i: (0, 0)), row, vec, vec],
        out_specs=[row, row],
        out_shape=[jax.ShapeDtypeStruct((m, D_MODEL), F32), jax.ShapeDtypeStruct((m, D_MODEL), BF16)],
        compiler_params=_params(("parallel",), 48),
        name="out_proj_layernorm",
    )(y, w, x, g.reshape(1, -1), b.reshape(1, -1))


def _first_of4(vals, target):
    return jnp.where(vals[0] == target, 0.0, jnp.where(vals[1] == target, 1.0, jnp.where(vals[2] == target, 2.0, 3.0)))


def _select4(idx, vals):
    return jnp.where(idx == 0.0, vals[0], jnp.where(idx == 1.0, vals[1], jnp.where(idx == 2.0, vals[2], vals[3])))


def _router_body(x_ref, rw_ref, rwl_ref, rb_ref, tri_ref, o_ref, cnt_ref, carry_ref, *, tm):
    @pl.when(pl.program_id(0) == 0)
    def _():
        carry_ref[...] = jnp.zeros_like(carry_ref)

    x = x_ref[...]
    x_hi = x.astype(BF16)
    x_lo = (x - x_hi.astype(F32)).astype(BF16)
    logits = _nt(rw_ref[...], x_hi) + _nt(rw_ref[...], x_lo) + _nt(rwl_ref[...], x_hi)
    aff = jax.nn.sigmoid(logits)
    biased = aff + rb_ref[:, 0:1]
    a = [biased[i * N_GROUPS:(i + 1) * N_GROUPS, :] for i in range(EXP_PER_GROUP)]
    af = [aff[i * N_GROUPS:(i + 1) * N_GROUPS, :] for i in range(EXP_PER_GROUP)]
    m1 = jnp.maximum(jnp.maximum(a[0], a[1]), jnp.maximum(a[2], a[3]))
    i1 = _first_of4(a, m1)
    rest = [jnp.where(i1 == float(i), -jnp.inf, a[i]) for i in range(EXP_PER_GROUP)]
    m2 = jnp.maximum(jnp.maximum(rest[0], rest[1]), jnp.maximum(rest[2], rest[3]))
    i2 = _first_of4(rest, m2)
    gscore = m1 + m2
    g_iota = lax.broadcasted_iota(jnp.int32, (N_GROUPS, tm), 0).astype(F32)
    g_idx = jnp.min(jnp.where(gscore == jnp.max(gscore, axis=0, keepdims=True), g_iota, float(N_GROUPS)),
                    axis=0, keepdims=True)
    in_g = g_iota == g_idx
    take = lambda v: jnp.sum(jnp.where(in_g, v, 0.0), axis=0, keepdims=True)
    s0 = take(i1)
    s1 = take(i2)
    w0 = take(_select4(i1, af))
    w1 = take(_select4(i2, af))
    wsum = w0 + w1
    r0 = s0 * N_GROUPS + g_idx
    r1 = s1 * N_GROUPS + g_idx
    r_iota = lax.broadcasted_iota(jnp.int32, (N_EXPERTS, tm), 0).astype(F32)
    member = (r_iota == r0) | (r_iota == r1)
    before = jnp.dot(member.astype(BF16), tri_ref[...], preferred_element_type=F32)
    base = before + carry_ref[:, 0:1]
    rank0 = jnp.sum(jnp.where(r_iota == r0, base, 0.0), axis=0, keepdims=True)
    rank1 = jnp.sum(jnp.where(r_iota == r1, base, 0.0), axis=0, keepdims=True)
    new_carry = carry_ref[:, 0:1] + jnp.sum(member.astype(F32), axis=1, keepdims=True)
    carry_ref[...] = jnp.broadcast_to(new_carry, carry_ref.shape)
    cnt_ref[...] = jnp.broadcast_to(new_carry, cnt_ref.shape)
    o_ref[0:1, :] = g_idx * EXP_PER_GROUP + s0
    o_ref[1:2, :] = g_idx * EXP_PER_GROUP + s1
    o_ref[2:3, :] = w0 / wsum
    o_ref[3:4, :] = w1 / wsum
    o_ref[4:5, :] = rank0
    o_ref[5:6, :] = rank1
    o_ref[6:8, :] = jnp.zeros((2, tm), F32)


def _router(x, rw_t, rb):
    m = x.shape[0]
    tm = ROUTER_TM
    rw_hi = rw_t.astype(BF16)
    rw_lo = (rw_t - rw_hi.astype(F32)).astype(BF16)
    tri = jnp.asarray(np.triu(np.ones((tm, tm), np.float32), 1), BF16)
    w_spec = pl.BlockSpec((N_EXPERTS, D_MODEL), lambda i: (0, 0))
    return pl.pallas_call(
        functools.partial(_router_body, tm=tm),
        grid=(m // tm,),
        in_specs=[pl.BlockSpec((tm, D_MODEL), lambda i: (i, 0)), w_spec, w_spec,
                  pl.BlockSpec((N_EXPERTS, LANES), lambda i: (0, 0)),
                  pl.BlockSpec((tm, tm), lambda i: (0, 0))],
        out_specs=[pl.BlockSpec((8, tm), lambda i: (0, i)),
                   pl.BlockSpec((N_EXPERTS, LANES), lambda i: (0, 0))],
        out_shape=[jax.ShapeDtypeStruct((8, m), F32), jax.ShapeDtypeStruct((N_EXPERTS, LANES), F32)],
        scratch_shapes=[pltpu.VMEM((N_EXPERTS, LANES), F32)],
        compiler_params=_params(("arbitrary",), 48),
        name="moe_router",
    )(x, rw_hi, rw_lo, rb, tri)


def _row_copy(src_ref, src_row, dst_ref, dst_row, sem):
    return pltpu.make_async_copy(src_ref.at[pl.ds(src_row, 1), :], dst_ref.at[pl.ds(dst_row, 1), :], sem)


def _dispatch_body(dest_ref, x_ref, init_ref, xs_ref, sem, *, td):
    del init_ref

    def issue(r, c):
        _row_copy(x_ref, r, xs_ref, dest_ref[0, 0, r], sem).start(priority=0)
        _row_copy(x_ref, r, xs_ref, dest_ref[0, 0, td + r], sem).start(priority=1)
        return c

    lax.fori_loop(0, td, issue, 0)
    for _ in range(2):
        pltpu.make_async_copy(x_ref, xs_ref.at[pl.ds(0, td), :], sem).wait()


def _dispatch(x, dest, init):
    rows = init.shape[0]
    m = x.shape[0]
    td = MOE_TD
    return pl.pallas_call(
        functools.partial(_dispatch_body, td=td),
        grid=(m // td,),
        in_specs=[pl.BlockSpec((1, 1, 2 * td), lambda i: (i, 0, 0), memory_space=pltpu.SMEM),
                  pl.BlockSpec((td, D_MODEL), lambda i: (i, 0)),
                  pl.BlockSpec(memory_space=pl.ANY)],
        out_specs=pl.BlockSpec(memory_space=pl.ANY),
        out_shape=jax.ShapeDtypeStruct((rows, D_MODEL), F32),
        scratch_shapes=[pltpu.SemaphoreType.DMA(())],
        input_output_aliases={2: 0},
        compiler_params=_params(("arbitrary",), 32),
        name="moe_dispatch",
    )(dest, x, init)


def _expert_changed(te_ref, i):
    return (i == 0) | (te_ref[i] != te_ref[jnp.maximum(i - 1, 0)])


def _experts_up_body(te_ref, nused_ref, x_ref, w1_ref, w3_ref, h_ref, w1b_ref, w3b_ref):
    i = pl.program_id(0)
    used = i < nused_ref[0]

    @pl.when(used & _expert_changed(te_ref, i))
    def _():
        w1b_ref[...] = w1_ref[...].astype(BF16)
        w3b_ref[...] = w3_ref[...].astype(BF16)

    @pl.when(used)
    def _():
        xb = x_ref[...].astype(BF16)
        a = jnp.dot(xb, w1b_ref[...], preferred_element_type=F32)
        b = jnp.dot(xb, w3b_ref[...], preferred_element_type=F32)
        h_ref[...] = (a * jax.nn.sigmoid(a) * b).astype(BF16)

    @pl.when(jnp.logical_not(used))
    def _():
        h_ref[...] = jnp.zeros_like(h_ref)


def _experts_down_body(te_ref, nused_ref, h_ref, w2_ref, o_ref, w2b_ref):
    i = pl.program_id(0)
    used = i < nused_ref[0]

    @pl.when(used & _expert_changed(te_ref, i))
    def _():
        w2b_ref[...] = w2_ref[...].astype(BF16)

    @pl.when(used)
    def _():
        o_ref[...] = jnp.dot(h_ref[...], w2b_ref[...], preferred_element_type=F32)

    @pl.when(jnp.logical_not(used))
    def _():
        o_ref[...] = jnp.zeros_like(o_ref)


def _experts(xs, w1, w3, w2, layer, tile_expert, n_used):
    rows = xs.shape[0]
    tm = MOE_TM
    by_expert = lambda r, c: pl.BlockSpec((None, None, r, c), lambda i, te, nu: (layer, te[i], 0, 0))
    by_tile = lambda c: pl.BlockSpec((tm, c), lambda i, te, nu: (i, 0))
    h = pl.pallas_call(
        _experts_up_body,
        grid_spec=pltpu.PrefetchScalarGridSpec(
            num_scalar_prefetch=2, grid=(rows // tm,),
            in_specs=[by_tile(D_MODEL), by_expert(D_MODEL, D_FF), by_expert(D_MODEL, D_FF)],
            out_specs=by_tile(D_FF),
            scratch_shapes=[pltpu.VMEM((D_MODEL, D_FF), BF16), pltpu.VMEM((D_MODEL, D_FF), BF16)]),
        out_shape=jax.ShapeDtypeStruct((rows, D_FF), BF16),
        compiler_params=_params(("arbitrary",), 56),
        name="moe_experts_up",
    )(tile_expert, n_used, xs, w1, w3)
    return pl.pallas_call(
        _experts_down_body,
        grid_spec=pltpu.PrefetchScalarGridSpec(
            num_scalar_prefetch=2, grid=(rows // tm,),
            in_specs=[by_tile(D_FF), by_expert(D_FF, D_MODEL)],
            out_specs=by_tile(D_MODEL),
            scratch_shapes=[pltpu.VMEM((D_FF, D_MODEL), BF16)]),
        out_shape=jax.ShapeDtypeStruct((rows, D_MODEL), F32),
        compiler_params=_params(("arbitrary",), 40),
        name="moe_experts_down",
    )(tile_expert, n_used, h, w2)


def _combine_body(dest_ref, next_ref, ys_ref, x_ref, w_ref, g_ref, b_ref, o_ref, ob_ref, buf_ref, sem, *, td):
    i = pl.program_id(0)
    slot = i % 2

    def gather(idx_ref, s):
        def issue(r, c):
            _row_copy(ys_ref, idx_ref[0, 0, r], buf_ref.at[s, 0], r, sem.at[s]).start(priority=0)
            _row_copy(ys_ref, idx_ref[0, 0, td + r], buf_ref.at[s, 1], r, sem.at[s]).start(priority=1)
            return c

        lax.fori_loop(0, td, issue, 0)

    @pl.when(i == 0)
    def _():
        gather(dest_ref, 0)

    @pl.when(i + 1 < pl.num_programs(0))
    def _():
        gather(next_ref, 1 - slot)

    for half in range(2):
        pltpu.make_async_copy(ys_ref.at[pl.ds(0, td), :], buf_ref.at[slot, half], sem.at[slot]).wait()
    w = w_ref[...]
    ff = w[:, 0:1] * buf_ref[slot, 0] + w[:, 1:2] * buf_ref[slot, 1]
    out = _standardize(ALPHA * x_ref[...] + ff) * g_ref[...] + b_ref[...]
    o_ref[...] = out
    ob_ref[...] = out.astype(BF16)


def _combine(ys, dest, x, w, g, b):
    m = x.shape[0]
    td = MOE_TD
    row = pl.BlockSpec((td, D_MODEL), lambda i: (i, 0))
    vec = pl.BlockSpec((1, D_MODEL), lambda i: (0, 0))
    last = m // td - 1
    return pl.pallas_call(
        functools.partial(_combine_body, td=td),
        grid=(m // td,),
        in_specs=[pl.BlockSpec((1, 1, 2 * td), lambda i: (i, 0, 0), memory_space=pltpu.SMEM),
                  pl.BlockSpec((1, 1, 2 * td), lambda i: (jnp.minimum(i + 1, last), 0, 0), memory_space=pltpu.SMEM),
                  pl.BlockSpec(memory_space=pl.ANY), row,
                  pl.BlockSpec((td, 8), lambda i: (i, 0)), vec, vec],
        out_specs=[row, row],
        out_shape=[jax.ShapeDtypeStruct((m, D_MODEL), F32), jax.ShapeDtypeStruct((m, D_MODEL), BF16)],
        scratch_shapes=[pltpu.VMEM((2, 2, td, D_MODEL), F32), pltpu.SemaphoreType.DMA((2,))],
        compiler_params=_params(("arbitrary",), 40),
        name="moe_combine",
    )(dest, dest, ys, x, w, g.reshape(1, -1), b.reshape(1, -1))


def _moe(x, sorted_buf, rw_t, rb, w1, w3, w2, layer, ln_g, ln_b):
    m = x.shape[0]
    tm, td = MOE_TM, MOE_TD
    rows = sorted_buf.shape[0]
    ro, cnt = _router(x, rw_t, rb)
    e0 = ro[0].astype(jnp.int32)
    e1 = ro[1].astype(jnp.int32)
    counts = cnt[:, 0].astype(jnp.int32).reshape(EXP_PER_GROUP, N_GROUPS).T.reshape(N_EXPERTS)
    padded = (counts + tm - 1) // tm * tm
    ends = jnp.cumsum(padded)
    offs = ends - padded
    d0 = offs[e0] + ro[4].astype(jnp.int32)
    d1 = offs[e1] + ro[5].astype(jnp.int32)
    dest = jnp.concatenate([d0.reshape(m // td, 1, td), d1.reshape(m // td, 1, td)], axis=-1)
    tile_start = jnp.arange(rows // tm, dtype=jnp.int32) * tm
    tile_expert = jnp.minimum(jnp.sum(tile_start[:, None] >= ends[None, :], axis=1), N_EXPERTS - 1).astype(jnp.int32)
    n_used = (ends[-1:] // tm).astype(jnp.int32)
    xs = _dispatch(x, dest, sorted_buf)
    ys = _experts(xs, w1, w3, w2, layer, tile_expert, n_used)
    wcol = jnp.pad(ro[2:4].T, ((0, 0), (0, 6)))
    out, out_bf16 = _combine(ys, dest, x, wcol, ln_g, ln_b)
    return out, out_bf16, xs


def _rope_tables(pos):
    half = N_HD // 2
    inv = ROPE_THETA ** (-jnp.arange(half, dtype=F32) / half)
    ang = pos.astype(F32)[:, None] * inv[None, :]
    cos, sin = jnp.cos(ang), jnp.sin(ang)
    return jnp.concatenate([cos, cos], -1), jnp.concatenate([-sin, sin], -1)


def _overlap_t(T, ncmp_pad):
    n_cmp = (T - CMP_BLOCK) // CMP_STRIDE + 1
    n_sel = T // SEL_BLOCK
    cs = np.arange(n_cmp) * CMP_STRIDE
    ss = np.arange(n_sel) * SEL_BLOCK
    ov = ((cs[:, None] <= ss[None, :] + SEL_BLOCK - 1) & (cs[:, None] + CMP_BLOCK - 1 >= ss[None, :])).astype(np.float32)
    out = np.zeros((n_sel, ncmp_pad), np.float32)
    out[:, :n_cmp] = ov.T
    return jnp.asarray(out)


def _pack_body(w_ref, o1_ref, o2_ref, oc_ref, og_ref):
    (mq, mk, mv, mo, mi, mf, nq, nkc, nvc, nks, nvs, nkw, nvw, ng, gm, end) = IN_OFFS

    def put(o_ref, off, a, b, scale=1.0):
        o_ref[:, off:off + b - a] = (w_ref[:, a:b] * scale).astype(o_ref.dtype)

    put(o1_ref, P1_MQ, mq, mk, M_DQK ** -0.5)
    put(o1_ref, P1_MK, mk, mo)
    put(o1_ref, P1_MO, mo, mi)
    put(o1_ref, P1_NVS, nvs, nkw)
    put(o1_ref, P1_NVW, nvw, ng)
    put(o1_ref, P1_GM, gm, end)
    put(o2_ref, P2_NQ, nq, nkc, N_HD ** -0.5)
    put(o2_ref, P2_NKS, nks, nvs)
    put(o2_ref, P2_NKW, nkw, nvw)
    put(oc_ref, 0, nkc, nks)
    og_ref[...] = jnp.zeros(og_ref.shape, og_ref.dtype)
    put(og_ref, 0, mi, nq)
    put(og_ref, nq - mi, ng, gm)


def _pack(w, layer, out_dtype, tr):
    rows = w.shape[1]
    widths = (P1_W, P2_W, 2 * N_KV_W, LANES)
    return pl.pallas_call(
        _pack_body,
        grid=(rows // tr,),
        in_specs=[pl.BlockSpec((None, tr, IN_OFFS[-1]), lambda i: (layer, i, 0))],
        out_specs=[pl.BlockSpec((tr, n), lambda i: (i, 0)) for n in widths],
        out_shape=[jax.ShapeDtypeStruct((rows, n), out_dtype) for n in widths],
        compiler_params=_params(("parallel",), 56),
        name="pack_in_proj",
    )(w)


def kernel(x, w_in, b_in, m_norm_w, cmp_pos_k, cmp_w1_k, cmp_b1_k, cmp_w2_k, cmp_b2_k, cmp_pos_v, cmp_w1_v,
           cmp_b1_v, cmp_w2_v, cmp_b2_v, w_branch_m, w_branch_n, w_out, ln1_g, ln1_b, router_w, router_b,
           exp_w1, exp_w3, exp_w2, ln2_g, ln2_b):
    B, T, D = x.shape
    m = B * T
    nb = T // CMP_STRIDE
    cos_t, sin_t = _rope_tables(jnp.arange(T))
    cos_c, sin_c = _rope_tables(jnp.arange(nb) * CMP_STRIDE + CMP_BLOCK - 1)
    ovt = _overlap_t(T, nb)
    perm = (np.arange(N_GROUPS)[None, :] * EXP_PER_GROUP + np.arange(EXP_PER_GROUP)[:, None]).reshape(-1)
    rw_t = router_w.T[perm]
    rb = jnp.broadcast_to(router_b[perm][:, None], (N_EXPERTS, LANES))

    (mi, nq, ng, gm) = (IN_OFFS[4], IN_OFFS[6], IN_OFFS[13], IN_OFFS[14])
    n_gate = (nq - mi) + (gm - ng)
    b_rows = jnp.pad(b_in[:, None, :], ((0, 0), (0, 7), (0, 0)))
    xf = x.reshape(m, D)
    xb = xf.astype(BF16)
    sorted_buf = jnp.zeros((2 * m + N_EXPERTS * MOE_TM, D), F32)
    for l in range(DEPTH):
        w1p, w2p, wcp, wgp = _pack(w_in, l, BF16, 256)
        b1p, b2p, bcp, bgp = (b[0] for b in _pack(b_rows, l, F32, 8))
        p1 = _matmul(xb, w1p, b1p, BF16, MM_TN)
        p2 = _matmul(xb, w2p, b2p, BF16, MM_TN, rope=(cos_t, sin_t), seq=T)
        pc = _matmul(xb, wcp, bcp, F32, MM_TN)
        pg = _matmul(xb, wgp, bgp, F32, LANES).reshape(B, T, LANES)
        gt_m = pg[:, :, :nq - mi].transpose(0, 2, 1)
        gt_n = jnp.pad(pg[:, :, nq - mi:n_gate].reshape(B, T, N_KV, 3 * N_HPG).transpose(0, 2, 3, 1),
                       ((0, 0), (0, 0), (0, 16 - 3 * N_HPG), (0, 0)))
        h_m = _mlstm(p1, pg, gt_m, m_norm_w[l], B, T).reshape(m, M_V_W)

        pc = pc.reshape(B, T, 2 * N_KV_W)
        k_cmp = _compress(pc, 0, cmp_pos_k[l], cmp_w1_k[l], cmp_b1_k[l], cmp_w2_k[l], cmp_b2_k[l],
                          cos_c, sin_c, True)
        v_cmp = _compress(pc, 1, cmp_pos_v[l], cmp_w1_v[l], cmp_b1_v[l], cmp_w2_v[l], cmp_b2_v[l],
                          cos_c, sin_c, False)
        h_n = _nsa(p1, p2, gt_n, k_cmp, v_cmp, ovt, B, T).reshape(m, N_Q_W)
        merged = _merge(h_m, h_n, w_branch_m[l].astype(BF16), w_branch_n[l].astype(BF16), p1)
        xf, xb = _out_ln(merged, w_out[l].astype(BF16), xf, ln1_g[l], ln1_b[l])
        xf, xb, sorted_buf = _moe(xf, sorted_buf, rw_t, rb, exp_w1, exp_w3, exp_w2, l, ln2_g[l], ln2_b[l])
    return xf.reshape(B, T, D)
```
